```python
import math
import jax, jax.numpy as jnp
from jax import lax
import numpy as np

D_MODEL = 1024
BATCH = 8
SEQ = 8192
DEPTH = 1

HEAD_DIM = 64
ATTN_WIDTH = D_MODEL // 2
CONV_WIDTH = D_MODEL - ATTN_WIDTH
N_ATTN_HEADS = ATTN_WIDTH // HEAD_DIM
CONV_GROUPS = CONV_WIDTH // HEAD_DIM
CONV_KERNEL = 31
D_FF = 4 * D_MODEL
Q_BLOCK = 128
LN_EPS = 1e-5
DEEPNORM_ALPHA = (2.0 * DEPTH) ** 0.25
DEEPNORM_BETA = (8.0 * DEPTH) ** -0.25
IN_SPLITS = [ATTN_WIDTH, 2 * ATTN_WIDTH, 3 * ATTN_WIDTH, 3 * ATTN_WIDTH + N_ATTN_HEADS,
             3 * ATTN_WIDTH + N_ATTN_HEADS + CONV_WIDTH]
N_IN_COLS = 3 * ATTN_WIDTH + N_ATTN_HEADS + 2 * CONV_WIDTH

kernel_name = "hymba_fox_conformer_deepnorm_adaln_block"


def _layernorm(x, g, b):
    xf = x.astype(jnp.float32)
    mu = jnp.mean(xf, axis=-1, keepdims=True)
    var = jnp.mean(jnp.square(xf - mu), axis=-1, keepdims=True)
    return ((xf - mu) * lax.rsqrt(var + LN_EPS)).astype(x.dtype) * g + b


def _rmsnorm(x, g):
    xf = x.astype(jnp.float32)
    return (xf * lax.rsqrt(jnp.mean(xf * xf, axis=-1, keepdims=True) + LN_EPS)).astype(x.dtype) * g


def _forgetting_attention(q, k, v, log_f):
    b, h, s, dh = q.shape
    n_blk = s // Q_BLOCK
    cum = jnp.cumsum(log_f, axis=-1)
    q_blocks = (q * (dh ** -0.5)).reshape(b, h, n_blk, Q_BLOCK, dh).transpose(2, 0, 1, 3, 4)
    cq_blocks = cum.reshape(b, h, n_blk, Q_BLOCK).transpose(2, 0, 1, 3)
    k_pos = jnp.arange(s)

    def one_block(args):
        qb, cqb, blk = args
        q_pos = blk * Q_BLOCK + jnp.arange(Q_BLOCK)
        logits = jnp.einsum('bhqd,bhkd->bhqk', qb, k).astype(jnp.float32)
        logits = logits + cqb[..., :, None] - cum[..., None, :]
        causal = k_pos[None, :] <= q_pos[:, None]
        logits = jnp.where(causal, logits, -jnp.inf)
        p = jax.nn.softmax(logits, axis=-1)
        return jnp.einsum('bhqk,bhkd->bhqd', p.astype(v.dtype), v)

    out = lax.map(one_block, (q_blocks, cq_blocks, jnp.arange(n_blk)))
    return out.transpose(1, 2, 0, 3, 4).reshape(b, h, s, dh)


def _conformer_conv(a, gate, w_dw, b_dw, gn_g, gn_b):
    u = a * jax.nn.sigmoid(gate)
    y = lax.conv_general_dilated(u, w_dw, window_strides=(1,), padding=((CONV_KERNEL - 1, 0),),
                                 dimension_numbers=('NWC', 'WIO', 'NWC'),
                                 feature_group_count=CONV_WIDTH) + b_dw
    bsz, seq, ch = y.shape
    yg = y.reshape(bsz, seq, CONV_GROUPS, ch // CONV_GROUPS).astype(jnp.float32)
    mu = jnp.mean(yg, axis=-1, keepdims=True)
    var = jnp.mean(jnp.square(yg - mu), axis=-1, keepdims=True)
    yn = ((yg - mu) * lax.rsqrt(var + LN_EPS)).reshape(bsz, seq, ch).astype(y.dtype) * gn_g + gn_b
    return jax.nn.silu(yn)


def _fwd_setup_inputs(seed: int = 0) -> dict:
    key = jax.random.key(seed)
    ks = jax.random.split(key, 24)
    nrm = jax.random.normal
    d = D_MODEL
    x = nrm(ks[0], (BATCH, SEQ, d), jnp.float32)
    c = nrm(ks[1], (BATCH, d), jnp.float32)
    w_ada = nrm(ks[2], (DEPTH, d, 6 * d), jnp.float32) * (0.1 * d ** -0.5)
    b_ada = nrm(ks[3], (DEPTH, 6 * d), jnp.float32) * 0.02
    col_scale = jnp.concatenate([
        jnp.ones((2 * ATTN_WIDTH,), jnp.float32),
        jnp.full((ATTN_WIDTH,), DEEPNORM_BETA, jnp.float32),
        jnp.ones((N_ATTN_HEADS,), jnp.float32),
        jnp.full((CONV_WIDTH,), DEEPNORM_BETA, jnp.float32),
        jnp.ones((CONV_WIDTH,), jnp.float32)])
    w_in = nrm(ks[4], (DEPTH, d, N_IN_COLS), jnp.float32) * (d ** -0.5) * col_scale
    b_forget = jax.random.uniform(ks[5], (DEPTH, N_ATTN_HEADS), jnp.float32, minval=2.0, maxval=6.0)
    w_dw = nrm(ks[6], (DEPTH, CONV_KERNEL, 1, CONV_WIDTH), jnp.float32) * (CONV_KERNEL ** -0.5)
    b_dw = nrm(ks[7], (DEPTH, CONV_WIDTH), jnp.float32) * 0.02
    gn_g = 1.0 + 0.02 * nrm(ks[8], (DEPTH, CONV_WIDTH), jnp.float32)
    gn_b = 0.02 * nrm(ks[9], (DEPTH, CONV_WIDTH), jnp.float32)
    g_attn_out = 1.0 + 0.02 * nrm(ks[10], (DEPTH, ATTN_WIDTH), jnp.float32)
    g_conv_out = 1.0 + 0.02 * nrm(ks[11], (DEPTH, CONV_WIDTH), jnp.float32)
    w_out = nrm(ks[12], (DEPTH, d, d), jnp.float32) * (d ** -0.5) * DEEPNORM_BETA
    ln1_g = 1.0 + 0.02 * nrm(ks[13], (DEPTH, d), jnp.float32)
    ln1_b = 0.02 * nrm(ks[14], (DEPTH, d), jnp.float32)
    w_ff1 = nrm(ks[15], (DEPTH, d, D_FF), jnp.float32) * (d ** -0.5) * DEEPNORM_BETA
    w_ff2 = nrm(ks[16], (DEPTH, D_FF, d), jnp.float32) * (D_FF ** -0.5) * DEEPNORM_BETA
    ln2_g = 1.0 + 0.02 * nrm(ks[17], (DEPTH, d), jnp.float32)
    ln2_b = 0.02 * nrm(ks[18], (DEPTH, d), jnp.float32)
    return {"x": x, "c": c, "w_ada": w_ada, "b_ada": b_ada, "w_in": w_in, "b_forget": b_forget,
            "w_dw": w_dw, "b_dw": b_dw, "gn_g": gn_g, "gn_b": gn_b, "g_attn_out": g_attn_out,
            "g_conv_out": g_conv_out, "w_out": w_out, "ln1_g": ln1_g, "ln1_b": ln1_b,
            "w_ff1": w_ff1, "w_ff2": w_ff2, "ln2_g": ln2_g, "ln2_b": ln2_b}


def _fwd_reference(x, c, w_ada, b_ada, w_in, b_forget, w_dw, b_dw, gn_g, gn_b, g_attn_out,
              g_conv_out, w_out, ln1_g, ln1_b, w_ff1, w_ff2, ln2_g, ln2_b):
    bsz, seq, _ = x.shape
    for layer in range(DEPTH):
        ada = jax.nn.silu(c) @ w_ada[layer] + b_ada[layer]
        sh1, sc1, gt1, sh2, sc2, gt2 = jnp.split(ada[:, None, :], 6, axis=-1)

        u = x * (1 + sc1) + sh1
        proj = u @ w_in[layer]
        q, k, v, f_logit, a, g = jnp.split(proj, IN_SPLITS, axis=-1)

        def heads(t):
            return t.reshape(bsz, seq, N_ATTN_HEADS, HEAD_DIM).transpose(0, 2, 1, 3)

        log_f = jax.nn.log_sigmoid((f_logit + b_forget[layer]).astype(jnp.float32)).transpose(0, 2, 1)
        attn = _forgetting_attention(heads(q), heads(k), heads(v), log_f)
        attn = attn.transpose(0, 2, 1, 3).reshape(bsz, seq, ATTN_WIDTH)
        conv = _conformer_conv(a, g, w_dw[layer], b_dw[layer], gn_g[layer], gn_b[layer])

        mixed = jnp.concatenate([_rmsnorm(attn, g_attn_out[layer]),
                                 _rmsnorm(conv, g_conv_out[layer])], axis=-1) @ w_out[layer]
        x = _layernorm(DEEPNORM_ALPHA * x + (1 + gt1) * mixed, ln1_g[layer], ln1_b[layer])

        u2 = x * (1 + sc2) + sh2
        hid = jnp.square(jax.nn.relu(u2 @ w_ff1[layer]))
        ff = hid @ w_ff2[layer]
        x = _layernorm(DEEPNORM_ALPHA * x + (1 + gt2) * ff, ln2_g[layer], ln2_b[layer])
    return x


import jax as _jax
import jax.numpy as _jnp

TWIN_FORMAT = 'train_step'
FWD_PARAMS = ['x', 'c', 'w_ada', 'b_ada', 'w_in', 'b_forget', 'w_dw', 'b_dw', 'gn_g', 'gn_b', 'g_attn_out', 'g_conv_out', 'w_out', 'ln1_g', 'ln1_b', 'w_ff1', 'w_ff2', 'ln2_g', 'ln2_b']
TWIN_WEIGHTS = ['w_ada', 'b_ada', 'w_in', 'b_forget', 'w_dw', 'b_dw', 'gn_g', 'gn_b', 'g_attn_out', 'g_conv_out', 'w_out', 'ln1_g', 'ln1_b', 'w_ff1', 'w_ff2', 'ln2_g', 'ln2_b']
TWIN_DIFF_INPUT = 'x'
TWIN_INPUTS = ['x', 'c', 'w_ada', 'b_ada', 'w_in', 'b_forget', 'w_dw', 'b_dw', 'gn_g', 'gn_b', 'g_attn_out', 'g_conv_out', 'w_out', 'ln1_g', 'ln1_b', 'w_ff1', 'w_ff2', 'ln2_g', 'ln2_b', 'loss_target', 'm_w_ada', 'm_b_ada', 'm_w_in', 'm_b_forget', 'm_w_dw', 'm_b_dw', 'm_gn_g', 'm_gn_b', 'm_g_attn_out', 'm_g_conv_out', 'm_w_out', 'm_ln1_g', 'm_ln1_b', 'm_w_ff1', 'm_w_ff2', 'm_ln2_g', 'm_ln2_b', 'v_w_ada', 'v_b_ada', 'v_w_in', 'v_b_forget', 'v_w_dw', 'v_b_dw', 'v_gn_g', 'v_gn_b', 'v_g_attn_out', 'v_g_conv_out', 'v_w_out', 'v_ln1_g', 'v_ln1_b', 'v_w_ff1', 'v_w_ff2', 'v_ln2_g', 'v_ln2_b']
TWIN_OUTPUTS = ['loss', 'grad_x', 'grad_w_ada', 'grad_b_ada', 'grad_w_in', 'grad_b_forget', 'grad_w_dw', 'grad_b_dw', 'grad_gn_g', 'grad_gn_b', 'grad_g_attn_out', 'grad_g_conv_out', 'grad_w_out', 'grad_ln1_g', 'grad_ln1_b', 'grad_w_ff1', 'grad_w_ff2', 'grad_ln2_g', 'grad_ln2_b', 'delta_w_ada', 'delta_b_ada', 'delta_w_in', 'delta_b_forget', 'delta_w_dw', 'delta_b_dw', 'delta_gn_g', 'delta_gn_b', 'delta_g_attn_out', 'delta_g_conv_out', 'delta_w_out', 'delta_ln1_g', 'delta_ln1_b', 'delta_w_ff1', 'delta_w_ff2', 'delta_ln2_g', 'delta_ln2_b', 'new_m_w_ada', 'new_m_b_ada', 'new_m_w_in', 'new_m_b_forget', 'new_m_w_dw', 'new_m_b_dw', 'new_m_gn_g', 'new_m_gn_b', 'new_m_g_attn_out', 'new_m_g_conv_out', 'new_m_w_out', 'new_m_ln1_g', 'new_m_ln1_b', 'new_m_w_ff1', 'new_m_w_ff2', 'new_m_ln2_g', 'new_m_ln2_b', 'new_v_w_ada', 'new_v_b_ada', 'new_v_w_in', 'new_v_b_forget', 'new_v_w_dw', 'new_v_b_dw', 'new_v_gn_g', 'new_v_gn_b', 'new_v_g_attn_out', 'new_v_g_conv_out', 'new_v_w_out', 'new_v_ln1_g', 'new_v_ln1_b', 'new_v_w_ff1', 'new_v_w_ff2', 'new_v_ln2_g', 'new_v_ln2_b']
TWIN_LEAF_KINDS = {'loss': 'loss', 'grad_x': 'grad_x', 'grad_w_ada': 'grad_w', 'grad_b_ada': 'grad_w', 'grad_w_in': 'grad_w', 'grad_b_forget': 'grad_w', 'grad_w_dw': 'grad_w', 'grad_b_dw': 'grad_w', 'grad_gn_g': 'grad_w', 'grad_gn_b': 'grad_w', 'grad_g_attn_out': 'grad_w', 'grad_g_conv_out': 'grad_w', 'grad_w_out': 'grad_w', 'grad_ln1_g': 'grad_w', 'grad_ln1_b': 'grad_w', 'grad_w_ff1': 'grad_w', 'grad_w_ff2': 'grad_w', 'grad_ln2_g': 'grad_w', 'grad_ln2_b': 'grad_w', 'delta_w_ada': 'delta_w', 'delta_b_ada': 'delta_w', 'delta_w_in': 'delta_w', 'delta_b_forget': 'delta_w', 'delta_w_dw': 'delta_w', 'delta_b_dw': 'delta_w', 'delta_gn_g': 'delta_w', 'delta_gn_b': 'delta_w', 'delta_g_attn_out': 'delta_w', 'delta_g_conv_out': 'delta_w', 'delta_w_out': 'delta_w', 'delta_ln1_g': 'delta_w', 'delta_ln1_b': 'delta_w', 'delta_w_ff1': 'delta_w', 'delta_w_ff2': 'delta_w', 'delta_ln2_g': 'delta_w', 'delta_ln2_b': 'delta_w', 'new_m_w_ada': 'new_m', 'new_m_b_ada': 'new_m', 'new_m_w_in': 'new_m', 'new_m_b_forget': 'new_m', 'new_m_w_dw': 'new_m', 'new_m_b_dw': 'new_m', 'new_m_gn_g': 'new_m', 'new_m_gn_b': 'new_m', 'new_m_g_attn_out': 'new_m', 'new_m_g_conv_out': 'new_m', 'new_m_w_out': 'new_m', 'new_m_ln1_g': 'new_m', 'new_m_ln1_b': 'new_m', 'new_m_w_ff1': 'new_m', 'new_m_w_ff2': 'new_m', 'new_m_ln2_g': 'new_m', 'new_m_ln2_b': 'new_m', 'new_v_w_ada': 'new_v', 'new_v_b_ada': 'new_v', 'new_v_w_in': 'new_v', 'new_v_b_forget': 'new_v', 'new_v_w_dw': 'new_v', 'new_v_b_dw': 'new_v', 'new_v_gn_g': 'new_v', 'new_v_gn_b': 'new_v', 'new_v_g_attn_out': 'new_v', 'new_v_g_conv_out': 'new_v', 'new_v_w_out': 'new_v', 'new_v_ln1_g': 'new_v', 'new_v_ln1_b': 'new_v', 'new_v_w_ff1': 'new_v', 'new_v_w_ff2': 'new_v', 'new_v_ln2_g': 'new_v', 'new_v_ln2_b': 'new_v'}


def _forward(args):
    return _fwd_reference(*[args[k] for k in FWD_PARAMS])


def _output_shape():
    def fwd():
        inp = _fwd_setup_inputs(0)
        return _fwd_reference(*[inp[k] for k in FWD_PARAMS])
    out = _jax.eval_shape(fwd)
    return out.shape, out.dtype

N_MICROBATCH = 1
ADAM_LR = 0.001
ADAM_B1 = 0.9
ADAM_B2 = 0.999
ADAM_EPS = 1e-08
ADAM_WD = 0.01
ADAM_STEP = 10
PER_EXAMPLE_BATCH_AXIS = {'x': 0, 'c': 0, 'loss_target': 0}
SHARED_INPUTS = []
_WEIGHT_DTYPES = {'w_ada': _jnp.float32, 'b_ada': _jnp.float32, 'w_in': _jnp.float32, 'b_forget': _jnp.float32, 'w_dw': _jnp.float32, 'b_dw': _jnp.float32, 'gn_g': _jnp.float32, 'gn_b': _jnp.float32, 'g_attn_out': _jnp.float32, 'g_conv_out': _jnp.float32, 'w_out': _jnp.float32, 'ln1_g': _jnp.float32, 'ln1_b': _jnp.float32, 'w_ff1': _jnp.float32, 'w_ff2': _jnp.float32, 'ln2_g': _jnp.float32, 'ln2_b': _jnp.float32}
MOMENT_SCALE = {'w_ada': 2.498126e-01, 'b_ada': 8.707381e-01, 'w_in': 1.463148e-01, 'b_forget': 1.218506e+00, 'w_dw': 1.233842e-01, 'b_dw': 1.012589e+00, 'gn_g': 2.664516e-01, 'gn_b': 3.792402e-01, 'g_attn_out': 1.238047e-01, 'g_conv_out': 1.751928e-01, 'w_out': 2.514379e-01, 'ln1_g': 2.126671e+00, 'ln1_b': 1.076399e+00, 'w_ff1': 5.375648e-02, 'w_ff2': 1.521081e-01, 'ln2_g': 6.412104e+01, 'ln2_b': 1.022952e+01}


def _to_microbatches(a, axis):
    t = _jnp.moveaxis(a, axis, 0)
    t = t.reshape((N_MICROBATCH, t.shape[0] // N_MICROBATCH) + t.shape[1:])
    return _jnp.moveaxis(t, 1, axis + 1)


def setup_inputs(seed: int = 0) -> dict:
    inp = _fwd_setup_inputs(seed)
    key = _jax.random.fold_in(_jax.random.key(seed), 7919)
    shape, _ = _output_shape()
    out = dict(inp)
    out["loss_target"] = _jax.random.normal(_jax.random.fold_in(key, 0), shape, _jnp.float32)
    for i, name in enumerate(TWIN_WEIGHTS):
        w = inp[name].astype(_jnp.float32)
        if MOMENT_SCALE is None:
            s = _jnp.sqrt(_jnp.mean(_jnp.square(w)) + 1e-30)
        else:
            s = MOMENT_SCALE[name]
        km, kv = _jax.random.split(_jax.random.fold_in(key, i + 1))
        out[name] = w
        out["m_" + name] = s * _jax.random.normal(km, w.shape, _jnp.float32)
        out["v_" + name] = (s * s) * _jax.random.uniform(kv, w.shape, _jnp.float32, 0.5, 1.5)
    if N_MICROBATCH > 1:
        for name, axis in PER_EXAMPLE_BATCH_AXIS.items():
            out[name] = _to_microbatches(out[name], axis)
    return {'x': out['x'], 'c': out['c'], 'w_ada': out['w_ada'], 'b_ada': out['b_ada'], 'w_in': out['w_in'], 'b_forget': out['b_forget'], 'w_dw': out['w_dw'], 'b_dw': out['b_dw'], 'gn_g': out['gn_g'], 'gn_b': out['gn_b'], 'g_attn_out': out['g_attn_out'], 'g_conv_out': out['g_conv_out'], 'w_out': out['w_out'], 'ln1_g': out['ln1_g'], 'ln1_b': out['ln1_b'], 'w_ff1': out['w_ff1'], 'w_ff2': out['w_ff2'], 'ln2_g': out['ln2_g'], 'ln2_b': out['ln2_b'], 'loss_target': out['loss_target'], 'm_w_ada': out['m_w_ada'], 'm_b_ada': out['m_b_ada'], 'm_w_in': out['m_w_in'], 'm_b_forget': out['m_b_forget'], 'm_w_dw': out['m_w_dw'], 'm_b_dw': out['m_b_dw'], 'm_gn_g': out['m_gn_g'], 'm_gn_b': out['m_gn_b'], 'm_g_attn_out': out['m_g_attn_out'], 'm_g_conv_out': out['m_g_conv_out'], 'm_w_out': out['m_w_out'], 'm_ln1_g': out['m_ln1_g'], 'm_ln1_b': out['m_ln1_b'], 'm_w_ff1': out['m_w_ff1'], 'm_w_ff2': out['m_w_ff2'], 'm_ln2_g': out['m_ln2_g'], 'm_ln2_b': out['m_ln2_b'], 'v_w_ada': out['v_w_ada'], 'v_b_ada': out['v_b_ada'], 'v_w_in': out['v_w_in'], 'v_b_forget': out['v_b_forget'], 'v_w_dw': out['v_w_dw'], 'v_b_dw': out['v_b_dw'], 'v_gn_g': out['v_gn_g'], 'v_gn_b': out['v_gn_b'], 'v_g_attn_out': out['v_g_attn_out'], 'v_g_conv_out': out['v_g_conv_out'], 'v_w_out': out['v_w_out'], 'v_ln1_g': out['v_ln1_g'], 'v_ln1_b': out['v_ln1_b'], 'v_w_ff1': out['v_w_ff1'], 'v_w_ff2': out['v_w_ff2'], 'v_ln2_g': out['v_ln2_g'], 'v_ln2_b': out['v_ln2_b']}


def _loss(weights, diff, rest, loss_target):
    with _jax.named_scope("forward"):
        args = {**rest, TWIN_DIFF_INPUT: diff, **{k: w.astype(_WEIGHT_DTYPES[k]) for k, w in weights.items()}}
        y = _forward(args)
    with _jax.named_scope("loss_head"):
        err = _jnp.square(y.astype(_jnp.float32) - loss_target)
        return 0.5 * _jnp.sum(_jnp.mean(err, axis=-1)) if err.ndim else 0.5 * err


def _adamw(w, g, m, v):
    m = ADAM_B1 * m + (1.0 - ADAM_B1) * g
    v = ADAM_B2 * v + (1.0 - ADAM_B2) * _jnp.square(g)
    m_hat = m / (1.0 - ADAM_B1 ** ADAM_STEP)
    v_hat = v / (1.0 - ADAM_B2 ** ADAM_STEP)
    delta = -ADAM_LR * (m_hat / (_jnp.sqrt(v_hat) + ADAM_EPS) + ADAM_WD * w)
    return delta, m, v


def reference(x, c, w_ada, b_ada, w_in, b_forget, w_dw, b_dw, gn_g, gn_b, g_attn_out, g_conv_out, w_out, ln1_g, ln1_b, w_ff1, w_ff2, ln2_g, ln2_b, loss_target, m_w_ada, m_b_ada, m_w_in, m_b_forget, m_w_dw, m_b_dw, m_gn_g, m_gn_b, m_g_attn_out, m_g_conv_out, m_w_out, m_ln1_g, m_ln1_b, m_w_ff1, m_w_ff2, m_ln2_g, m_ln2_b, v_w_ada, v_b_ada, v_w_in, v_b_forget, v_w_dw, v_b_dw, v_gn_g, v_gn_b, v_g_attn_out, v_g_conv_out, v_w_out, v_ln1_g, v_ln1_b, v_w_ff1, v_w_ff2, v_ln2_g, v_ln2_b):
    given = dict(x=x, c=c, w_ada=w_ada, b_ada=b_ada, w_in=w_in, b_forget=b_forget, w_dw=w_dw, b_dw=b_dw, gn_g=gn_g, gn_b=gn_b, g_attn_out=g_attn_out, g_conv_out=g_conv_out, w_out=w_out, ln1_g=ln1_g, ln1_b=ln1_b, w_ff1=w_ff1, w_ff2=w_ff2, ln2_g=ln2_g, ln2_b=ln2_b, loss_target=loss_target, m_w_ada=m_w_ada, m_b_ada=m_b_ada, m_w_in=m_w_in, m_b_forget=m_b_forget, m_w_dw=m_w_dw, m_b_dw=m_b_dw, m_gn_g=m_gn_g, m_gn_b=m_gn_b, m_g_attn_out=m_g_attn_out, m_g_conv_out=m_g_conv_out, m_w_out=m_w_out, m_ln1_g=m_ln1_g, m_ln1_b=m_ln1_b, m_w_ff1=m_w_ff1, m_w_ff2=m_w_ff2, m_ln2_g=m_ln2_g, m_ln2_b=m_ln2_b, v_w_ada=v_w_ada, v_b_ada=v_b_ada, v_w_in=v_w_in, v_b_forget=v_b_forget, v_w_dw=v_w_dw, v_b_dw=v_b_dw, v_gn_g=v_gn_g, v_gn_b=v_gn_b, v_g_attn_out=v_g_attn_out, v_g_conv_out=v_g_conv_out, v_w_out=v_w_out, v_ln1_g=v_ln1_g, v_ln1_b=v_ln1_b, v_w_ff1=v_w_ff1, v_w_ff2=v_w_ff2, v_ln2_g=v_ln2_g, v_ln2_b=v_ln2_b)
    weights = {n: given[n] for n in TWIN_WEIGHTS}
    shared = {n: given[n] for n in SHARED_INPUTS}
    per_example = {n: given[n] for n in ['x', 'c']}
    grad_fn = _jax.value_and_grad(_loss, argnums=(0, 1))

    def one_microbatch(ex, loss_target):
        ex = dict(ex)
        diff = ex.pop(TWIN_DIFF_INPUT)
        return grad_fn(weights, diff, {**shared, **ex}, loss_target)

    if N_MICROBATCH == 1:
        loss, (grad_w, grad_x) = one_microbatch(per_example, given["loss_target"])
    else:
        def body(carry, xs):
            loss_sum, grad_sum = carry
            l_k, (gw_k, gx_k) = one_microbatch(xs[0], xs[1])
            with _jax.named_scope("update"):
                return (loss_sum + l_k, _jax.tree.map(_jnp.add, grad_sum, gw_k)), gx_k

        init = (_jnp.zeros((), _jnp.float32), _jax.tree.map(_jnp.zeros_like, weights))
        (loss, grad_w), grad_x = _jax.lax.scan(body, init, (per_example, given["loss_target"]))
    with _jax.named_scope("update"):
        delta_w, new_m, new_v = {}, {}, {}
        for n in TWIN_WEIGHTS:
            delta_w[n], new_m[n], new_v[n] = _adamw(weights[n], grad_w[n], given["m_" + n], given["v_" + n])
    return (loss, grad_x, *[grad_w[n] for n in TWIN_WEIGHTS], *[delta_w[n] for n in TWIN_WEIGHTS],
            *[new_m[n] for n in TWIN_WEIGHTS], *[new_v[n] for n in TWIN_WEIGHTS])
```

```python
import functools

import jax
import jax.numpy as jnp
from jax import lax
from jax.experimental import pallas as pl
from jax.experimental.pallas import tpu as pltpu

F32 = jnp.float32
BF16 = jnp.bfloat16
HIGHEST = lax.Precision.HIGHEST
MESH = pl.DeviceIdType.MESH
ANY = pl.BlockSpec(memory_space=pl.ANY)

D_MODEL = 1024
HEAD_DIM = 64
ATTN_W = 512
CONV_W = 512
N_HEADS = 8
N_PAIRS = 4
CONV_K = 31
HALO = 32
D_FF = 4096
N_SHARD = 4
FF_CHUNK = D_FF // N_SHARD
N_IN = 2568
IN_SHARD = N_IN // N_SHARD
IN_SHARD_PAD = 768
N_IN_PAD = 5 * 512 + 128
LN_EPS = 1e-5
ALPHA = 2.0 ** 0.25
LR, B1, B2, ADAM_EPS, WD, STEP = 0.001, 0.9, 0.999, 1e-08, 0.01, 10
VMEM_LIMIT = 56 * 1024 * 1024
SMALL_ROWS = 48
STACK_ROWS = 768 + 256 + 1024 + 1024


def _cparams(n_axes):
    return pltpu.CompilerParams(dimension_semantics=("arbitrary",) * n_axes, vmem_limit_bytes=VMEM_LIMIT)


def _dot(a, b):
    return jnp.dot(a, b, preferred_element_type=F32)


def _dot_nt(a, b):
    return lax.dot_general(a, b, (((1,), (1,)), ((), ())), preferred_element_type=F32)


def _dot_tn(a, b):
    return lax.dot_general(a, b, (((0,), (0,)), ((), ())), preferred_element_type=F32)


def _dot_f32(a, b):
    return jnp.dot(a, b, preferred_element_type=F32, precision=HIGHEST)


def _split3(x):
    hi = x.astype(BF16)
    r = x - hi.astype(F32)
    mid = r.astype(BF16)
    lo = (r - mid.astype(F32)).astype(BF16)
    return hi, mid, lo


def _tri_dot(tri, x):
    hi, mid, lo = _split3(x)
    return _dot(tri, hi) + _dot(tri, mid) + _dot(tri, lo)


def _rowsum(x):
    return jnp.sum(x, axis=0, keepdims=True)


def _mean_last(x):
    return jnp.mean(x, axis=-1, keepdims=True)


def _position():
    x, y, c = lax.axis_index("x"), lax.axis_index("y"), lax.axis_index("c")
    return x, y, c


def _ada_fwd(c_row, w_ada, b_ada):
    n_col = w_ada.shape[1]

    def body(c_ref, w_ref, b_ref, sc_ref, ada_ref, call_ref, part_ref, pall_ref, s1, r1, s2, r2):
        x, y, c = _position()
        me = 4 * x + 2 * y + c
        q = 2 * x + y
        call_ref[me] = jnp.broadcast_to(c_ref[...], (8, D_MODEL))

        def c_copy(k):
            peer = (x ^ ((k >> 2) & 1), y ^ ((k >> 1) & 1), c ^ (k & 1))
            return pltpu.make_async_remote_copy(
                src_ref=call_ref.at[me], dst_ref=call_ref.at[me], send_sem=s1.at[k], recv_sem=r1.at[k],
                device_id=peer, device_id_type=MESH)

        def c_recv(k):
            src = me ^ k
            return pltpu.make_async_remote_copy(
                src_ref=call_ref.at[src], dst_ref=call_ref.at[src], send_sem=s1.at[k], recv_sem=r1.at[k],
                device_id=(x, y, c), device_id_type=MESH)

        sends = [c_copy(k) for k in range(1, 8)]
        for cp in sends:
            cp.start()
        for k in range(1, 8):
            c_recv(k).wait_recv()
        for cp in sends:
            cp.wait_send()

        row = lax.broadcasted_iota(jnp.int32, (8, D_MODEL), 0)
        c_all = jnp.zeros((8, D_MODEL), F32)
        for j in range(8):
            c_all = jnp.where(row == j, call_ref[j], c_all)
        sc_all = c_all * jax.nn.sigmoid(c_all)
        sc_ref[...] = sc_all
        b_slice = b_ref[:, pl.ds(pl.multiple_of(q * n_col, 128), n_col)]
        part = _dot(sc_all.astype(BF16), w_ref[...].astype(BF16)) + b_slice
        part_ref[...] = part
        pall_ref[q] = part

        def p_copy(j):
            peer = (x ^ ((j >> 1) & 1), y ^ (j & 1), c)
            return pltpu.make_async_remote_copy(
                src_ref=part_ref, dst_ref=pall_ref.at[q], send_sem=s2.at[j], recv_sem=r2.at[j],
                device_id=peer, device_id_type=MESH)

        def p_recv(j):
            src_q = q ^ j
            return pltpu.make_async_remote_copy(
                src_ref=part_ref, dst_ref=pall_ref.at[src_q], send_sem=s2.at[j], recv_sem=r2.at[j],
                device_id=(x, y, c), device_id_type=MESH)

        sends2 = [p_copy(j) for j in range(1, 4)]
        for cp in sends2:
            cp.start()
        for j in range(1, 4):
            p_recv(j).wait_recv()
        for cp in sends2:
            cp.wait_send()
        for qq in range(N_SHARD):
            ada_ref[qq] = pall_ref[qq, pl.ds(me, 1), :]

    vm = pl.BlockSpec(memory_space=pltpu.VMEM)
    sc_all, ada = pl.pallas_call(
        body, name="ada_fwd",
        out_shape=(jax.ShapeDtypeStruct((8, D_MODEL), F32), jax.ShapeDtypeStruct((N_SHARD, 1, n_col), F32)),
        in_specs=[vm, vm, vm], out_specs=(vm, vm),
        scratch_shapes=[pltpu.VMEM((8, 8, D_MODEL), F32), pltpu.VMEM((8, n_col), F32),
                        pltpu.VMEM((N_SHARD, 8, n_col), F32),
                        pltpu.SemaphoreType.DMA((8,)), pltpu.SemaphoreType.DMA((8,)),
                        pltpu.SemaphoreType.DMA((4,)), pltpu.SemaphoreType.DMA((4,))],
        compiler_params=pltpu.CompilerParams(vmem_limit_bytes=VMEM_LIMIT),
    )(c_row, w_ada, b_ada)
    return sc_all, ada.reshape(6, D_MODEL)


def _weight_gather(shards):
    n = len(shards)

    def body(*refs):
        ins, outs = refs[:n], refs[n:2 * n]
        lsem, ssem, rsem = refs[2 * n:]
        x, y, c = _position()
        q = 2 * x + y
        started = []
        for a in range(n):
            loc = pltpu.make_async_copy(ins[a], outs[a].at[q], lsem.at[a])
            loc.start()
            started.append(loc)
        sends = []
        for a in range(n):
            for j in range(1, 4):
                peer = (x ^ ((j >> 1) & 1), y ^ (j & 1), c)
                cp = pltpu.make_async_remote_copy(
                    src_ref=ins[a], dst_ref=outs[a].at[q], send_sem=ssem.at[a, j], recv_sem=rsem.at[a, j],
                    device_id=peer, device_id_type=MESH)
                cp.start()
                sends.append(cp)
        for a in range(n):
            for j in range(1, 4):
                pltpu.make_async_remote_copy(
                    src_ref=ins[a], dst_ref=outs[a].at[q ^ j], send_sem=ssem.at[a, j], recv_sem=rsem.at[a, j],
                    device_id=(x, y, c), device_id_type=MESH).wait_recv()
        for cp in sends:
            cp.wait_send()
        for loc in started:
            loc.wait()

    return pl.pallas_call(
        body, name="weight_gather",
        out_shape=tuple(jax.ShapeDtypeStruct((N_SHARD,) + s.shape, s.dtype) for s in shards),
        in_specs=[ANY] * n, out_specs=tuple([ANY] * n),
        scratch_shapes=[pltpu.SemaphoreType.DMA((n,)), pltpu.SemaphoreType.DMA((n, 4)),
                        pltpu.SemaphoreType.DMA((n, 4))],
    )(*shards)


def _inproj_fwd(x, ada, w_p, bf, ts):
    s = x.shape[0]
    ns = s // ts

    def body(x_ref, ada_ref, w_ref, bf_ref, q_ref, k_ref, v_ref, a_ref, g_ref, lf_ref, fc_ref, fr_ref, carry):
        i = pl.program_id(0)

        @pl.when(i == 0)
        def _():
            carry[...] = jnp.zeros_like(carry)

        u = (x_ref[...] * (1.0 + ada_ref[1:2, :]) + ada_ref[0:1, :]).astype(BF16)
        proj = _dot(u, w_ref[...])
        q_ref[...] = (proj[:, 0:512] * (HEAD_DIM ** -0.5)).astype(BF16)
        k_ref[...] = proj[:, 512:1024].astype(BF16)
        v_ref[...] = proj[:, 1024:1536].astype(BF16)
        a_ref[...] = proj[:, 1536:2048]
        g_ref[...] = proj[:, 2048:2560]
        z = proj[:, 2560:2688] + bf_ref[...]
        lane = lax.broadcasted_iota(jnp.int32, (ts, 128), 1)
        logf = jnp.minimum(z, 0.0) - jnp.log(1.0 + jnp.exp(-jnp.abs(z)))
        logf = jnp.where(lane < N_HEADS, logf, 0.0)
        lf_ref[...] = logf
        r = lax.broadcasted_iota(jnp.int32, (ts, ts), 0)
        cc = lax.broadcasted_iota(jnp.int32, (ts, ts), 1)
        tri = (cc <= r).astype(BF16)
        fc = _tri_dot(tri, logf) + carry[...]
        fc_ref[...] = fc
        carry[...] = fc[ts - 1:ts, :]
        fr_ref[...] = fc.T[0:8, :]

    row = lambda i: (i, 0)
    full = lambda i: (0, 0)
    return pl.pallas_call(
        body, name="inproj_fwd", grid=(ns,),
        out_shape=(jax.ShapeDtypeStruct((s, 512), BF16),) * 3 + (jax.ShapeDtypeStruct((s, 512), F32),) * 2
        + (jax.ShapeDtypeStruct((s, 128), F32),) * 2 + (jax.ShapeDtypeStruct((8, s), F32),),
        in_specs=[pl.BlockSpec((ts, D_MODEL), row), pl.BlockSpec((6, D_MODEL), full),
                  pl.BlockSpec((D_MODEL, N_IN_PAD), full), pl.BlockSpec((1, 128), full)],
        out_specs=(pl.BlockSpec((ts, 512), row),) * 5 + (pl.BlockSpec((ts, 128), row),) * 2
        + (pl.BlockSpec((8, ts), lambda i: (0, i)),),
        scratch_shapes=[pltpu.VMEM((1, 128), F32)],
        compiler_params=_cparams(1),
    )(x, ada, w_p, bf)


def _attn_fwd(q, k, v, fc, fr, tq):
    s = q.shape[0]
    nq = s // tq
    tk = tq

    def body(q_ref, k_hbm, v_hbm, fc_ref, fr_ref, o_ref, lse_ref, k_vm, v_vm, sem):
        i = pl.program_id(0)

        @pl.when(i == 0)
        def _():
            ck = pltpu.make_async_copy(k_hbm, k_vm, sem.at[0])
            cv = pltpu.make_async_copy(v_hbm, v_vm, sem.at[1])
            ck.start()
            cv.start()
            ck.wait()
            cv.wait()

        lane = lax.broadcasted_iota(jnp.int32, (tq, 128), 1)
        low = lane < HEAD_DIM
        t_pos = i * tq + lax.broadcasted_iota(jnp.int32, (tq, tk), 0)
        s_off = lax.broadcasted_iota(jnp.int32, (tq, tk), 1)
        lse_tile = jnp.zeros((tq, 128), F32)
        for p in range(N_PAIRS):
            cols = slice(128 * p, 128 * p + 128)
            qb = q_ref[:, cols]
            qm = (jnp.where(low, qb, jnp.zeros_like(qb)), jnp.where(low, jnp.zeros_like(qb), qb))
            ft = (fc_ref[:, 2 * p:2 * p + 1], fc_ref[:, 2 * p + 1:2 * p + 2])

            def step(j, carry):
                start = pl.multiple_of(j * tk, tk)
                kb = k_vm[pl.ds(start, tk), cols]
                vb = v_vm[pl.ds(start, tk), cols]
                causal = (s_off + j * tk) <= t_pos
                out = []
                for hh in range(2):
                    m, l, acc = carry[3 * hh:3 * hh + 3]
                    fs = fr_ref[2 * p + hh:2 * p + hh + 1, pl.ds(start, tk)]
                    sc = (_dot_nt(qm[hh], kb) + ft[hh]) - fs
                    sc = jnp.where(causal, sc, -jnp.inf)
                    m_new = jnp.maximum(m, jnp.max(sc, axis=-1, keepdims=True))
                    scale = jnp.exp(m - m_new)
                    pr = jnp.exp(sc - m_new)
                    l = l * scale + jnp.sum(pr, axis=-1, keepdims=True)
                    acc = acc * scale + _dot(pr.astype(BF16), vb)
                    out += [m_new, l, acc]
                return tuple(out)

            init = (jnp.full((tq, 1), -jnp.inf, F32), jnp.zeros((tq, 1), F32), jnp.zeros((tq, 128), F32)) * 2
            res = lax.fori_loop(0, i + 1, step, init)
            o0 = res[2] / res[1]
            o1 = res[5] / res[4]
            o_ref[:, cols] = jnp.where(low, o0, o1)
            lse_tile = jnp.where(lane == 2 * p, res[0] + jnp.log(res[1]), lse_tile)
            lse_tile = jnp.where(lane == 2 * p + 1, res[3] + jnp.log(res[4]), lse_tile)
        lse_ref[...] = lse_tile

    row = lambda i: (i, 0)
    return pl.pallas_call(
        body, name="attn_fwd", grid=(nq,),
        out_shape=(jax.ShapeDtypeStruct((s, 512), F32), jax.ShapeDtypeStruct((s, 128), F32)),
        in_specs=[pl.BlockSpec((tq, 512), row), ANY, ANY, pl.BlockSpec((tq, 128), row),
                  pl.BlockSpec((8, s), lambda i: (0, 0))],
        out_specs=(pl.BlockSpec((tq, 512), row), pl.BlockSpec((tq, 128), row)),
        scratch_shapes=[pltpu.VMEM((s, 512), BF16), pltpu.VMEM((s, 512), BF16), pltpu.SemaphoreType.DMA((2,))],
        compiler_params=_cparams(1),
    )(q, k, v, fc, fr)


def _conv_branch(a, g, ah, gh, first, ugx, wdw_ref, prm_ref, gm, ts):
    sg_g = jax.nn.sigmoid(g)
    ug = (a * sg_g).astype(BF16).astype(F32)
    ugh = jnp.where(first, 0.0, (ah * jax.nn.sigmoid(gh)).astype(BF16).astype(F32))
    ugx[0:HALO, :] = ugh
    ugx[HALO:HALO + ts, :] = ug
    y = jnp.zeros((ts, CONV_W), F32) + prm_ref[0:1, :]
    for kk in range(CONV_K):
        off = HALO - (CONV_K - 1) + kk
        y = y + wdw_ref[kk:kk + 1, :] * ugx[off:off + ts, :]
    mu = _dot_f32(y, gm)
    d = y - mu
    var = _dot_f32(d * d, gm)
    rs = lax.rsqrt(var + LN_EPS)
    yhat = d * rs
    yn = yhat * prm_ref[1:2, :] + prm_ref[2:3, :]
    sg = jax.nn.sigmoid(yn)
    co = yn * sg
    return sg_g, rs, yhat, yn, sg, co


def _mix_inputs(o, co, prm_ref):
    ra = lax.rsqrt(_mean_last(o * o) + LN_EPS)
    oh = o * ra
    rc = lax.rsqrt(_mean_last(co * co) + LN_EPS)
    ch = co * rc
    mi = jnp.concatenate([oh * prm_ref[3:4, :], ch * prm_ref[4:5, :]], axis=-1).astype(BF16)
    return ra, oh, rc, ch, mi


def _layernorm_stats(r):
    mu = _mean_last(r)
    d = r - mu
    rstd = lax.rsqrt(_mean_last(d * d) + LN_EPS)
    return d * rstd, rstd


def _layernorm_bwd(dout, xh, rstd, gain):
    dxh = dout * gain
    return rstd * (dxh - _mean_last(dxh) - xh * _mean_last(dxh * xh))


def _halo_index(tile, ts):
    return jnp.maximum(tile * (ts // HALO) - 1, 0)


def _mid_fwd(a, g, o, x, ada, wdw, prm, ln1, w_out, gm, ts):
    s = x.shape[0]
    ns = s // ts

    def body(a_ref, g_ref, ah_ref, gh_ref, o_ref, x_ref, ada_ref, wdw_ref, prm_ref, ln_ref, wo_ref, gm_ref,
             x1_ref, ugx):
        i = pl.program_id(0)
        co = _conv_branch(a_ref[...], g_ref[...], ah_ref[...], gh_ref[...], i == 0, ugx, wdw_ref, prm_ref,
                          gm_ref[...], ts)[-1]
        mi = _mix_inputs(o_ref[...], co, prm_ref)[-1]
        mixed = _dot(mi, wo_ref[...])
        r1 = ALPHA * x_ref[...] + (1.0 + ada_ref[2:3, :]) * mixed
        xh, _ = _layernorm_stats(r1)
        x1_ref[...] = xh * ln_ref[0:1, :] + ln_ref[1:2, :]

    row = lambda i: (i, 0)
    full = lambda i: (0, 0)
    halo = lambda i: (_halo_index(i, ts), 0)
    return pl.pallas_call(
        body, name="mid_fwd", grid=(ns,),
        out_shape=jax.ShapeDtypeStruct((s, D_MODEL), F32),
        in_specs=[pl.BlockSpec((ts, 512), row), pl.BlockSpec((ts, 512), row),
                  pl.BlockSpec((HALO, 512), halo), pl.BlockSpec((HALO, 512), halo),
                  pl.BlockSpec((ts, 512), row), pl.BlockSpec((ts, D_MODEL), row),
                  pl.BlockSpec((6, D_MODEL), full), pl.BlockSpec((32, 512), full), pl.BlockSpec((8, 512), full),
                  pl.BlockSpec((2, D_MODEL), full), pl.BlockSpec((D_MODEL, D_MODEL), full),
                  pl.BlockSpec((512, 512), full)],
        out_specs=pl.BlockSpec((ts, D_MODEL), row),
        scratch_shapes=[pltpu.VMEM((ts + HALO, 512), F32)],
        compiler_params=_cparams(1),
    )(a, g, a, g, o, x, ada, wdw, prm, ln1, w_out, gm)


def _ffn_fwd(x1, ada, w1, w2, ln2, tgt, ts):
    s = x1.shape[0]
    ns = s // ts
    nf = N_SHARD

    def body(x1_ref, ada_ref, w1_ref, w2_ref, ln_ref, t_ref, dff_ref, dx1_ref, pg_ref, ffacc, u2):
        i = pl.program_id(0)
        f = pl.program_id(1)

        @pl.when((i == 0) & (f == 0))
        def _():
            pg_ref[...] = jnp.zeros_like(pg_ref)

        @pl.when(f == 0)
        def _():
            u2[...] = (x1_ref[...] * (1.0 + ada_ref[4:5, :]) + ada_ref[3:4, :]).astype(BF16)
            ffacc[...] = jnp.zeros_like(ffacc)

        h = _dot(u2[...], w1_ref[0])
        r = jnp.maximum(h, 0.0)
        ffacc[...] += _dot((r * r).astype(BF16), w2_ref[0])

        @pl.when(f == nf - 1)
        def _():
            ff = ffacc[...]
            r2 = ALPHA * x1_ref[...] + (1.0 + ada_ref[5:6, :]) * ff
            xh, rstd = _layernorm_stats(r2)
            yv = xh * ln_ref[0:1, :] + ln_ref[1:2, :]
            err = yv - t_ref[...]
            dy = err * (1.0 / D_MODEL)
            dr2 = _layernorm_bwd(dy, xh, rstd, ln_ref[0:1, :])
            pg_ref[0:1, :] += _rowsum(dy * xh)
            pg_ref[1:2, :] += _rowsum(dy)
            pg_ref[2:3, :] += _rowsum(dr2 * ff)
            pg_ref[3:4, :] += _rowsum(err * err) * (0.5 / D_MODEL)
            dff_ref[...] = ((1.0 + ada_ref[5:6, :]) * dr2).astype(BF16)
            dx1_ref[...] = ALPHA * dr2

    row = lambda i, f: (i, 0)
    full = lambda i, f: (0, 0)
    chunk = lambda i, f: (f, 0, 0)
    return pl.pallas_call(
        body, name="ffn_fwd", grid=(ns, nf),
        out_shape=(jax.ShapeDtypeStruct((s, D_MODEL), BF16), jax.ShapeDtypeStruct((s, D_MODEL), F32),
                   jax.ShapeDtypeStruct((8, D_MODEL), F32)),
        in_specs=[pl.BlockSpec((ts, D_MODEL), row), pl.BlockSpec((6, D_MODEL), full),
                  pl.BlockSpec((1, D_MODEL, FF_CHUNK), chunk), pl.BlockSpec((1, FF_CHUNK, D_MODEL), chunk),
                  pl.BlockSpec((2, D_MODEL), full), pl.BlockSpec((ts, D_MODEL), row)],
        out_specs=(pl.BlockSpec((ts, D_MODEL), row), pl.BlockSpec((ts, D_MODEL), row),
                   pl.BlockSpec((8, D_MODEL), full)),
        scratch_shapes=[pltpu.VMEM((ts, D_MODEL), F32), pltpu.VMEM((ts, D_MODEL), BF16)],
        compiler_params=_cparams(2),
    )(x1, ada, w1, w2, ln2, tgt)


def _ffn_bwd_chunk(f, x1, ada, w1, w2, dff, dx1, ts):
    s = x1.shape[0]
    ns = s // ts

    def body(x1_ref, ada_ref, w1_ref, w2_ref, dff_ref, dx1_in, dx1_out, dw1_ref, dw2_ref, pg_ref):
        i = pl.program_id(0)

        @pl.when(i == 0)
        def _():
            pg_ref[...] = jnp.zeros_like(pg_ref)
            dw1_ref[...] = jnp.zeros_like(dw1_ref)
            dw2_ref[...] = jnp.zeros_like(dw2_ref)

        x1v = x1_ref[...]
        u2 = (x1v * (1.0 + ada_ref[4:5, :]) + ada_ref[3:4, :]).astype(BF16)
        h = _dot(u2, w1_ref[0])
        r = jnp.maximum(h, 0.0)
        hid = (r * r).astype(BF16)
        dffv = dff_ref[...]
        dh = (_dot_nt(dffv, w2_ref[0]) * (2.0 * r)).astype(BF16)
        dw2_ref[0] += _dot_tn(hid, dffv)
        dw1_ref[0] += _dot_tn(u2, dh)
        du2 = _dot_nt(dh, w1_ref[0])
        dx1_out[...] = dx1_in[...] + du2 * (1.0 + ada_ref[4:5, :])
        pg_ref[0:1, :] += _rowsum(du2 * x1v)
        pg_ref[1:2, :] += _rowsum(du2)

    row = lambda i: (i, 0)
    full = lambda i: (0, 0)
    full3 = lambda i: (0, 0, 0)
    chunk = lambda i: (f, 0, 0)
    return pl.pallas_call(
        body, name=f"ffn_bwd_{f}", grid=(ns,),
        out_shape=(jax.ShapeDtypeStruct((s, D_MODEL), F32), jax.ShapeDtypeStruct((1, D_MODEL, FF_CHUNK), F32),
                   jax.ShapeDtypeStruct((1, FF_CHUNK, D_MODEL), F32), jax.ShapeDtypeStruct((8, D_MODEL), F32)),
        in_specs=[pl.BlockSpec((ts, D_MODEL), row), pl.BlockSpec((6, D_MODEL), full),
                  pl.BlockSpec((1, D_MODEL, FF_CHUNK), chunk), pl.BlockSpec((1, FF_CHUNK, D_MODEL), chunk),
                  pl.BlockSpec((ts, D_MODEL), row), pl.BlockSpec((ts, D_MODEL), row)],
        out_specs=(pl.BlockSpec((ts, D_MODEL), row), pl.BlockSpec((1, D_MODEL, FF_CHUNK), full3),
                   pl.BlockSpec((1, FF_CHUNK, D_MODEL), full3), pl.BlockSpec((8, D_MODEL), full)),
        compiler_params=_cparams(1),
    )(x1, ada, w1, w2, dff, dx1)


def _mid_bwd(a, g, o, x, dx1, ada, wdw, prm, ln1, w_out, gm, sel, lse, fc, ts):
    s = x.shape[0]
    ns = s // ts

    def body(a_ref, g_ref, ah_ref, gh_ref, o_ref, x_ref, dx1_ref, ada_ref, wdw_ref, prm_ref, ln_ref, wo_ref,
             gm_ref, sel_ref, lse_ref, fc_ref,
             do_ref, da_ref, dg_ref, dxa_ref, st_ref, dwo_ref, pgm_ref, pgc_ref, dwdw_ref,
             ugx, dyx):
        i = pl.program_id(0)
        tile = ns - 1 - i

        @pl.when(i == 0)
        def _():
            dwo_ref[...] = jnp.zeros_like(dwo_ref)
            pgm_ref[...] = jnp.zeros_like(pgm_ref)
            pgc_ref[...] = jnp.zeros_like(pgc_ref)
            dwdw_ref[...] = jnp.zeros_like(dwdw_ref)
            dyx[ts:ts + HALO, :] = jnp.zeros((HALO, 512), F32)

        gmv = gm_ref[...]
        av = a_ref[...]
        ov = o_ref[...]
        sg_g, rs, yhat, yn, sg, co = _conv_branch(av, g_ref[...], ah_ref[...], gh_ref[...], tile == 0, ugx,
                                                  wdw_ref, prm_ref, gmv, ts)
        ra, oh, rc, ch, mi = _mix_inputs(ov, co, prm_ref)
        mixed = _dot(mi, wo_ref[...])
        gt1 = 1.0 + ada_ref[2:3, :]
        r1 = ALPHA * x_ref[...] + gt1 * mixed
        xh, rstd = _layernorm_stats(r1)
        dx1 = dx1_ref[...]
        pgm_ref[0:1, :] += _rowsum(dx1 * xh)
        pgm_ref[1:2, :] += _rowsum(dx1)
        dr1 = _layernorm_bwd(dx1, xh, rstd, ln_ref[0:1, :])
        dxa_ref[...] = ALPHA * dr1
        pgm_ref[2:3, :] += _rowsum(dr1 * mixed)
        dmixed = (gt1 * dr1).astype(BF16)
        dmi = _dot_nt(dmixed, wo_ref[...])
        dwo_ref[...] += _dot_tn(mi, dmixed)
        dna = dmi[:, 0:512]
        dnc = dmi[:, 512:1024]
        pgc_ref[3:4, :] += _rowsum(dna * oh)
        doh = dna * prm_ref[3:4, :]
        do = ra * (doh - oh * _mean_last(doh * oh))
        do_ref[...] = do.astype(BF16)
        lane = lax.broadcasted_iota(jnp.int32, (ts, 128), 1)
        delta = _dot_f32(do * ov, sel_ref[...])
        st_ref[...] = jnp.where(lane < N_HEADS, fc_ref[...] - lse_ref[...], delta)
        pgc_ref[4:5, :] += _rowsum(dnc * ch)
        dch = dnc * prm_ref[4:5, :]
        dco = rc * (dch - ch * _mean_last(dch * ch))
        dyn = dco * (sg * (1.0 + yn * (1.0 - sg)))
        pgc_ref[1:2, :] += _rowsum(dyn * yhat)
        pgc_ref[2:3, :] += _rowsum(dyn)
        dyh = dyn * prm_ref[1:2, :]
        dy = rs * (dyh - _dot_f32(dyh, gmv) - yhat * _dot_f32(dyh * yhat, gmv))
        pgc_ref[0:1, :] += _rowsum(dy)
        dyr = dy.astype(BF16).astype(F32)
        dyx[0:ts, :] = dyr
        dug = jnp.zeros((ts, CONV_W), F32)
        for kk in range(CONV_K):
            off = HALO - (CONV_K - 1) + kk
            dwdw_ref[kk:kk + 1, :] += _rowsum(dyr * ugx[off:off + ts, :])
            back = CONV_K - 1 - kk
            dug = dug + wdw_ref[kk:kk + 1, :] * dyx[back:back + ts, :]
        dyx[ts:ts + HALO, :] = dyr[0:HALO, :]
        da_ref[...] = (dug * sg_g).astype(BF16)
        dg_ref[...] = (dug * av * sg_g * (1.0 - sg_g)).astype(BF16)

    row = lambda i: (ns - 1 - i, 0)
    full = lambda i: (0, 0)
    halo = lambda i: (_halo_index(ns - 1 - i, ts), 0)
    return pl.pallas_call(
        body, name="mid_bwd", grid=(ns,),
        out_shape=(jax.ShapeDtypeStruct((s, 512), BF16), jax.ShapeDtypeStruct((s, 512), BF16),
                   jax.ShapeDtypeStruct((s, 512), BF16), jax.ShapeDtypeStruct((s, D_MODEL), F32),
                   jax.ShapeDtypeStruct((s, 128), F32), jax.ShapeDtypeStruct((D_MODEL, D_MODEL), F32),
                   jax.ShapeDtypeStruct((8, D_MODEL), F32), jax.ShapeDtypeStruct((8, 512), F32),
                   jax.ShapeDtypeStruct((32, 512), F32)),
        in_specs=[pl.BlockSpec((ts, 512), row), pl.BlockSpec((ts, 512), row),
                  pl.BlockSpec((HALO, 512), halo), pl.BlockSpec((HALO, 512), halo),
                  pl.BlockSpec((ts, 512), row), pl.BlockSpec((ts, D_MODEL), row), pl.BlockSpec((ts, D_MODEL), row),
                  pl.BlockSpec((6, D_MODEL), full), pl.BlockSpec((32, 512), full), pl.BlockSpec((8, 512), full),
                  pl.BlockSpec((2, D_MODEL), full), pl.BlockSpec((D_MODEL, D_MODEL), full),
                  pl.BlockSpec((512, 512), full), pl.BlockSpec((512, 128), full),
                  pl.BlockSpec((ts, 128), row), pl.BlockSpec((ts, 128), row)],
        out_specs=(pl.BlockSpec((ts, 512), row), pl.BlockSpec((ts, 512), row), pl.BlockSpec((ts, 512), row),
                   pl.BlockSpec((ts, D_MODEL), row), pl.BlockSpec((ts, 128), row),
                   pl.BlockSpec((D_MODEL, D_MODEL), full), pl.BlockSpec((8, D_MODEL), full),
                   pl.BlockSpec((8, 512), full), pl.BlockSpec((32, 512), full)),
        scratch_shapes=[pltpu.VMEM((ts + HALO, 512), F32), pltpu.VMEM((ts + HALO, 512), F32)],
        compiler_params=_cparams(1),
    )(a, g, a, g, o, x, dx1, ada, wdw, prm, ln1, w_out, gm, sel, lse, fc)


def _attn_bwd(q, k, v, do, stats, fr, tk):
    s = q.shape[0]
    nk = s // tk
    tq = tk

    def body(q_hbm, do_hbm, st_hbm, k_ref, v_ref, fr_ref, dq_hbm, rs_hbm, dk_ref, dv_ref, dfc_ref,
             q_vm, do_vm, st_vm, dq_vm, rs_vm, sem):
        j = pl.program_id(0)

        @pl.when(j == 0)
        def _():
            cps = [pltpu.make_async_copy(q_hbm, q_vm, sem.at[0]), pltpu.make_async_copy(do_hbm, do_vm, sem.at[1]),
                   pltpu.make_async_copy(st_hbm, st_vm, sem.at[2])]
            for cp in cps:
                cp.start()
            dq_vm[...] = jnp.zeros_like(dq_vm)
            rs_vm[...] = jnp.zeros_like(rs_vm)
            for cp in cps:
                cp.wait()

        lane = lax.broadcasted_iota(jnp.int32, (tk, 128), 1)
        low = lane < HEAD_DIM
        s_pos = j * tk + lax.broadcasted_iota(jnp.int32, (tq, tk), 1)
        t_off = lax.broadcasted_iota(jnp.int32, (tq, tk), 0)
        k_start = pl.multiple_of(j * tk, tk)
        dfc_tile = jnp.zeros((tk, 128), F32)
        one = jnp.ones((tq, 128), BF16)
        zero = jnp.zeros((tq, 128), BF16)
        for p in range(N_PAIRS):
            cols = slice(128 * p, 128 * p + 128)
            kb = k_ref[:, cols]
            vb = v_ref[:, cols]
            km = (jnp.where(low, kb, jnp.zeros_like(kb)), jnp.where(low, jnp.zeros_like(kb), kb))
            vm = (jnp.where(low, vb, jnp.zeros_like(vb)), jnp.where(low, jnp.zeros_like(vb), vb))
            k_aug = (jnp.where(low, kb, jnp.where(lane == HEAD_DIM, one, zero)),
                     jnp.where(low, jnp.where(lane == 0, one, zero), kb))
            fs = (fr_ref[2 * p:2 * p + 1, pl.ds(k_start, tk)], fr_ref[2 * p + 1:2 * p + 2, pl.ds(k_start, tk)])

            def step(i, carry):
                start = pl.multiple_of(i * tq, tq)
                qb = q_vm[pl.ds(start, tq), cols]
                dob = do_vm[pl.ds(start, tq), cols]
                st = st_vm[pl.ds(start, tq), :]
                causal = s_pos <= (t_off + i * tq)
                q_aug = (jnp.where(low, qb, jnp.where(lane == HEAD_DIM, one, zero)),
                         jnp.where(low, jnp.where(lane == 0, one, zero), qb))
                out = []
                dq_h = []
                for hh in range(2):
                    h = 2 * p + hh
                    dk_acc, dv_acc = carry[2 * hh:2 * hh + 2]
                    sc = (_dot_nt(qb, km[hh]) + st[:, h:h + 1]) - fs[hh]
                    pr = jnp.where(causal, jnp.exp(sc), 0.0)
                    dp = _dot_nt(dob, vm[hh])
                    ds = (pr * (dp - st[:, N_HEADS + h:N_HEADS + h + 1])).astype(BF16)
                    dq_h.append(_dot(ds, k_aug[hh]))
                    dk_acc = dk_acc + _dot_tn(ds, q_aug[hh])
                    dv_acc = dv_acc + _dot_tn(pr.astype(BF16), dob)
                    out += [dk_acc, dv_acc]
                dq_vm[pl.ds(start, tq), cols] += jnp.where(low, dq_h[0], dq_h[1])
                rs_vm[pl.ds(start, tq), :] += (jnp.where(lane == 2 * p, dq_h[0][:, HEAD_DIM:HEAD_DIM + 1], 0.0)
                                               + jnp.where(lane == 2 * p + 1, dq_h[1][:, 0:1], 0.0))
                return tuple(out)

            init = (jnp.zeros((tk, 128), F32),) * 4
            dk0, dv0, dk1, dv1 = lax.fori_loop(j, nk, step, init)
            dk_ref[:, cols] = jnp.where(low, dk0, dk1).astype(BF16)
            dv_ref[:, cols] = jnp.where(low, dv0, dv1).astype(BF16)
            dfc_tile = jnp.where(lane == 2 * p, dk0[:, HEAD_DIM:HEAD_DIM + 1], dfc_tile)
            dfc_tile = jnp.where(lane == 2 * p + 1, dk1[:, 0:1], dfc_tile)
        dfc_ref[...] = dfc_tile

        @pl.when(j == nk - 1)
        def _():
            cp = pltpu.make_async_copy(dq_vm, dq_hbm, sem.at[3])
            cr = pltpu.make_async_copy(rs_vm, rs_hbm, sem.at[4])
            cp.start()
            cr.start()
            cp.wait()
            cr.wait()

    row = lambda j: (j, 0)
    return pl.pallas_call(
        body, name="attn_bwd", grid=(nk,),
        out_shape=(jax.ShapeDtypeStruct((s, 512), F32), jax.ShapeDtypeStruct((s, 128), F32),
                   jax.ShapeDtypeStruct((s, 512), BF16), jax.ShapeDtypeStruct((s, 512), BF16),
                   jax.ShapeDtypeStruct((s, 128), F32)),
        in_specs=[ANY, ANY, ANY, pl.BlockSpec((tk, 512), row), pl.BlockSpec((tk, 512), row),
                  pl.BlockSpec((8, s), lambda j: (0, 0))],
        out_specs=(ANY, ANY, pl.BlockSpec((tk, 512), row), pl.BlockSpec((tk, 512), row),
                   pl.BlockSpec((tk, 128), row)),
        scratch_shapes=[pltpu.VMEM((s, 512), BF16), pltpu.VMEM((s, 512), BF16), pltpu.VMEM((s, 128), F32),
                        pltpu.VMEM((s, 512), F32), pltpu.VMEM((s, 128), F32), pltpu.SemaphoreType.DMA((5,))],
        compiler_params=_cparams(1),
    )(q, do, stats, k, v, fr)


def _inproj_bwd(x, ada, w_p, dq, dk, dv, da, dg, dfc, drs, logf, dxa, ts):
    s = x.shape[0]
    ns = s // ts

    def body(x_ref, ada_ref, w_ref, dq_ref, dk_ref, dv_ref, da_ref, dg_ref, dfc_ref, drs_ref, lf_ref, dxa_ref,
             gx_ref, dw_hbm, pgi_ref, dbf_ref, carry, dw_vm, sem):
        i = pl.program_id(0)

        @pl.when(i == 0)
        def _():
            carry[...] = jnp.zeros_like(carry)
            dw_vm[...] = jnp.zeros_like(dw_vm)
            pgi_ref[...] = jnp.zeros_like(pgi_ref)
            dbf_ref[...] = jnp.zeros_like(dbf_ref)

        r = lax.broadcasted_iota(jnp.int32, (ts, ts), 0)
        cc = lax.broadcasted_iota(jnp.int32, (ts, ts), 1)
        tri = (cc >= r).astype(BF16)
        dlogf = carry[...] + _tri_dot(tri, drs_ref[...] - dfc_ref[...])
        carry[...] = dlogf[0:1, :]
        lane = lax.broadcasted_iota(jnp.int32, (ts, 128), 1)
        dz = jnp.where(lane < N_HEADS, dlogf * (1.0 - jnp.exp(lf_ref[...])), 0.0)
        dbf_ref[0:1, :] += _rowsum(dz)
        dproj = jnp.concatenate(
            [(dq_ref[...] * (HEAD_DIM ** -0.5)).astype(BF16), dk_ref[...], dv_ref[...], da_ref[...], dg_ref[...],
             dz.astype(BF16)], axis=-1)
        xv = x_ref[...]
        sc1 = 1.0 + ada_ref[1:2, :]
        u = (xv * sc1 + ada_ref[0:1, :]).astype(BF16)
        du = _dot_nt(dproj, w_ref[...])
        dw_vm[...] += _dot_tn(u, dproj)
        gx_ref[...] = dxa_ref[...] + du * sc1
        pgi_ref[0:1, :] += _rowsum(du * xv)
        pgi_ref[1:2, :] += _rowsum(du)

        @pl.when(i == ns - 1)
        def _():
            cp = pltpu.make_async_copy(dw_vm, dw_hbm, sem.at[0])
            cp.start()
            cp.wait()

    row = lambda i: (ns - 1 - i, 0)
    full = lambda i: (0, 0)
    return pl.pallas_call(
        body, name="inproj_bwd", grid=(ns,),
        out_shape=(jax.ShapeDtypeStruct((s, D_MODEL), F32), jax.ShapeDtypeStruct((D_MODEL, N_IN_PAD), F32),
                   jax.ShapeDtypeStruct((8, D_MODEL), F32), jax.ShapeDtypeStruct((8, 128), F32)),
        in_specs=[pl.BlockSpec((ts, D_MODEL), row), pl.BlockSpec((6, D_MODEL), full),
                  pl.BlockSpec((D_MODEL, N_IN_PAD), full)]
        + [pl.BlockSpec((ts, 512), row)] * 5 + [pl.BlockSpec((ts, 128), row)] * 3
        + [pl.BlockSpec((ts, D_MODEL), row)],
        out_specs=(pl.BlockSpec((ts, D_MODEL), row), ANY, pl.BlockSpec((8, D_MODEL), full),
                   pl.BlockSpec((8, 128), full)),
        scratch_shapes=[pltpu.VMEM((1, 128), F32), pltpu.VMEM((D_MODEL, N_IN_PAD), F32),
                        pltpu.SemaphoreType.DMA((1,))],
        compiler_params=_cparams(1),
    )(x, ada, w_p, dq, dk, dv, da, dg, dfc, drs, logf, dxa)


def _small_reduce(packed):
    def body(p_ref, sum_ref, all_ref, ssem, rsem):
        x, y, c = _position()
        me = 4 * x + 2 * y + c
        all_ref[me] = p_ref[...]
        sends = []
        for k in range(1, 8):
            peer = (x ^ ((k >> 2) & 1), y ^ ((k >> 1) & 1), c ^ (k & 1))
            cp = pltpu.make_async_remote_copy(
                src_ref=p_ref, dst_ref=all_ref.at[me], send_sem=ssem.at[k], recv_sem=rsem.at[k],
                device_id=peer, device_id_type=MESH)
            cp.start()
            sends.append(cp)
        for k in range(1, 8):
            pltpu.make_async_remote_copy(
                src_ref=p_ref, dst_ref=all_ref.at[me ^ k], send_sem=ssem.at[k], recv_sem=rsem.at[k],
                device_id=(x, y, c), device_id_type=MESH).wait_recv()
        for cp in sends:
            cp.wait_send()
        total = all_ref[0]
        for dev in range(1, 8):
            total = total + all_ref[dev]
        sum_ref[...] = total
        loss = jnp.sum(total[SMALL_ROWS - 1:SMALL_ROWS, :], axis=-1, keepdims=True)
        sum_ref[SMALL_ROWS - 1:SMALL_ROWS, :] = jnp.broadcast_to(loss, (1, D_MODEL))

    vm = pl.BlockSpec(memory_space=pltpu.VMEM)
    return pl.pallas_call(
        body, name="small_reduce",
        out_shape=(jax.ShapeDtypeStruct((SMALL_ROWS, D_MODEL), F32), jax.ShapeDtypeStruct((8, SMALL_ROWS, D_MODEL), F32)),
        in_specs=[vm], out_specs=(vm, vm),
        scratch_shapes=[pltpu.SemaphoreType.DMA((8,)), pltpu.SemaphoreType.DMA((8,))],
        compiler_params=pltpu.CompilerParams(vmem_limit_bytes=VMEM_LIMIT),
    )(packed)


def _adam_math(gv, wv, mv, vv):
    m_new = B1 * mv + (1.0 - B1) * gv
    v_new = B2 * vv + (1.0 - B2) * (gv * gv)
    m_hat = m_new / (1.0 - B1 ** STEP)
    v_hat = v_new / (1.0 - B2 ** STEP)
    delta = -LR * (m_hat / (jnp.sqrt(v_hat) + ADAM_EPS) + WD * wv)
    return delta, m_new, v_new


def _adamw(gv, wv, mv, vv, name):
    rows, cols = gv.shape
    tr = rows
    for cand in (256, 128, 64, 32, 16, 8):
        if rows % cand == 0 and rows > cand:
            tr = cand
            break

    def body(g_ref, w_ref, m_ref, v_ref, d_ref, mo_ref, vo_ref):
        d_ref[...], mo_ref[...], vo_ref[...] = _adam_math(g_ref[...], w_ref[...], m_ref[...], v_ref[...])

    spec = pl.BlockSpec((tr, cols), lambda i: (i, 0))
    return pl.pallas_call(
        body, name=name, grid=(rows // tr,),
        out_shape=(jax.ShapeDtypeStruct((rows, cols), F32),) * 3,
        in_specs=[spec] * 4, out_specs=(spec,) * 3,
        compiler_params=_cparams(1),
    )(gv, wv, mv, vv)


def _w_ada_update(sct, dd, wv, mv, vv):
    rows, cols = wv.shape
    tr = 128

    def body(s_ref, d_ref, w_ref, m_ref, v_ref, g_ref, dl_ref, mo_ref, vo_ref):
        sv = s_ref[...]
        dv = d_ref[...]
        gv = sv[:, 0:1] * dv[0:1, :]
        for b in range(1, 8):
            gv = gv + sv[:, b:b + 1] * dv[b:b + 1, :]
        g_ref[...] = gv
        dl_ref[...], mo_ref[...], vo_ref[...] = _adam_math(gv, w_ref[...], m_ref[...], v_ref[...])

    spec = pl.BlockSpec((tr, cols), lambda i: (i, 0))
    return pl.pallas_call(
        body, name="w_ada_update", grid=(rows // tr,),
        out_shape=(jax.ShapeDtypeStruct((rows, cols), F32),) * 4,
        in_specs=[pl.BlockSpec((tr, 8), lambda i: (i, 0)), pl.BlockSpec((8, cols), lambda i: (0, 0))] + [spec] * 3,
        out_specs=(spec,) * 4,
        compiler_params=_cparams(1),
    )(sct, dd, wv, mv, vv)


_STACK = ((0, 768), (768, 256), (1024, 1024), (2048, 1024))


def _grad_exchange(parts):
    n = len(parts)

    def body(*refs):
        ins, land = refs[:n], refs[n]
        lsem, ssem, rsem = refs[n + 1:]
        x, y, c = _position()
        q = 2 * x + y
        locs, sends = [], []
        for a, (off, rows) in enumerate(_STACK):
            loc = pltpu.make_async_copy(ins[a].at[q], land.at[q, pl.ds(off, rows)], lsem.at[a])
            loc.start()
            locs.append(loc)
            for j in range(1, 4):
                peer = (x ^ ((j >> 1) & 1), y ^ (j & 1), c)
                cp = pltpu.make_async_remote_copy(
                    src_ref=ins[a].at[q ^ j], dst_ref=land.at[q, pl.ds(off, rows)],
                    send_sem=ssem.at[a, j], recv_sem=rsem.at[a, j], device_id=peer, device_id_type=MESH)
                cp.start()
                sends.append(cp)
        for a, (off, rows) in enumerate(_STACK):
            for j in range(1, 4):
                pltpu.make_async_remote_copy(
                    src_ref=ins[a].at[q], dst_ref=land.at[q ^ j, pl.ds(off, rows)],
                    send_sem=ssem.at[a, j], recv_sem=rsem.at[a, j], device_id=(x, y, c),
                    device_id_type=MESH).wait_recv()
        for cp in sends:
            cp.wait_send()
        for loc in locs:
            loc.wait()

    return pl.pallas_call(
        body, name="grad_exchange",
        out_shape=jax.ShapeDtypeStruct((N_SHARD, STACK_ROWS, D_MODEL), F32),
        in_specs=[ANY] * n, out_specs=ANY,
        scratch_shapes=[pltpu.SemaphoreType.DMA((n,)), pltpu.SemaphoreType.DMA((n, 4)),
                        pltpu.SemaphoreType.DMA((n, 4))],
    )(*parts)


def _sum_chips(land):
    tr = 256

    def body(l_ref, s_ref):
        s_ref[...] = ((l_ref[0] + l_ref[1]) + l_ref[2]) + l_ref[3]

    return pl.pallas_call(
        body, name="sum_chips", grid=(STACK_ROWS // tr,),
        out_shape=jax.ShapeDtypeStruct((STACK_ROWS, D_MODEL), F32),
        in_specs=[pl.BlockSpec((N_SHARD, tr, D_MODEL), lambda i: (0, i, 0))],
        out_specs=pl.BlockSpec((tr, D_MODEL), lambda i: (i, 0)),
        compiler_params=_cparams(1),
    )(land)


def _core_swap(part):
    def body(p_ref, o_ref, ssem, rsem):
        x, y, c = _position()
        cp = pltpu.make_async_remote_copy(src_ref=p_ref, dst_ref=o_ref, send_sem=ssem, recv_sem=rsem,
                                          device_id=(x, y, 1 - c), device_id_type=MESH)
        cp.start()
        cp.wait()

    return pl.pallas_call(
        body, name="core_swap",
        out_shape=jax.ShapeDtypeStruct(part.shape, part.dtype),
        in_specs=[ANY], out_specs=ANY,
        scratch_shapes=[pltpu.SemaphoreType.DMA, pltpu.SemaphoreType.DMA],
    )(part)


def _add_pair(mine, other):
    tr = 256

    def body(a_ref, b_ref, o_ref):
        o_ref[...] = a_ref[...] + b_ref[...]

    spec = pl.BlockSpec((tr, D_MODEL), lambda i: (i, 0))
    return pl.pallas_call(
        body, name="add_pair", grid=(STACK_ROWS // tr,),
        out_shape=jax.ShapeDtypeStruct((STACK_ROWS, D_MODEL), F32),
        in_specs=[spec, spec], out_specs=spec,
        compiler_params=_cparams(1),
    )(mine, other)


def _pad_lanes(v, width=D_MODEL):
    v = v.reshape(1, -1)
    return jnp.pad(v, ((0, 0), (0, width - v.shape[1])))


def _pack_small(b_ada, ln1_g, ln1_b, ln2_g, ln2_b, b_dw, gn_g, gn_b, g_attn, g_conv, b_forget, w_dw_full, last):
    rows = [b_ada.reshape(6, D_MODEL)] + [_pad_lanes(v) for v in
                                          (ln1_g, ln1_b, ln2_g, ln2_b, b_dw, gn_g, gn_b, g_attn, g_conv, b_forget)]
    rows.append(jnp.pad(w_dw_full.reshape(CONV_K, -1), ((0, 0), (0, D_MODEL - w_dw_full.reshape(CONV_K, -1).shape[1]))))
    rows.append(_pad_lanes(last))
    return jnp.concatenate(rows, axis=0)


def _unpack_small(p):
    return dict(b_ada=p[0:6].reshape(1, 6 * D_MODEL), ln1_g=p[6:7], ln1_b=p[7:8], ln2_g=p[8:9], ln2_b=p[9:10],
                b_dw=p[10:11, :512], gn_g=p[11:12, :512], gn_b=p[12:13, :512], g_attn_out=p[13:14, :512],
                g_conv_out=p[14:15, :512], b_forget=p[15:16, :N_HEADS])


def kernel(x, c, w_ada, b_ada, w_in, b_forget, w_dw, b_dw, gn_g, gn_b, g_attn_out, g_conv_out, w_out, ln1_g, ln1_b, w_ff1, w_ff2, ln2_g, ln2_b, loss_target, m_w_ada, m_b_ada, m_w_in, m_b_forget, m_w_dw, m_b_dw, m_gn_g, m_gn_b, m_g_attn_out, m_g_conv_out, m_w_out, m_ln1_g, m_ln1_b, m_w_ff1, m_w_ff2, m_ln2_g, m_ln2_b, v_w_ada, v_b_ada, v_w_in, v_b_forget, v_w_dw, v_b_dw, v_gn_g, v_gn_b, v_g_attn_out, v_g_conv_out, v_w_out, v_ln1_g, v_ln1_b, v_w_ff1, v_w_ff2, v_ln2_g, v_ln2_b):
    seq = x.shape[1]
    ts = min(512, seq // 2)
    tq = min(512, seq // 2)
    ts_mid = min(256, seq // 2)
    q_idx = 2 * lax.axis_index("x") + lax.axis_index("y")
    xs = x[0]
    tgt = loss_target[0]

    sc_all, ada = _ada_fwd(c, w_ada[0], b_ada)
    w_in_sh = jnp.pad(w_in[0], ((0, 0), (0, IN_SHARD_PAD - IN_SHARD))).astype(BF16)
    wdw_rows = jnp.pad(w_dw[0, :, 0, :], ((0, 1), (0, 0)))
    win_all, wout_all, w1_all, w2_all, wdw_all = _weight_gather(
        [w_in_sh, w_out[0].astype(BF16), w_ff1[0].astype(BF16), w_ff2[0].astype(BF16), wdw_rows])
    w_in_full = jnp.transpose(win_all[:, :, :IN_SHARD], (1, 0, 2)).reshape(D_MODEL, N_IN)
    w_p = jnp.concatenate([w_in_full[:, 0:1536], w_in_full[:, 1544:2568], w_in_full[:, 1536:1544],
                           jnp.zeros((D_MODEL, 120), BF16)], axis=1)
    w_out_full = wout_all.reshape(D_MODEL, D_MODEL)
    bf = _pad_lanes(b_forget, 128)
    wdw_full = lax.reduce_precision(jnp.transpose(wdw_all, (1, 0, 2)).reshape(32, 512), 8, 7)

    prm = jnp.concatenate([b_dw, gn_g, gn_b, g_attn_out, g_conv_out, jnp.zeros((3, 512), F32)], axis=0)
    ln1 = jnp.concatenate([ln1_g, ln1_b], axis=0)
    ln2 = jnp.concatenate([ln2_g, ln2_b], axis=0)
    ch = jnp.arange(512)
    gm = (ch[:, None] // HEAD_DIM == ch[None, :] // HEAD_DIM).astype(F32) / HEAD_DIM
    sel = (ch[:, None] // HEAD_DIM + N_HEADS == jnp.arange(128)[None, :]).astype(F32)

    q, k, v, a, g, logf, fc, fr = _inproj_fwd(xs, ada, w_p, bf, ts)
    o, lse = _attn_fwd(q, k, v, fc, fr, tq)
    x1 = _mid_fwd(a, g, o, xs, ada, wdw_full, prm, ln1, w_out_full, gm, ts_mid)
    dff, dx1, pg_f = _ffn_fwd(x1, ada, w1_all, w2_all, ln2, tgt, ts)

    dw1, dw2, pg_b = [], [], jnp.zeros((8, D_MODEL), F32)
    for f in range(N_SHARD):
        dx1, dw1_f, dw2_f, pg_bf = _ffn_bwd_chunk(f, x1, ada, w1_all, w2_all, dff, dx1, ts)
        dw1.append(dw1_f)
        dw2.append(dw2_f)
        pg_b = pg_b + pg_bf
    dw1 = jnp.concatenate(dw1, axis=0)
    dw2 = jnp.concatenate(dw2, axis=0)
    do, da, dg, dxa, stats, dwo, pgm, pgc, dwdw = _mid_bwd(a, g, o, xs, dx1, ada, wdw_full, prm, ln1, w_out_full,
                                                            gm, sel, lse, fc, ts_mid)
    dq, drs, dk, dv, dfc = _attn_bwd(q, k, v, do, stats, fr, tq)
    gx, dwp, pgi, dbf = _inproj_bwd(xs, ada, w_p, dq, dk, dv, da, dg, dfc, drs, logf, dxa, ts)

    d_ada = jnp.concatenate([pgi[1:2], pgi[0:1], pgm[2:3], pg_b[1:2], pg_b[0:1], pg_f[2:3]], axis=0)
    packed = _pack_small(d_ada, pgm[0:1], pgm[1:2], pg_f[0:1], pg_f[1:2], pgc[0:1], pgc[1:2], pgc[2:3], pgc[3:4],
                         pgc[4:5], dbf[0:1, :N_HEADS], dwdw[0:CONV_K], pg_f[3:4])
    small_sum, small_all = _small_reduce(packed)
    loss = small_sum[SMALL_ROWS - 1, 0]
    gsm = _unpack_small(small_sum)
    g_wdw = lax.dynamic_slice(small_sum[16:16 + CONV_K, :512], (0, q_idx * 128), (CONV_K, 128))

    zrow = jnp.zeros((CONV_K + 1, D_MODEL), F32)
    w_small = _pack_small(b_ada, ln1_g, ln1_b, ln2_g, ln2_b, b_dw, gn_g, gn_b, g_attn_out,
                          g_conv_out, b_forget, zrow[:CONV_K, :512], zrow[0])
    m_small = _pack_small(m_b_ada, m_ln1_g, m_ln1_b, m_ln2_g, m_ln2_b, m_b_dw, m_gn_g, m_gn_b, m_g_attn_out,
                          m_g_conv_out, m_b_forget, zrow[:CONV_K, :512], zrow[0])
    v_small = _pack_small(v_b_ada, v_ln1_g, v_ln1_b, v_ln2_g, v_ln2_b, v_b_dw, v_gn_g, v_gn_b, v_g_attn_out,
                          v_g_conv_out, v_b_forget, zrow[:CONV_K, :512], zrow[0])
    d_small, mn_small, vn_small = (_unpack_small(t) for t in _adamw(small_sum, w_small, m_small, v_small, "adamw_small"))
    d_wdw, mn_wdw, vn_wdw = _adamw(g_wdw, w_dw[0, :, 0, :], m_w_dw[0, :, 0, :], v_w_dw[0, :, 0, :], "adamw_wdw")

    dd = lax.dynamic_slice(small_all[:, 0:6, :].reshape(8, 6 * D_MODEL), (0, q_idx * 1536), (8, 1536))
    g_wada, d_wada, mn_wada, vn_wada = _w_ada_update(sc_all.T, dd, w_ada[0], m_w_ada[0], v_w_ada[0])

    dw_in_cols = jnp.concatenate([dwp[:, 0:1536], dwp[:, 2560:2568], dwp[:, 1536:2560]], axis=1)
    dw_in_sh = jnp.pad(jnp.transpose(dw_in_cols.reshape(D_MODEL, N_SHARD, IN_SHARD), (1, 0, 2)),
                       ((0, 0), (0, 0), (0, IN_SHARD_PAD - IN_SHARD))).reshape(N_SHARD, 768, D_MODEL)
    land = _grad_exchange([dw_in_sh, dwo.reshape(N_SHARD, 256, D_MODEL), dw1, dw2])
    part = _sum_chips(land)
    total = _add_pair(part, _core_swap(part))
    g_win = total[0:768].reshape(D_MODEL, IN_SHARD_PAD)[:, :IN_SHARD]
    g_wout = total[768:1024]
    g_w1 = total[1024:2048]
    g_w2 = total[2048:3072]
    d_win, mn_win, vn_win = _adamw(g_win, w_in[0], m_w_in[0], v_w_in[0], "adamw_w_in")
    d_wout, mn_wout, vn_wout = _adamw(g_wout, w_out[0], m_w_out[0], v_w_out[0], "adamw_w_out")
    d_w1, mn_w1, vn_w1 = _adamw(g_w1, w_ff1[0], m_w_ff1[0], v_w_ff1[0], "adamw_w_ff1")
    d_w2, mn_w2, vn_w2 = _adamw(g_w2, w_ff2[0], m_w_ff2[0], v_w_ff2[0], "adamw_w_ff2")

    def group(wada, sm, win, wdw, wout, w1, w2):
        return (wada[None], sm["b_ada"], win[None], sm["b_forget"], wdw[None, :, None, :], sm["b_dw"], sm["gn_g"],
                sm["gn_b"], sm["g_attn_out"], sm["g_conv_out"], wout[None], sm["ln1_g"], sm["ln1_b"], w1[None],
                w2[None], sm["ln2_g"], sm["ln2_b"])

    return ((loss, gx[None])
            + group(g_wada, gsm, g_win, g_wdw, g_wout, g_w1, g_w2)
            + group(d_wada, d_small, d_win, d_wdw, d_wout, d_w1, d_w2)
            + group(mn_wada, mn_small, mn_win, mn_wdw, mn_wout, mn_w1, mn_w2)
            + group(vn_wada, vn_small, vn_win, vn_wdw, vn_wout, vn_w1, vn_w2))
```

```python
import functools

import jax
import jax.numpy as jnp
from jax import lax
from jax.experimental import pallas as pl
from jax.experimental.pallas import tpu as pltpu

F32 = jnp.float32
BF16 = jnp.bfloat16
MESH = pl.DeviceIdType.MESH
ANY = pl.BlockSpec(memory_space=pl.ANY)

D_MODEL = 1024
HEAD_DIM = 64
ATTN_W = 512
CONV_W = 512
N_HEADS = 8
N_PAIRS = 4
CONV_K = 31
HALO = 32
D_FF = 4096
N_SHARD = 4
FF_CHUNK = D_FF // N_SHARD
N_IN = 2568
IN_SHARD = N_IN // N_SHARD
IN_SHARD_PAD = 768
N_IN_PAD = 5 * 512 + 128
LN_EPS = 1e-5
ALPHA = 2.0 ** 0.25
LR, B1, B2, ADAM_EPS, WD, STEP = 0.001, 0.9, 0.999, 1e-08, 0.01, 10
VMEM_LIMIT = 56 * 1024 * 1024
SMALL_ROWS = 48
STACK_ROWS = 768 + 256 + 1024 + 1024
_STACK_OUT = ((0, 256),)
_STACK_FF = ((0, 1024), (1024, 1024))


def _cparams(n_axes):
    return pltpu.CompilerParams(dimension_semantics=("arbitrary",) * n_axes, vmem_limit_bytes=VMEM_LIMIT)


def _dot(a, b):
    return jnp.dot(a, b, preferred_element_type=F32)


def _dot_nt(a, b):
    return lax.dot_general(a, b, (((1,), (1,)), ((), ())), preferred_element_type=F32)


def _dot_tn(a, b):
    return lax.dot_general(a, b, (((0,), (0,)), ((), ())), preferred_element_type=F32)


def _dot_f32(a, b):
    hi, mid, lo = _split3(a)
    return _dot(hi, b) + _dot(mid, b) + _dot(lo, b)


def _split3(x):
    hi = x.astype(BF16)
    r = x - hi.astype(F32)
    mid = r.astype(BF16)
    lo = (r - mid.astype(F32)).astype(BF16)
    return hi, mid, lo


def _tri_dot(tri, x):
    hi, mid, lo = _split3(x)
    return _dot(tri, hi) + _dot(tri, mid) + _dot(tri, lo)


def _rowsum(x):
    return jnp.sum(x, axis=0, keepdims=True)


def _mean_last(x):
    return jnp.mean(x, axis=-1, keepdims=True)


def _position():
    x, y, c = lax.axis_index("x"), lax.axis_index("y"), lax.axis_index("c")
    return x, y, c


def _ada_fwd(c_row, w_ada, b_ada):
    n_col = w_ada.shape[1]

    def body(c_ref, w_ref, b_ref, sc_ref, ada_ref, call_ref, part_ref, pall_ref, s1, r1, s2, r2):
        x, y, c = _position()
        me = 4 * x + 2 * y + c
        q = 2 * x + y
        call_ref[me] = jnp.broadcast_to(c_ref[...], (8, D_MODEL))

        def c_copy(k):
            peer = (x ^ ((k >> 2) & 1), y ^ ((k >> 1) & 1), c ^ (k & 1))
            return pltpu.make_async_remote_copy(
                src_ref=call_ref.at[me], dst_ref=call_ref.at[me], send_sem=s1.at[k], recv_sem=r1.at[k],
                device_id=peer, device_id_type=MESH)

        def c_recv(k):
            src = me ^ k
            return pltpu.make_async_remote_copy(
                src_ref=call_ref.at[src], dst_ref=call_ref.at[src], send_sem=s1.at[k], recv_sem=r1.at[k],
                device_id=(x, y, c), device_id_type=MESH)

        sends = [c_copy(k) for k in range(1, 8)]
        for cp in sends:
            cp.start()
        for k in range(1, 8):
            c_recv(k).wait_recv()
        for cp in sends:
            cp.wait_send()

        row = lax.broadcasted_iota(jnp.int32, (8, D_MODEL), 0)
        c_all = jnp.zeros((8, D_MODEL), F32)
        for j in range(8):
            c_all = jnp.where(row == j, call_ref[j], c_all)
        sc_all = c_all * jax.nn.sigmoid(c_all)
        sc_ref[...] = sc_all
        b_slice = b_ref[:, pl.ds(pl.multiple_of(q * n_col, 128), n_col)]
        part = _dot(sc_all.astype(BF16), w_ref[...].astype(BF16)) + b_slice
        part_ref[...] = part
        pall_ref[q] = part

        def p_copy(j):
            peer = (x ^ ((j >> 1) & 1), y ^ (j & 1), c)
            return pltpu.make_async_remote_copy(
                src_ref=part_ref, dst_ref=pall_ref.at[q], send_sem=s2.at[j], recv_sem=r2.at[j],
                device_id=peer, device_id_type=MESH)

        def p_recv(j):
            src_q = q ^ j
            return pltpu.make_async_remote_copy(
                src_ref=part_ref, dst_ref=pall_ref.at[src_q], send_sem=s2.at[j], recv_sem=r2.at[j],
                device_id=(x, y, c), device_id_type=MESH)

        sends2 = [p_copy(j) for j in range(1, 4)]
        for cp in sends2:
            cp.start()
        for j in range(1, 4):
            p_recv(j).wait_recv()
        for cp in sends2:
            cp.wait_send()
        for qq in range(N_SHARD):
            ada_ref[qq] = pall_ref[qq, pl.ds(me, 1), :]

    vm = pl.BlockSpec(memory_space=pltpu.VMEM)
    sc_all, ada = pl.pallas_call(
        body, name="ada_fwd",
        out_shape=(jax.ShapeDtypeStruct((8, D_MODEL), F32), jax.ShapeDtypeStruct((N_SHARD, 1, n_col), F32)),
        in_specs=[vm, vm, vm], out_specs=(vm, vm),
        scratch_shapes=[pltpu.VMEM((8, 8, D_MODEL), F32), pltpu.VMEM((8, n_col), F32),
                        pltpu.VMEM((N_SHARD, 8, n_col), F32),
                        pltpu.SemaphoreType.DMA((8,)), pltpu.SemaphoreType.DMA((8,)),
                        pltpu.SemaphoreType.DMA((4,)), pltpu.SemaphoreType.DMA((4,))],
        compiler_params=pltpu.CompilerParams(vmem_limit_bytes=VMEM_LIMIT),
    )(c_row, w_ada, b_ada)
    return sc_all, ada.reshape(6, D_MODEL)


class _ChipCopies:
    def __init__(self, lsem, ssem, rsem):
        self.x, self.y, self.c = _position()
        self.q = 2 * self.x + self.y
        self.lsem, self.ssem, self.rsem = lsem, ssem, rsem
        self.local, self.send, self.recv = [], [], []

    def _remote(self, a, j, src, dst, peer):
        return pltpu.make_async_remote_copy(src_ref=src, dst_ref=dst, send_sem=self.ssem.at[a, j],
                                            recv_sem=self.rsem.at[a, j], device_id=peer, device_id_type=MESH)

    def add(self, a, own_src, own_dst, src_for, dst_mine, dst_from):
        x, y, c, q = self.x, self.y, self.c, self.q
        self.local.append(pltpu.make_async_copy(own_src, own_dst, self.lsem.at[a]))
        for j in range(1, 4):
            peer = (x ^ ((j >> 1) & 1), y ^ (j & 1), c)
            self.send.append(self._remote(a, j, src_for(q ^ j), dst_mine, peer))
            self.recv.append(self._remote(a, j, own_src, dst_from(q ^ j), (x, y, c)))

    def start(self):
        for cp in self.local + self.send:
            cp.start()

    def wait(self):
        for cp in self.recv:
            cp.wait_recv()
        for cp in self.send:
            cp.wait_send()
        for cp in self.local:
            cp.wait()


def _gather_copies(ins, outs, lsem, ssem, rsem):
    cps = _ChipCopies(lsem, ssem, rsem)
    for a in range(len(ins)):
        cps.add(a, ins[a], outs[a].at[cps.q], lambda chip, a=a: ins[a], outs[a].at[cps.q],
                lambda chip, a=a: outs[a].at[chip])
    return cps


def _scatter_copies(ins, land, offs, lsem, ssem, rsem):
    cps = _ChipCopies(lsem, ssem, rsem)
    for a, (off, rows) in enumerate(offs):
        cps.add(a, ins[a].at[cps.q], land.at[cps.q, pl.ds(off, rows)], lambda chip, a=a: ins[a].at[chip],
                land.at[cps.q, pl.ds(off, rows)], lambda chip, off=off, rows=rows: land.at[chip, pl.ds(off, rows)])
    return cps


def _copy_sems(n):
    return [pltpu.SemaphoreType.DMA((n,)), pltpu.SemaphoreType.DMA((n, 4)), pltpu.SemaphoreType.DMA((n, 4))]


def _weight_gather(shards):
    n = len(shards)

    def body(*refs):
        cps = _gather_copies(refs[:n], refs[n:2 * n], *refs[2 * n:])
        cps.start()
        cps.wait()

    return pl.pallas_call(
        body, name="weight_gather",
        out_shape=tuple(jax.ShapeDtypeStruct((N_SHARD,) + s.shape, s.dtype) for s in shards),
        in_specs=[ANY] * n, out_specs=tuple([ANY] * n),
        scratch_shapes=[pltpu.SemaphoreType.DMA((n,)), pltpu.SemaphoreType.DMA((n, 4)),
                        pltpu.SemaphoreType.DMA((n, 4))],
    )(*shards)


def _inproj_fwd(x, ada, w_p, bf, ts):
    s = x.shape[0]
    ns = s // ts

    def body(x_ref, ada_ref, w_ref, bf_ref, q_ref, k_ref, v_ref, a_ref, g_ref, lf_ref, fc_ref, fr_ref, carry):
        i = pl.program_id(0)

        @pl.when(i == 0)
        def _():
            carry[...] = jnp.zeros_like(carry)

        u = (x_ref[...] * (1.0 + ada_ref[1:2, :]) + ada_ref[0:1, :]).astype(BF16)
        proj = _dot(u, w_ref[...])
        q_ref[...] = (proj[:, 0:512] * (HEAD_DIM ** -0.5)).astype(BF16)
        k_ref[...] = proj[:, 512:1024].astype(BF16)
        v_ref[...] = proj[:, 1024:1536].astype(BF16)
        a_ref[...] = proj[:, 1536:2048]
        g_ref[...] = proj[:, 2048:2560]
        z = proj[:, 2560:2688] + bf_ref[...]
        lane = lax.broadcasted_iota(jnp.int32, (ts, 128), 1)
        logf = jnp.minimum(z, 0.0) - jnp.log(1.0 + jnp.exp(-jnp.abs(z)))
        logf = jnp.where(lane < N_HEADS, logf, 0.0)
        lf_ref[...] = logf
        r = lax.broadcasted_iota(jnp.int32, (ts, ts), 0)
        cc = lax.broadcasted_iota(jnp.int32, (ts, ts), 1)
        tri = (cc <= r).astype(BF16)
        fc = _tri_dot(tri, logf) + carry[...]
        fc_ref[...] = fc
        carry[...] = fc[ts - 1:ts, :]
        fr_ref[...] = fc.T[0:8, :]

    row = lambda i: (i, 0)
    full = lambda i: (0, 0)
    return pl.pallas_call(
        body, name="inproj_fwd", grid=(ns,),
        out_shape=(jax.ShapeDtypeStruct((s, 512), BF16),) * 3 + (jax.ShapeDtypeStruct((s, 512), F32),) * 2
        + (jax.ShapeDtypeStruct((s, 128), F32),) * 2 + (jax.ShapeDtypeStruct((8, s), F32),),
        in_specs=[pl.BlockSpec((ts, D_MODEL), row), pl.BlockSpec((6, D_MODEL), full),
                  pl.BlockSpec((D_MODEL, N_IN_PAD), full), pl.BlockSpec((1, 128), full)],
        out_specs=(pl.BlockSpec((ts, 512), row),) * 5 + (pl.BlockSpec((ts, 128), row),) * 2
        + (pl.BlockSpec((8, ts), lambda i: (0, i)),),
        scratch_shapes=[pltpu.VMEM((1, 128), F32)],
        compiler_params=_cparams(1),
    )(x, ada, w_p, bf)


def _attn_fwd(q, k, v, fc, fr, shards, tq):
    s = q.shape[0]
    nq = s // tq
    tk = tq
    n = len(shards)

    def body(q_ref, k_hbm, v_hbm, fc_ref, fr_ref, *rest):
        sh_in, (o_ref, lse_ref), sh_out = rest[:n], rest[n:n + 2], rest[n + 2:2 * n + 2]
        k_vm, v_vm, sem, lsem, ssem, rsem = rest[2 * n + 2:]
        i = pl.program_id(0)

        @pl.when(i == 0)
        def _():
            _gather_copies(sh_in, sh_out, lsem, ssem, rsem).start()
            ck = pltpu.make_async_copy(k_hbm, k_vm, sem.at[0])
            cv = pltpu.make_async_copy(v_hbm, v_vm, sem.at[1])
            ck.start()
            cv.start()
            ck.wait()
            cv.wait()

        @pl.when(i == nq - 1)
        def _():
            _gather_copies(sh_in, sh_out, lsem, ssem, rsem).wait()

        lane = lax.broadcasted_iota(jnp.int32, (tq, 128), 1)
        low = lane < HEAD_DIM
        t_off = lax.broadcasted_iota(jnp.int32, (tq, tk), 0)
        s_off = lax.broadcasted_iota(jnp.int32, (tq, tk), 1)
        lse_tile = jnp.zeros((tq, 128), F32)
        for p in range(N_PAIRS):
            cols = slice(128 * p, 128 * p + 128)
            qb = q_ref[:, cols]
            qm = (jnp.where(low, qb, jnp.zeros_like(qb)), jnp.where(low, jnp.zeros_like(qb), qb))
            ft = (fc_ref[:, 2 * p:2 * p + 1], fc_ref[:, 2 * p + 1:2 * p + 2])

            def step(j, carry, diagonal):
                start = pl.multiple_of(j * tk, tk)
                kb = k_vm[pl.ds(start, tk), cols]
                vb = v_vm[pl.ds(start, tk), cols]
                out = []
                for hh in range(2):
                    m, l, acc = carry[3 * hh:3 * hh + 3]
                    fs = fr_ref[2 * p + hh:2 * p + hh + 1, pl.ds(start, tk)]
                    sc = (_dot_nt(qm[hh], kb) + ft[hh]) - fs
                    if diagonal:
                        sc = jnp.where(s_off <= t_off, sc, -jnp.inf)
                    m_new = jnp.maximum(m, jnp.max(sc, axis=-1, keepdims=True))
                    scale = jnp.exp(m - m_new)
                    pr = jnp.exp(sc - m_new)
                    l = l * scale + jnp.sum(pr, axis=-1, keepdims=True)
                    acc = acc * scale + _dot(pr.astype(BF16), vb)
                    out += [m_new, l, acc]
                return tuple(out)

            init = (jnp.full((tq, 1), -jnp.inf, F32), jnp.zeros((tq, 1), F32), jnp.zeros((tq, 128), F32)) * 2
            res = lax.fori_loop(0, i, functools.partial(step, diagonal=False), init)
            res = step(i, res, diagonal=True)
            o0 = res[2] / res[1]
            o1 = res[5] / res[4]
            o_ref[:, cols] = jnp.where(low, o0, o1)
            lse_tile = jnp.where(lane == 2 * p, res[0] + jnp.log(res[1]), lse_tile)
            lse_tile = jnp.where(lane == 2 * p + 1, res[3] + jnp.log(res[4]), lse_tile)
        lse_ref[...] = lse_tile

    row = lambda i: (i, 0)
    res = pl.pallas_call(
        body, name="attn_fwd", grid=(nq,),
        out_shape=(jax.ShapeDtypeStruct((s, 512), F32), jax.ShapeDtypeStruct((s, 128), F32))
        + tuple(jax.ShapeDtypeStruct((N_SHARD,) + w.shape, w.dtype) for w in shards),
        in_specs=[pl.BlockSpec((tq, 512), row), ANY, ANY, pl.BlockSpec((tq, 128), row),
                  pl.BlockSpec((8, s), lambda i: (0, 0))] + [ANY] * n,
        out_specs=(pl.BlockSpec((tq, 512), row), pl.BlockSpec((tq, 128), row)) + (ANY,) * n,
        scratch_shapes=[pltpu.VMEM((s, 512), BF16), pltpu.VMEM((s, 512), BF16), pltpu.SemaphoreType.DMA((2,))]
        + _copy_sems(n),
        compiler_params=_cparams(1),
    )(q, k, v, fc, fr, *shards)
    return res[0], res[1], res[2:]


def _conv_branch(a, g, ah, gh, first, ugx, wdw_ref, prm_ref, gm, ts):
    sg_g = jax.nn.sigmoid(g)
    ug = (a * sg_g).astype(BF16).astype(F32)
    ugh = jnp.where(first, 0.0, (ah * jax.nn.sigmoid(gh)).astype(BF16).astype(F32))
    ugx[0:HALO, :] = ugh
    ugx[HALO:HALO + ts, :] = ug
    y = jnp.zeros((ts, CONV_W), F32) + prm_ref[0:1, :]
    for kk in range(CONV_K):
        off = HALO - (CONV_K - 1) + kk
        y = y + wdw_ref[kk:kk + 1, :] * ugx[off:off + ts, :]
    mu = _dot_f32(y, gm)
    d = y - mu
    var = _dot_f32(d * d, gm)
    rs = lax.rsqrt(var + LN_EPS)
    yhat = d * rs
    yn = yhat * prm_ref[1:2, :] + prm_ref[2:3, :]
    sg = jax.nn.sigmoid(yn)
    co = yn * sg
    return sg_g, rs, yhat, yn, sg, co


def _mix_inputs(o, co, prm_ref):
    ra = lax.rsqrt(_mean_last(o * o) + LN_EPS)
    oh = o * ra
    rc = lax.rsqrt(_mean_last(co * co) + LN_EPS)
    ch = co * rc
    mi = jnp.concatenate([oh * prm_ref[3:4, :], ch * prm_ref[4:5, :]], axis=-1).astype(BF16)
    return ra, oh, rc, ch, mi


def _layernorm_stats(r):
    mu = _mean_last(r)
    d = r - mu
    rstd = lax.rsqrt(_mean_last(d * d) + LN_EPS)
    return d * rstd, rstd


def _layernorm_bwd(dout, xh, rstd, gain):
    dxh = dout * gain
    return rstd * (dxh - _mean_last(dxh) - xh * _mean_last(dxh * xh))


def _halo_index(tile, ts):
    return jnp.maximum(tile * (ts // HALO) - 1, 0)


def _mid_fwd(a, g, o, x, ada, wdw, prm, ln1, w_out, gm, ts):
    s = x.shape[0]
    ns = s // ts

    def body(a_ref, g_ref, ah_ref, gh_ref, o_ref, x_ref, ada_ref, wdw_ref, prm_ref, ln_ref, wo_ref, gm_ref,
             x1_ref, ugx):
        i = pl.program_id(0)
        co = _conv_branch(a_ref[...], g_ref[...], ah_ref[...], gh_ref[...], i == 0, ugx, wdw_ref, prm_ref,
                          gm_ref[...], ts)[-1]
        mi = _mix_inputs(o_ref[...], co, prm_ref)[-1]
        mixed = _dot(mi, wo_ref[...])
        r1 = ALPHA * x_ref[...] + (1.0 + ada_ref[2:3, :]) * mixed
        xh, _ = _layernorm_stats(r1)
        x1_ref[...] = xh * ln_ref[0:1, :] + ln_ref[1:2, :]

    row = lambda i: (i, 0)
    full = lambda i: (0, 0)
    halo = lambda i: (_halo_index(i, ts), 0)
    return pl.pallas_call(
        body, name="mid_fwd", grid=(ns,),
        out_shape=jax.ShapeDtypeStruct((s, D_MODEL), F32),
        in_specs=[pl.BlockSpec((ts, 512), row), pl.BlockSpec((ts, 512), row),
                  pl.BlockSpec((HALO, 512), halo), pl.BlockSpec((HALO, 512), halo),
                  pl.BlockSpec((ts, 512), row), pl.BlockSpec((ts, D_MODEL), row),
                  pl.BlockSpec((6, D_MODEL), full), pl.BlockSpec((32, 512), full), pl.BlockSpec((8, 512), full),
                  pl.BlockSpec((2, D_MODEL), full), pl.BlockSpec((D_MODEL, D_MODEL), full),
                  pl.BlockSpec((512, 512), full)],
        out_specs=pl.BlockSpec((ts, D_MODEL), row),
        scratch_shapes=[pltpu.VMEM((ts + HALO, 512), F32)],
        compiler_params=_cparams(1),
    )(a, g, a, g, o, x, ada, wdw, prm, ln1, w_out, gm)


def _ffn_fwd(x1, ada, w1, w2, ln2, tgt, ts):
    s = x1.shape[0]
    ns = s // ts
    nf = N_SHARD

    def body(x1_ref, ada_ref, w1_ref, w2_ref, ln_ref, t_ref, dff_ref, dx1_ref, pg_ref, ffacc, u2):
        i = pl.program_id(0)
        f = pl.program_id(1)

        @pl.when((i == 0) & (f == 0))
        def _():
            pg_ref[...] = jnp.zeros_like(pg_ref)

        @pl.when(f == 0)
        def _():
            u2[...] = (x1_ref[...] * (1.0 + ada_ref[4:5, :]) + ada_ref[3:4, :]).astype(BF16)
            ffacc[...] = jnp.zeros_like(ffacc)

        h = _dot(u2[...], w1_ref[0])
        r = jnp.maximum(h, 0.0)
        ffacc[...] += _dot((r * r).astype(BF16), w2_ref[0])

        @pl.when(f == nf - 1)
        def _():
            ff = ffacc[...]
            r2 = ALPHA * x1_ref[...] + (1.0 + ada_ref[5:6, :]) * ff
            xh, rstd = _layernorm_stats(r2)
            yv = xh * ln_ref[0:1, :] + ln_ref[1:2, :]
            err = yv - t_ref[...]
            dy = err * (1.0 / D_MODEL)
            dr2 = _layernorm_bwd(dy, xh, rstd, ln_ref[0:1, :])
            pg_ref[0:1, :] += _rowsum(dy * xh)
            pg_ref[1:2, :] += _rowsum(dy)
            pg_ref[2:3, :] += _rowsum(dr2 * ff)
            pg_ref[3:4, :] += _rowsum(err * err) * (0.5 / D_MODEL)
            dff_ref[...] = ((1.0 + ada_ref[5:6, :]) * dr2).astype(BF16)
            dx1_ref[...] = ALPHA * dr2

    row = lambda i, f: (i, 0)
    full = lambda i, f: (0, 0)
    chunk = lambda i, f: (f, 0, 0)
    return pl.pallas_call(
        body, name="ffn_fwd", grid=(ns, nf),
        out_shape=(jax.ShapeDtypeStruct((s, D_MODEL), BF16), jax.ShapeDtypeStruct((s, D_MODEL), F32),
                   jax.ShapeDtypeStruct((8, D_MODEL), F32)),
        in_specs=[pl.BlockSpec((ts, D_MODEL), row), pl.BlockSpec((6, D_MODEL), full),
                  pl.BlockSpec((1, D_MODEL, FF_CHUNK), chunk), pl.BlockSpec((1, FF_CHUNK, D_MODEL), chunk),
                  pl.BlockSpec((2, D_MODEL), full), pl.BlockSpec((ts, D_MODEL), row)],
        out_specs=(pl.BlockSpec((ts, D_MODEL), row), pl.BlockSpec((ts, D_MODEL), row),
                   pl.BlockSpec((8, D_MODEL), full)),
        scratch_shapes=[pltpu.VMEM((ts, D_MODEL), F32), pltpu.VMEM((ts, D_MODEL), BF16)],
        compiler_params=_cparams(2),
    )(x1, ada, w1, w2, ln2, tgt)


def _ffn_bwd_chunk(f, x1, ada, w1, w2, dff, dx1, ts):
    s = x1.shape[0]
    ns = s // ts

    def body(x1_ref, ada_ref, w1_ref, w2_ref, dff_ref, dx1_in, dx1_out, dw1_ref, dw2_ref, pg_ref):
        i = pl.program_id(0)

        @pl.when(i == 0)
        def _():
            pg_ref[...] = jnp.zeros_like(pg_ref)
            dw1_ref[...] = jnp.zeros_like(dw1_ref)
            dw2_ref[...] = jnp.zeros_like(dw2_ref)

        x1v = x1_ref[...]
        u2 = (x1v * (1.0 + ada_ref[4:5, :]) + ada_ref[3:4, :]).astype(BF16)
        h = _dot(u2, w1_ref[0])
        r = jnp.maximum(h, 0.0)
        hid = (r * r).astype(BF16)
        dffv = dff_ref[...]
        dh = (_dot_nt(dffv, w2_ref[0]) * (2.0 * r)).astype(BF16)
        dw2_ref[0] += _dot_tn(hid, dffv)
        dw1_ref[0] += _dot_tn(u2, dh)
        du2 = _dot_nt(dh, w1_ref[0])
        dx1_out[...] = dx1_in[...] + du2 * (1.0 + ada_ref[4:5, :])
        pg_ref[0:1, :] += _rowsum(du2 * x1v)
        pg_ref[1:2, :] += _rowsum(du2)

    row = lambda i: (i, 0)
    full = lambda i: (0, 0)
    full3 = lambda i: (0, 0, 0)
    chunk = lambda i: (f, 0, 0)
    return pl.pallas_call(
        body, name=f"ffn_bwd_{f}", grid=(ns,),
        out_shape=(jax.ShapeDtypeStruct((s, D_MODEL), F32), jax.ShapeDtypeStruct((1, D_MODEL, FF_CHUNK), F32),
                   jax.ShapeDtypeStruct((1, FF_CHUNK, D_MODEL), F32), jax.ShapeDtypeStruct((8, D_MODEL), F32)),
        in_specs=[pl.BlockSpec((ts, D_MODEL), row), pl.BlockSpec((6, D_MODEL), full),
                  pl.BlockSpec((1, D_MODEL, FF_CHUNK), chunk), pl.BlockSpec((1, FF_CHUNK, D_MODEL), chunk),
                  pl.BlockSpec((ts, D_MODEL), row), pl.BlockSpec((ts, D_MODEL), row)],
        out_specs=(pl.BlockSpec((ts, D_MODEL), row), pl.BlockSpec((1, D_MODEL, FF_CHUNK), full3),
                   pl.BlockSpec((1, FF_CHUNK, D_MODEL), full3), pl.BlockSpec((8, D_MODEL), full)),
        compiler_params=_cparams(1),
    )(x1, ada, w1, w2, dff, dx1)


def _mid_bwd(a, g, o, x, dx1, ada, wdw, prm, ln1, w_out, gm, sel, lse, fc, dw1, dw2, ts):
    s = x.shape[0]
    ns = s // ts

    def body(a_ref, g_ref, ah_ref, gh_ref, o_ref, x_ref, dx1_ref, ada_ref, wdw_ref, prm_ref, ln_ref, wo_ref,
             gm_ref, sel_ref, lse_ref, fc_ref, dw1_hbm, dw2_hbm,
             do_ref, da_ref, dg_ref, dxa_ref, st_ref, dwo_ref, pgm_ref, pgc_ref, dwdw_ref, land_hbm,
             ugx, dyx, lsem, ssem, rsem):
        i = pl.program_id(0)
        tile = ns - 1 - i

        def exchange():
            return _scatter_copies((dw1_hbm, dw2_hbm), land_hbm, _STACK_FF, lsem, ssem, rsem)

        @pl.when(i == 0)
        def _():
            exchange().start()
            dwo_ref[...] = jnp.zeros_like(dwo_ref)
            pgm_ref[...] = jnp.zeros_like(pgm_ref)
            pgc_ref[...] = jnp.zeros_like(pgc_ref)
            dwdw_ref[...] = jnp.zeros_like(dwdw_ref)
            dyx[ts:ts + HALO, :] = jnp.zeros((HALO, 512), F32)

        gmv = gm_ref[...]
        av = a_ref[...]
        ov = o_ref[...]
        sg_g, rs, yhat, yn, sg, co = _conv_branch(av, g_ref[...], ah_ref[...], gh_ref[...], tile == 0, ugx,
                                                  wdw_ref, prm_ref, gmv, ts)
        ra, oh, rc, ch, mi = _mix_inputs(ov, co, prm_ref)
        mixed = _dot(mi, wo_ref[...])
        gt1 = 1.0 + ada_ref[2:3, :]
        r1 = ALPHA * x_ref[...] + gt1 * mixed
        xh, rstd = _layernorm_stats(r1)
        dx1 = dx1_ref[...]
        pgm_ref[0:1, :] += _rowsum(dx1 * xh)
        pgm_ref[1:2, :] += _rowsum(dx1)
        dr1 = _layernorm_bwd(dx1, xh, rstd, ln_ref[0:1, :])
        dxa_ref[...] = ALPHA * dr1
        pgm_ref[2:3, :] += _rowsum(dr1 * mixed)
        dmixed = (gt1 * dr1).astype(BF16)
        dmi = _dot_nt(dmixed, wo_ref[...])
        dwo_ref[...] += _dot_tn(mi, dmixed)
        dna = dmi[:, 0:512]
        dnc = dmi[:, 512:1024]
        pgc_ref[3:4, :] += _rowsum(dna * oh)
        doh = dna * prm_ref[3:4, :]
        do = ra * (doh - oh * _mean_last(doh * oh))
        do_ref[...] = do.astype(BF16)
        lane = lax.broadcasted_iota(jnp.int32, (ts, 128), 1)
        delta = _dot_f32(do * ov, sel_ref[...])
        st_ref[...] = jnp.where(lane < N_HEADS, fc_ref[...] - lse_ref[...], delta)
        pgc_ref[4:5, :] += _rowsum(dnc * ch)
        dch = dnc * prm_ref[4:5, :]
        dco = rc * (dch - ch * _mean_last(dch * ch))
        dyn = dco * (sg * (1.0 + yn * (1.0 - sg)))
        pgc_ref[1:2, :] += _rowsum(dyn * yhat)
        pgc_ref[2:3, :] += _rowsum(dyn)
        dyh = dyn * prm_ref[1:2, :]
        dy = rs * (dyh - _dot_f32(dyh, gmv) - yhat * _dot_f32(dyh * yhat, gmv))
        pgc_ref[0:1, :] += _rowsum(dy)
        dyr = dy.astype(BF16).astype(F32)
        dyx[0:ts, :] = dyr
        dug = jnp.zeros((ts, CONV_W), F32)
        for kk in range(CONV_K):
            off = HALO - (CONV_K - 1) + kk
            dwdw_ref[kk:kk + 1, :] += _rowsum(dyr * ugx[off:off + ts, :])
            back = CONV_K - 1 - kk
            dug = dug + wdw_ref[kk:kk + 1, :] * dyx[back:back + ts, :]
        dyx[ts:ts + HALO, :] = dyr[0:HALO, :]
        da_ref[...] = (dug * sg_g).astype(BF16)
        dg_ref[...] = (dug * av * sg_g * (1.0 - sg_g)).astype(BF16)

        @pl.when(i == ns - 1)
        def _():
            exchange().wait()

    row = lambda i: (ns - 1 - i, 0)
    full = lambda i: (0, 0)
    halo = lambda i: (_halo_index(ns - 1 - i, ts), 0)
    return pl.pallas_call(
        body, name="mid_bwd", grid=(ns,),
        out_shape=(jax.ShapeDtypeStruct((s, 512), BF16), jax.ShapeDtypeStruct((s, 512), BF16),
                   jax.ShapeDtypeStruct((s, 512), BF16), jax.ShapeDtypeStruct((s, D_MODEL), F32),
                   jax.ShapeDtypeStruct((s, 128), F32), jax.ShapeDtypeStruct((D_MODEL, D_MODEL), F32),
                   jax.ShapeDtypeStruct((8, D_MODEL), F32), jax.ShapeDtypeStruct((8, 512), F32),
                   jax.ShapeDtypeStruct((32, 512), F32), jax.ShapeDtypeStruct((N_SHARD, 2 * FF_CHUNK, D_MODEL), F32)),
        in_specs=[pl.BlockSpec((ts, 512), row), pl.BlockSpec((ts, 512), row),
                  pl.BlockSpec((HALO, 512), halo), pl.BlockSpec((HALO, 512), halo),
                  pl.BlockSpec((ts, 512), row), pl.BlockSpec((ts, D_MODEL), row), pl.BlockSpec((ts, D_MODEL), row),
                  pl.BlockSpec((6, D_MODEL), full), pl.BlockSpec((32, 512), full), pl.BlockSpec((8, 512), full),
                  pl.BlockSpec((2, D_MODEL), full), pl.BlockSpec((D_MODEL, D_MODEL), full),
                  pl.BlockSpec((512, 512), full), pl.BlockSpec((512, 128), full),
                  pl.BlockSpec((ts, 128), row), pl.BlockSpec((ts, 128), row), ANY, ANY],
        out_specs=(pl.BlockSpec((ts, 512), row), pl.BlockSpec((ts, 512), row), pl.BlockSpec((ts, 512), row),
                   pl.BlockSpec((ts, D_MODEL), row), pl.BlockSpec((ts, 128), row),
                   pl.BlockSpec((D_MODEL, D_MODEL), full), pl.BlockSpec((8, D_MODEL), full),
                   pl.BlockSpec((8, 512), full), pl.BlockSpec((32, 512), full), ANY),
        scratch_shapes=[pltpu.VMEM((ts + HALO, 512), F32), pltpu.VMEM((ts + HALO, 512), F32)] + _copy_sems(2),
        compiler_params=_cparams(1),
    )(a, g, a, g, o, x, dx1, ada, wdw, prm, ln1, w_out, gm, sel, lse, fc, dw1, dw2)


def _attn_bwd(q, k, v, do, stats, fr, tk):
    s = q.shape[0]
    nk = s // tk
    tq = tk

    def body(q_hbm, do_hbm, st_hbm, k_ref, v_ref, fr_ref, dq_hbm, rs_hbm, dk_ref, dv_ref, dfc_ref,
             q_vm, do_vm, st_vm, dq_vm, rs_vm, sem):
        j = pl.program_id(0)

        @pl.when(j == 0)
        def _():
            cps = [pltpu.make_async_copy(q_hbm, q_vm, sem.at[0]), pltpu.make_async_copy(do_hbm, do_vm, sem.at[1]),
                   pltpu.make_async_copy(st_hbm, st_vm, sem.at[2])]
            for cp in cps:
                cp.start()
            dq_vm[...] = jnp.zeros_like(dq_vm)
            rs_vm[...] = jnp.zeros_like(rs_vm)
            for cp in cps:
                cp.wait()

        lane = lax.broadcasted_iota(jnp.int32, (tk, 128), 1)
        low = lane < HEAD_DIM
        s_off = lax.broadcasted_iota(jnp.int32, (tq, tk), 1)
        t_off = lax.broadcasted_iota(jnp.int32, (tq, tk), 0)
        k_start = pl.multiple_of(j * tk, tk)
        dfc_tile = jnp.zeros((tk, 128), F32)
        one = jnp.ones((tq, 128), BF16)
        zero = jnp.zeros((tq, 128), BF16)
        for p in range(N_PAIRS):
            cols = slice(128 * p, 128 * p + 128)
            kb = k_ref[:, cols]
            vb = v_ref[:, cols]
            km = (jnp.where(low, kb, jnp.zeros_like(kb)), jnp.where(low, jnp.zeros_like(kb), kb))
            vm = (jnp.where(low, vb, jnp.zeros_like(vb)), jnp.where(low, jnp.zeros_like(vb), vb))
            k_aug = (jnp.where(low, kb, jnp.where(lane == HEAD_DIM, one, zero)),
                     jnp.where(low, jnp.where(lane == 0, one, zero), kb))
            fs = (fr_ref[2 * p:2 * p + 1, pl.ds(k_start, tk)], fr_ref[2 * p + 1:2 * p + 2, pl.ds(k_start, tk)])

            def step(i, carry, diagonal):
                start = pl.multiple_of(i * tq, tq)
                qb = q_vm[pl.ds(start, tq), cols]
                dob = do_vm[pl.ds(start, tq), cols]
                st = st_vm[pl.ds(start, tq), :]
                q_aug = (jnp.where(low, qb, jnp.where(lane == HEAD_DIM, one, zero)),
                         jnp.where(low, jnp.where(lane == 0, one, zero), qb))
                out = []
                dq_h = []
                for hh in range(2):
                    h = 2 * p + hh
                    dk_acc, dv_acc = carry[2 * hh:2 * hh + 2]
                    sc = (_dot_nt(qb, km[hh]) + st[:, h:h + 1]) - fs[hh]
                    pr = jnp.exp(sc)
                    if diagonal:
                        pr = jnp.where(s_off <= t_off, pr, 0.0)
                    dp = _dot_nt(dob, vm[hh])
                    ds = (pr * (dp - st[:, N_HEADS + h:N_HEADS + h + 1])).astype(BF16)
                    dq_h.append(_dot(ds, k_aug[hh]))
                    dk_acc = dk_acc + _dot_tn(ds, q_aug[hh])
                    dv_acc = dv_acc + _dot_tn(pr.astype(BF16), dob)
                    out += [dk_acc, dv_acc]
                dq_vm[pl.ds(start, tq), cols] += jnp.where(low, dq_h[0], dq_h[1])
                rs_vm[pl.ds(start, tq), :] += (jnp.where(lane == 2 * p, dq_h[0][:, HEAD_DIM:HEAD_DIM + 1], 0.0)
                                               + jnp.where(lane == 2 * p + 1, dq_h[1][:, 0:1], 0.0))
                return tuple(out)

            init = (jnp.zeros((tk, 128), F32),) * 4
            first = step(j, init, diagonal=True)
            dk0, dv0, dk1, dv1 = lax.fori_loop(j + 1, nk, functools.partial(step, diagonal=False), first)
            dk_ref[:, cols] = jnp.where(low, dk0, dk1).astype(BF16)
            dv_ref[:, cols] = jnp.where(low, dv0, dv1).astype(BF16)
            dfc_tile = jnp.where(lane == 2 * p, dk0[:, HEAD_DIM:HEAD_DIM + 1], dfc_tile)
            dfc_tile = jnp.where(lane == 2 * p + 1, dk1[:, 0:1], dfc_tile)
        dfc_ref[...] = dfc_tile

        @pl.when(j == nk - 1)
        def _():
            cp = pltpu.make_async_copy(dq_vm, dq_hbm, sem.at[3])
            cr = pltpu.make_async_copy(rs_vm, rs_hbm, sem.at[4])
            cp.start()
            cr.start()
            cp.wait()
            cr.wait()

    row = lambda j: (j, 0)
    return pl.pallas_call(
        body, name="attn_bwd", grid=(nk,),
        out_shape=(jax.ShapeDtypeStruct((s, 512), F32), jax.ShapeDtypeStruct((s, 128), F32),
                   jax.ShapeDtypeStruct((s, 512), BF16), jax.ShapeDtypeStruct((s, 512), BF16),
                   jax.ShapeDtypeStruct((s, 128), F32)),
        in_specs=[ANY, ANY, ANY, pl.BlockSpec((tk, 512), row), pl.BlockSpec((tk, 512), row),
                  pl.BlockSpec((8, s), lambda j: (0, 0))],
        out_specs=(ANY, ANY, pl.BlockSpec((tk, 512), row), pl.BlockSpec((tk, 512), row),
                   pl.BlockSpec((tk, 128), row)),
        scratch_shapes=[pltpu.VMEM((s, 512), BF16), pltpu.VMEM((s, 512), BF16), pltpu.VMEM((s, 128), F32),
                        pltpu.VMEM((s, 512), F32), pltpu.VMEM((s, 128), F32), pltpu.SemaphoreType.DMA((5,))],
        compiler_params=_cparams(1),
    )(q, do, stats, k, v, fr)


def _inproj_bwd(x, ada, w_p, dq, dk, dv, da, dg, dfc, drs, logf, dxa, dwo, ts):
    s = x.shape[0]
    ns = s // ts

    def body(x_ref, ada_ref, w_ref, dq_ref, dk_ref, dv_ref, da_ref, dg_ref, dfc_ref, drs_ref, lf_ref, dxa_ref,
             dwo_hbm, gx_ref, dw_hbm, pgi_ref, dbf_ref, land_hbm, carry, dw_vm, sem, lsem, ssem, rsem):
        i = pl.program_id(0)

        def exchange():
            return _scatter_copies((dwo_hbm,), land_hbm, _STACK_OUT, lsem, ssem, rsem)

        @pl.when(i == 0)
        def _():
            exchange().start()
            carry[...] = jnp.zeros_like(carry)
            dw_vm[...] = jnp.zeros_like(dw_vm)
            pgi_ref[...] = jnp.zeros_like(pgi_ref)
            dbf_ref[...] = jnp.zeros_like(dbf_ref)

        r = lax.broadcasted_iota(jnp.int32, (ts, ts), 0)
        cc = lax.broadcasted_iota(jnp.int32, (ts, ts), 1)
        tri = (cc >= r).astype(BF16)
        dlogf = carry[...] + _tri_dot(tri, drs_ref[...] - dfc_ref[...])
        carry[...] = dlogf[0:1, :]
        lane = lax.broadcasted_iota(jnp.int32, (ts, 128), 1)
        dz = jnp.where(lane < N_HEADS, dlogf * (1.0 - jnp.exp(lf_ref[...])), 0.0)
        dbf_ref[0:1, :] += _rowsum(dz)
        dproj = jnp.concatenate(
            [(dq_ref[...] * (HEAD_DIM ** -0.5)).astype(BF16), dk_ref[...], dv_ref[...], da_ref[...], dg_ref[...],
             dz.astype(BF16)], axis=-1)
        xv = x_ref[...]
        sc1 = 1.0 + ada_ref[1:2, :]
        u = (xv * sc1 + ada_ref[0:1, :]).astype(BF16)
        du = _dot_nt(dproj, w_ref[...])
        dw_vm[...] += _dot_tn(u, dproj)
        gx_ref[...] = dxa_ref[...] + du * sc1
        pgi_ref[0:1, :] += _rowsum(du * xv)
        pgi_ref[1:2, :] += _rowsum(du)

        @pl.when(i == ns - 1)
        def _():
            cp = pltpu.make_async_copy(dw_vm, dw_hbm, sem.at[0])
            cp.start()
            cp.wait()
            exchange().wait()

    row = lambda i: (ns - 1 - i, 0)
    full = lambda i: (0, 0)
    return pl.pallas_call(
        body, name="inproj_bwd", grid=(ns,),
        out_shape=(jax.ShapeDtypeStruct((s, D_MODEL), F32), jax.ShapeDtypeStruct((D_MODEL, N_IN_PAD), F32),
                   jax.ShapeDtypeStruct((8, D_MODEL), F32), jax.ShapeDtypeStruct((8, 128), F32),
                   jax.ShapeDtypeStruct(dwo.shape, F32)),
        in_specs=[pl.BlockSpec((ts, D_MODEL), row), pl.BlockSpec((6, D_MODEL), full),
                  pl.BlockSpec((D_MODEL, N_IN_PAD), full)]
        + [pl.BlockSpec((ts, 512), row)] * 5 + [pl.BlockSpec((ts, 128), row)] * 3
        + [pl.BlockSpec((ts, D_MODEL), row), ANY],
        out_specs=(pl.BlockSpec((ts, D_MODEL), row), ANY, pl.BlockSpec((8, D_MODEL), full),
                   pl.BlockSpec((8, 128), full), ANY),
        scratch_shapes=[pltpu.VMEM((1, 128), F32), pltpu.VMEM((D_MODEL, N_IN_PAD), F32),
                        pltpu.SemaphoreType.DMA((1,))] + _copy_sems(1),
        compiler_params=_cparams(1),
    )(x, ada, w_p, dq, dk, dv, da, dg, dfc, drs, logf, dxa, dwo)


def _small_reduce(packed):
    def body(p_ref, sum_ref, all_ref, ssem, rsem):
        x, y, c = _position()
        me = 4 * x + 2 * y + c
        all_ref[me] = p_ref[...]
        sends = []
        for k in range(1, 8):
            peer = (x ^ ((k >> 2) & 1), y ^ ((k >> 1) & 1), c ^ (k & 1))
            cp = pltpu.make_async_remote_copy(
                src_ref=p_ref, dst_ref=all_ref.at[me], send_sem=ssem.at[k], recv_sem=rsem.at[k],
                device_id=peer, device_id_type=MESH)
            cp.start()
            sends.append(cp)
        for k in range(1, 8):
            pltpu.make_async_remote_copy(
                src_ref=p_ref, dst_ref=all_ref.at[me ^ k], send_sem=ssem.at[k], recv_sem=rsem.at[k],
                device_id=(x, y, c), device_id_type=MESH).wait_recv()
        for cp in sends:
            cp.wait_send()
        total = all_ref[0]
        for dev in range(1, 8):
            total = total + all_ref[dev]
        sum_ref[...] = total
        loss = jnp.sum(total[SMALL_ROWS - 1:SMALL_ROWS, :], axis=-1, keepdims=True)
        sum_ref[SMALL_ROWS - 1:SMALL_ROWS, :] = jnp.broadcast_to(loss, (1, D_MODEL))

    vm = pl.BlockSpec(memory_space=pltpu.VMEM)
    return pl.pallas_call(
        body, name="small_reduce",
        out_shape=(jax.ShapeDtypeStruct((SMALL_ROWS, D_MODEL), F32), jax.ShapeDtypeStruct((8, SMALL_ROWS, D_MODEL), F32)),
        in_specs=[vm], out_specs=(vm, vm),
        scratch_shapes=[pltpu.SemaphoreType.DMA((8,)), pltpu.SemaphoreType.DMA((8,))],
        compiler_params=pltpu.CompilerParams(vmem_limit_bytes=VMEM_LIMIT),
    )(packed)


def _adam_math(gv, wv, mv, vv):
    m_new = B1 * mv + (1.0 - B1) * gv
    v_new = B2 * vv + (1.0 - B2) * (gv * gv)
    m_hat = m_new / (1.0 - B1 ** STEP)
    v_hat = v_new / (1.0 - B2 ** STEP)
    delta = -LR * (m_hat / (jnp.sqrt(v_hat) + ADAM_EPS) + WD * wv)
    return delta, m_new, v_new


def _adamw(gv, wv, mv, vv, name):
    rows, cols = gv.shape
    tr = rows
    for cand in (256, 128, 64, 32, 16, 8):
        if rows % cand == 0 and rows > cand:
            tr = cand
            break

    def body(g_ref, w_ref, m_ref, v_ref, d_ref, mo_ref, vo_ref):
        d_ref[...], mo_ref[...], vo_ref[...] = _adam_math(g_ref[...], w_ref[...], m_ref[...], v_ref[...])

    spec = pl.BlockSpec((tr, cols), lambda i: (i, 0))
    return pl.pallas_call(
        body, name=name, grid=(rows // tr,),
        out_shape=(jax.ShapeDtypeStruct((rows, cols), F32),) * 3,
        in_specs=[spec] * 4, out_specs=(spec,) * 3,
        compiler_params=_cparams(1),
    )(gv, wv, mv, vv)


def _w_ada_update(sct, dd, wv, mv, vv):
    rows, cols = wv.shape
    tr = 128

    def body(s_ref, d_ref, w_ref, m_ref, v_ref, g_ref, dl_ref, mo_ref, vo_ref):
        sv = s_ref[...]
        dv = d_ref[...]
        gv = sv[:, 0:1] * dv[0:1, :]
        for b in range(1, 8):
            gv = gv + sv[:, b:b + 1] * dv[b:b + 1, :]
        g_ref[...] = gv
        dl_ref[...], mo_ref[...], vo_ref[...] = _adam_math(gv, w_ref[...], m_ref[...], v_ref[...])

    spec = pl.BlockSpec((tr, cols), lambda i: (i, 0))
    return pl.pallas_call(
        body, name="w_ada_update", grid=(rows // tr,),
        out_shape=(jax.ShapeDtypeStruct((rows, cols), F32),) * 4,
        in_specs=[pl.BlockSpec((tr, 8), lambda i: (i, 0)), pl.BlockSpec((8, cols), lambda i: (0, 0))] + [spec] * 3,
        out_specs=(spec,) * 4,
        compiler_params=_cparams(1),
    )(sct, dd, wv, mv, vv)


def _grad_exchange(part):
    rows = part.shape[1]

    def body(p_hbm, land_hbm, lsem, ssem, rsem):
        cps = _scatter_copies((p_hbm,), land_hbm, ((0, rows),), lsem, ssem, rsem)
        cps.start()
        cps.wait()

    return pl.pallas_call(
        body, name="grad_exchange",
        out_shape=jax.ShapeDtypeStruct(part.shape, F32),
        in_specs=[ANY], out_specs=ANY,
        scratch_shapes=_copy_sems(1),
    )(part)


def _sum_chips(land_in, land_out, land_ff):
    tr = 256
    n_in, n_out = 768 // tr, 256 // tr

    def body(in_ref, out_ref, ff_ref, s_ref):
        i = pl.program_id(0)

        def total(ref):
            s_ref[...] = ((ref[0] + ref[1]) + ref[2]) + ref[3]

        pl.when(i < n_in)(lambda: total(in_ref))
        pl.when((i >= n_in) & (i < n_in + n_out))(lambda: total(out_ref))
        pl.when(i >= n_in + n_out)(lambda: total(ff_ref))

    return pl.pallas_call(
        body, name="sum_chips", grid=(STACK_ROWS // tr,),
        out_shape=jax.ShapeDtypeStruct((STACK_ROWS, D_MODEL), F32),
        in_specs=[pl.BlockSpec((N_SHARD, tr, D_MODEL), lambda i: (0, jnp.minimum(i, n_in - 1), 0)),
                  pl.BlockSpec((N_SHARD, tr, D_MODEL), lambda i: (0, jnp.clip(i - n_in, 0, n_out - 1), 0)),
                  pl.BlockSpec((N_SHARD, tr, D_MODEL), lambda i: (0, jnp.maximum(i - n_in - n_out, 0), 0))],
        out_specs=pl.BlockSpec((tr, D_MODEL), lambda i: (i, 0)),
        compiler_params=_cparams(1),
    )(land_in, land_out, land_ff)


def _core_swap(part):
    def body(p_ref, o_ref, ssem, rsem):
        x, y, c = _position()
        cp = pltpu.make_async_remote_copy(src_ref=p_ref, dst_ref=o_ref, send_sem=ssem, recv_sem=rsem,
                                          device_id=(x, y, 1 - c), device_id_type=MESH)
        cp.start()
        cp.wait()

    return pl.pallas_call(
        body, name="core_swap",
        out_shape=jax.ShapeDtypeStruct(part.shape, part.dtype),
        in_specs=[ANY], out_specs=ANY,
        scratch_shapes=[pltpu.SemaphoreType.DMA, pltpu.SemaphoreType.DMA],
    )(part)


def _add_pair(mine, other):
    tr = 256

    def body(a_ref, b_ref, o_ref):
        o_ref[...] = a_ref[...] + b_ref[...]

    spec = pl.BlockSpec((tr, D_MODEL), lambda i: (i, 0))
    return pl.pallas_call(
        body, name="add_pair", grid=(STACK_ROWS // tr,),
        out_shape=jax.ShapeDtypeStruct((STACK_ROWS, D_MODEL), F32),
        in_specs=[spec, spec], out_specs=spec,
        compiler_params=_cparams(1),
    )(mine, other)


def _pad_lanes(v, width=D_MODEL):
    v = v.reshape(1, -1)
    return jnp.pad(v, ((0, 0), (0, width - v.shape[1])))


def _pack_small(b_ada, ln1_g, ln1_b, ln2_g, ln2_b, b_dw, gn_g, gn_b, g_attn, g_conv, b_forget, w_dw_full, last):
    rows = [b_ada.reshape(6, D_MODEL)] + [_pad_lanes(v) for v in
                                          (ln1_g, ln1_b, ln2_g, ln2_b, b_dw, gn_g, gn_b, g_attn, g_conv, b_forget)]
    rows.append(jnp.pad(w_dw_full.reshape(CONV_K, -1), ((0, 0), (0, D_MODEL - w_dw_full.reshape(CONV_K, -1).shape[1]))))
    rows.append(_pad_lanes(last))
    return jnp.concatenate(rows, axis=0)


def _unpack_small(p):
    return dict(b_ada=p[0:6].reshape(1, 6 * D_MODEL), ln1_g=p[6:7], ln1_b=p[7:8], ln2_g=p[8:9], ln2_b=p[9:10],
                b_dw=p[10:11, :512], gn_g=p[11:12, :512], gn_b=p[12:13, :512], g_attn_out=p[13:14, :512],
                g_conv_out=p[14:15, :512], b_forget=p[15:16, :N_HEADS])


def kernel(x, c, w_ada, b_ada, w_in, b_forget, w_dw, b_dw, gn_g, gn_b, g_attn_out, g_conv_out, w_out, ln1_g, ln1_b, w_ff1, w_ff2, ln2_g, ln2_b, loss_target, m_w_ada, m_b_ada, m_w_in, m_b_forget, m_w_dw, m_b_dw, m_gn_g, m_gn_b, m_g_attn_out, m_g_conv_out, m_w_out, m_ln1_g, m_ln1_b, m_w_ff1, m_w_ff2, m_ln2_g, m_ln2_b, v_w_ada, v_b_ada, v_w_in, v_b_forget, v_w_dw, v_b_dw, v_gn_g, v_gn_b, v_g_attn_out, v_g_conv_out, v_w_out, v_ln1_g, v_ln1_b, v_w_ff1, v_w_ff2, v_ln2_g, v_ln2_b):
    seq = x.shape[1]
    ts = min(512, seq // 2)
    tq = min(512, seq // 2)
    ts_mid = min(256, seq // 2)
    q_idx = 2 * lax.axis_index("x") + lax.axis_index("y")
    xs = x[0]
    tgt = loss_target[0]

    sc_all, ada = _ada_fwd(c, w_ada[0], b_ada)
    w_in_sh = jnp.pad(w_in[0], ((0, 0), (0, IN_SHARD_PAD - IN_SHARD))).astype(BF16)
    wdw_rows = jnp.pad(w_dw[0, :, 0, :], ((0, 1), (0, 0)))
    win_all, wdw_all = _weight_gather([w_in_sh, wdw_rows])
    w_in_full = jnp.transpose(win_all[:, :, :IN_SHARD], (1, 0, 2)).reshape(D_MODEL, N_IN)
    w_p = jnp.concatenate([w_in_full[:, 0:1536], w_in_full[:, 1544:2568], w_in_full[:, 1536:1544],
                           jnp.zeros((D_MODEL, 120), BF16)], axis=1)
    bf = _pad_lanes(b_forget, 128)
    wdw_full = lax.reduce_precision(jnp.transpose(wdw_all, (1, 0, 2)).reshape(32, 512), 8, 7)

    prm = jnp.concatenate([b_dw, gn_g, gn_b, g_attn_out, g_conv_out, jnp.zeros((3, 512), F32)], axis=0)
    ln1 = jnp.concatenate([ln1_g, ln1_b], axis=0)
    ln2 = jnp.concatenate([ln2_g, ln2_b], axis=0)
    ch = jnp.arange(512)
    gm = ((ch[:, None] // HEAD_DIM == ch[None, :] // HEAD_DIM).astype(F32) / HEAD_DIM).astype(BF16)
    sel = (ch[:, None] // HEAD_DIM + N_HEADS == jnp.arange(128)[None, :]).astype(BF16)

    q, k, v, a, g, logf, fc, fr = _inproj_fwd(xs, ada, w_p, bf, ts)
    o, lse, (wout_all, w1_all, w2_all) = _attn_fwd(
        q, k, v, fc, fr, [w_out[0].astype(BF16), w_ff1[0].astype(BF16), w_ff2[0].astype(BF16)], tq)
    w_out_full = wout_all.reshape(D_MODEL, D_MODEL)
    x1 = _mid_fwd(a, g, o, xs, ada, wdw_full, prm, ln1, w_out_full, gm, ts_mid)
    dff, dx1, pg_f = _ffn_fwd(x1, ada, w1_all, w2_all, ln2, tgt, ts)

    dw1, dw2, pg_b = [], [], jnp.zeros((8, D_MODEL), F32)
    for f in range(N_SHARD):
        dx1, dw1_f, dw2_f, pg_bf = _ffn_bwd_chunk(f, x1, ada, w1_all, w2_all, dff, dx1, ts)
        dw1.append(dw1_f)
        dw2.append(dw2_f)
        pg_b = pg_b + pg_bf
    dw1 = jnp.concatenate(dw1, axis=0)
    dw2 = jnp.concatenate(dw2, axis=0)
    do, da, dg, dxa, stats, dwo, pgm, pgc, dwdw, land_ff = _mid_bwd(
        a, g, o, xs, dx1, ada, wdw_full, prm, ln1, w_out_full, gm, sel, lse, fc, dw1, dw2, ts_mid)
    dq, drs, dk, dv, dfc = _attn_bwd(q, k, v, do, stats, fr, tq)
    gx, dwp, pgi, dbf, land_out = _inproj_bwd(xs, ada, w_p, dq, dk, dv, da, dg, dfc, drs, logf, dxa,
                                              dwo.reshape(N_SHARD, 256, D_MODEL), ts)

    d_ada = jnp.concatenate([pgi[1:2], pgi[0:1], pgm[2:3], pg_b[1:2], pg_b[0:1], pg_f[2:3]], axis=0)
    packed = _pack_small(d_ada, pgm[0:1], pgm[1:2], pg_f[0:1], pg_f[1:2], pgc[0:1], pgc[1:2], pgc[2:3], pgc[3:4],
                         pgc[4:5], dbf[0:1, :N_HEADS], dwdw[0:CONV_K], pg_f[3:4])
    small_sum, small_all = _small_reduce(packed)
    loss = small_sum[SMALL_ROWS - 1, 0]
    gsm = _unpack_small(small_sum)
    g_wdw = lax.dynamic_slice(small_sum[16:16 + CONV_K, :512], (0, q_idx * 128), (CONV_K, 128))

    zrow = jnp.zeros((CONV_K + 1, D_MODEL), F32)
    w_small = _pack_small(b_ada, ln1_g, ln1_b, ln2_g, ln2_b, b_dw, gn_g, gn_b, g_attn_out,
                          g_conv_out, b_forget, zrow[:CONV_K, :512], zrow[0])
    m_small = _pack_small(m_b_ada, m_ln1_g, m_ln1_b, m_ln2_g, m_ln2_b, m_b_dw, m_gn_g, m_gn_b, m_g_attn_out,
                          m_g_conv_out, m_b_forget, zrow[:CONV_K, :512], zrow[0])
    v_small = _pack_small(v_b_ada, v_ln1_g, v_ln1_b, v_ln2_g, v_ln2_b, v_b_dw, v_gn_g, v_gn_b, v_g_attn_out,
                          v_g_conv_out, v_b_forget, zrow[:CONV_K, :512], zrow[0])
    d_small, mn_small, vn_small = (_unpack_small(t) for t in _adamw(small_sum, w_small, m_small, v_small, "adamw_small"))
    d_wdw, mn_wdw, vn_wdw = _adamw(g_wdw, w_dw[0, :, 0, :], m_w_dw[0, :, 0, :], v_w_dw[0, :, 0, :], "adamw_wdw")

    dd = lax.dynamic_slice(small_all[:, 0:6, :].reshape(8, 6 * D_MODEL), (0, q_idx * 1536), (8, 1536))
    g_wada, d_wada, mn_wada, vn_wada = _w_ada_update(sc_all.T, dd, w_ada[0], m_w_ada[0], v_w_ada[0])

    dw_in_cols = jnp.concatenate([dwp[:, 0:1536], dwp[:, 2560:2568], dwp[:, 1536:2560]], axis=1)
    dw_in_sh = jnp.pad(jnp.transpose(dw_in_cols.reshape(D_MODEL, N_SHARD, IN_SHARD), (1, 0, 2)),
                       ((0, 0), (0, 0), (0, IN_SHARD_PAD - IN_SHARD))).reshape(N_SHARD, 768, D_MODEL)
    part = _sum_chips(_grad_exchange(dw_in_sh), land_out, land_ff)
    total = _add_pair(part, _core_swap(part))
    g_win = total[0:768].reshape(D_MODEL, IN_SHARD_PAD)[:, :IN_SHARD]
    g_wout = total[768:1024]
    g_w1 = total[1024:2048]
    g_w2 = total[2048:3072]
    d_win, mn_win, vn_win = _adamw(g_win, w_in[0], m_w_in[0], v_w_in[0], "adamw_w_in")
    d_wout, mn_wout, vn_wout = _adamw(g_wout, w_out[0], m_w_out[0], v_w_out[0], "adamw_w_out")
    d_w1, mn_w1, vn_w1 = _adamw(g_w1, w_ff1[0], m_w_ff1[0], v_w_ff1[0], "adamw_w_ff1")
    d_w2, mn_w2, vn_w2 = _adamw(g_w2, w_ff2[0], m_w_ff2[0], v_w_ff2[0], "adamw_w_ff2")

    def group(wada, sm, win, wdw, wout, w1, w2):
        return (wada[None], sm["b_ada"], win[None], sm["b_forget"], wdw[None, :, None, :], sm["b_dw"], sm["gn_g"],
                sm["gn_b"], sm["g_attn_out"], sm["g_conv_out"], wout[None], sm["ln1_g"], sm["ln1_b"], w1[None],
                w2[None], sm["ln2_g"], sm["ln2_b"])

    return ((loss, gx[None])
            + group(g_wada, gsm, g_win, g_wdw, g_wout, g_w1, g_w2)
            + group(d_wada, d_small, d_win, d_wdw, d_wout, d_w1, d_w2)
            + group(mn_wada, mn_small, mn_win, mn_wdw, mn_wout, mn_w1, mn_w2)
            + group(vn_wada, vn_small, vn_win, vn_wdw, vn_wout, vn_w1, vn_w2))
```

```python
import functools

import jax
import jax.numpy as jnp
from jax import lax
from jax.experimental import pallas as pl
from jax.experimental.pallas import tpu as pltpu

F32 = jnp.float32
BF16 = jnp.bfloat16
MESH = pl.DeviceIdType.MESH
ANY = pl.BlockSpec(memory_space=pl.ANY)

D_MODEL = 1024
HEAD_DIM = 64
ATTN_W = 512
CONV_W = 512
N_HEADS = 8
N_PAIRS = 4
CONV_K = 31
HALO = 32
D_FF = 4096
N_SHARD = 4
FF_CHUNK = D_FF // N_SHARD
N_IN = 2568
IN_SHARD = N_IN // N_SHARD
IN_SHARD_PAD = 768
N_IN_PAD = 5 * 512 + 128
LN_EPS = 1e-5
ALPHA = 2.0 ** 0.25
LR, B1, B2, ADAM_EPS, WD, STEP = 0.001, 0.9, 0.999, 1e-08, 0.01, 10
VMEM_LIMIT = 56 * 1024 * 1024
SMALL_ROWS = 48
STACK_ROWS = 768 + 256 + 1024 + 1024
_STACK_OUT = ((0, 256),)
_STACK_FF = ((0, 1024), (1024, 1024))


def _cparams(n_axes):
    return pltpu.CompilerParams(dimension_semantics=("arbitrary",) * n_axes, vmem_limit_bytes=VMEM_LIMIT)


def _dot(a, b):
    return jnp.dot(a, b, preferred_element_type=F32)


def _dot_nt(a, b):
    return lax.dot_general(a, b, (((1,), (1,)), ((), ())), preferred_element_type=F32)


def _dot_tn(a, b):
    return lax.dot_general(a, b, (((0,), (0,)), ((), ())), preferred_element_type=F32)


def _dot_f32(a, b):
    hi, mid, lo = _split3(a)
    return _dot(hi, b) + _dot(mid, b) + _dot(lo, b)


def _split3(x):
    hi = x.astype(BF16)
    r = x - hi.astype(F32)
    mid = r.astype(BF16)
    lo = (r - mid.astype(F32)).astype(BF16)
    return hi, mid, lo


def _tri_dot(tri, x):
    hi, mid, lo = _split3(x)
    return _dot(tri, hi) + _dot(tri, mid) + _dot(tri, lo)


def _rowsum(x):
    return jnp.sum(x, axis=0, keepdims=True)


def _mean_last(x):
    return jnp.mean(x, axis=-1, keepdims=True)


def _position():
    x, y, c = lax.axis_index("x"), lax.axis_index("y"), lax.axis_index("c")
    return x, y, c


def _ada_fwd(c_row, w_ada, b_ada):
    n_col = w_ada.shape[1]

    def body(c_ref, w_ref, b_ref, sc_ref, ada_ref, call_ref, part_ref, pall_ref, s1, r1, s2, r2):
        x, y, c = _position()
        me = 4 * x + 2 * y + c
        q = 2 * x + y
        call_ref[me] = jnp.broadcast_to(c_ref[...], (8, D_MODEL))

        def c_copy(k):
            peer = (x ^ ((k >> 2) & 1), y ^ ((k >> 1) & 1), c ^ (k & 1))
            return pltpu.make_async_remote_copy(
                src_ref=call_ref.at[me], dst_ref=call_ref.at[me], send_sem=s1.at[k], recv_sem=r1.at[k],
                device_id=peer, device_id_type=MESH)

        def c_recv(k):
            src = me ^ k
            return pltpu.make_async_remote_copy(
                src_ref=call_ref.at[src], dst_ref=call_ref.at[src], send_sem=s1.at[k], recv_sem=r1.at[k],
                device_id=(x, y, c), device_id_type=MESH)

        sends = [c_copy(k) for k in range(1, 8)]
        for cp in sends:
            cp.start()
        for k in range(1, 8):
            c_recv(k).wait_recv()
        for cp in sends:
            cp.wait_send()

        row = lax.broadcasted_iota(jnp.int32, (8, D_MODEL), 0)
        c_all = jnp.zeros((8, D_MODEL), F32)
        for j in range(8):
            c_all = jnp.where(row == j, call_ref[j], c_all)
        sc_all = c_all * jax.nn.sigmoid(c_all)
        sc_ref[...] = sc_all
        b_slice = b_ref[:, pl.ds(pl.multiple_of(q * n_col, 128), n_col)]
        part = _dot(sc_all.astype(BF16), w_ref[...].astype(BF16)) + b_slice
        part_ref[...] = part
        pall_ref[q] = part

        def p_copy(j):
            peer = (x ^ ((j >> 1) & 1), y ^ (j & 1), c)
            return pltpu.make_async_remote_copy(
                src_ref=part_ref, dst_ref=pall_ref.at[q], send_sem=s2.at[j], recv_sem=r2.at[j],
                device_id=peer, device_id_type=MESH)

        def p_recv(j):
            src_q = q ^ j
            return pltpu.make_async_remote_copy(
                src_ref=part_ref, dst_ref=pall_ref.at[src_q], send_sem=s2.at[j], recv_sem=r2.at[j],
                device_id=(x, y, c), device_id_type=MESH)

        sends2 = [p_copy(j) for j in range(1, 4)]
        for cp in sends2:
            cp.start()
        for j in range(1, 4):
            p_recv(j).wait_recv()
        for cp in sends2:
            cp.wait_send()
        for qq in range(N_SHARD):
            ada_ref[qq] = pall_ref[qq, pl.ds(me, 1), :]

    vm = pl.BlockSpec(memory_space=pltpu.VMEM)
    sc_all, ada = pl.pallas_call(
        body, name="ada_fwd",
        out_shape=(jax.ShapeDtypeStruct((8, D_MODEL), F32), jax.ShapeDtypeStruct((N_SHARD, 1, n_col), F32)),
        in_specs=[vm, vm, vm], out_specs=(vm, vm),
        scratch_shapes=[pltpu.VMEM((8, 8, D_MODEL), F32), pltpu.VMEM((8, n_col), F32),
                        pltpu.VMEM((N_SHARD, 8, n_col), F32),
                        pltpu.SemaphoreType.DMA((8,)), pltpu.SemaphoreType.DMA((8,)),
                        pltpu.SemaphoreType.DMA((4,)), pltpu.SemaphoreType.DMA((4,))],
        compiler_params=pltpu.CompilerParams(vmem_limit_bytes=VMEM_LIMIT),
    )(c_row, w_ada, b_ada)
    return sc_all, ada.reshape(6, D_MODEL)


class _ChipCopies:
    def __init__(self, lsem, ssem, rsem):
        self.x, self.y, self.c = _position()
        self.q = 2 * self.x + self.y
        self.lsem, self.ssem, self.rsem = lsem, ssem, rsem
        self.local, self.send, self.recv = [], [], []

    def _remote(self, a, j, src, dst, peer):
        return pltpu.make_async_remote_copy(src_ref=src, dst_ref=dst, send_sem=self.ssem.at[a, j],
                                            recv_sem=self.rsem.at[a, j], device_id=peer, device_id_type=MESH)

    def add(self, a, own_src, own_dst, src_for, dst_mine, dst_from):
        x, y, c, q = self.x, self.y, self.c, self.q
        self.local.append(pltpu.make_async_copy(own_src, own_dst, self.lsem.at[a]))
        for j in range(1, 4):
            peer = (x ^ ((j >> 1) & 1), y ^ (j & 1), c)
            self.send.append(self._remote(a, j, src_for(q ^ j), dst_mine, peer))
            self.recv.append(self._remote(a, j, own_src, dst_from(q ^ j), (x, y, c)))

    def start(self):
        for cp in self.local + self.send:
            cp.start()

    def wait(self):
        for cp in self.recv:
            cp.wait_recv()
        for cp in self.send:
            cp.wait_send()
        for cp in self.local:
            cp.wait()


def _gather_copies(ins, outs, lsem, ssem, rsem):
    cps = _ChipCopies(lsem, ssem, rsem)
    for a in range(len(ins)):
        cps.add(a, ins[a], outs[a].at[cps.q], lambda chip, a=a: ins[a], outs[a].at[cps.q],
                lambda chip, a=a: outs[a].at[chip])
    return cps


def _scatter_copies(ins, land, offs, lsem, ssem, rsem):
    cps = _ChipCopies(lsem, ssem, rsem)
    for a, (off, rows) in enumerate(offs):
        cps.add(a, ins[a].at[cps.q], land.at[cps.q, pl.ds(off, rows)], lambda chip, a=a: ins[a].at[chip],
                land.at[cps.q, pl.ds(off, rows)], lambda chip, off=off, rows=rows: land.at[chip, pl.ds(off, rows)])
    return cps


def _copy_sems(n):
    return [pltpu.SemaphoreType.DMA((n,)), pltpu.SemaphoreType.DMA((n, 4)), pltpu.SemaphoreType.DMA((n, 4))]


def _weight_gather(shards):
    n = len(shards)

    def body(*refs):
        cps = _gather_copies(refs[:n], refs[n:2 * n], *refs[2 * n:])
        cps.start()
        cps.wait()

    return pl.pallas_call(
        body, name="weight_gather",
        out_shape=tuple(jax.ShapeDtypeStruct((N_SHARD,) + s.shape, s.dtype) for s in shards),
        in_specs=[ANY] * n, out_specs=tuple([ANY] * n),
        scratch_shapes=[pltpu.SemaphoreType.DMA((n,)), pltpu.SemaphoreType.DMA((n, 4)),
                        pltpu.SemaphoreType.DMA((n, 4))],
    )(*shards)


def _aug_masks(lane, h):
    a0 = HEAD_DIM if h % 2 == 0 else 0
    own = (lane < HEAD_DIM) if h % 2 == 0 else (lane >= HEAD_DIM)
    return own, (lambda k: lane == a0 + k), (lambda k0, k1: (lane >= a0 + k0) & (lane < a0 + k1))


def _pieces(x):
    hi, mid, lo = _split3(x)
    return hi.astype(F32), mid.astype(F32), lo.astype(F32)


def _inproj_fwd(x, ada, w_p, bf, ts):
    s = x.shape[0]
    ns = s // ts

    def body(x_ref, ada_ref, w_ref, bf_ref, q_ref, k_ref, v_ref, a_ref, g_ref, lf_ref, carry):
        i = pl.program_id(0)

        @pl.when(i == 0)
        def _():
            carry[...] = jnp.zeros_like(carry)

        u = (x_ref[...] * (1.0 + ada_ref[1:2, :]) + ada_ref[0:1, :]).astype(BF16)
        proj = _dot(u, w_ref[...])
        a_ref[...] = proj[:, 1536:2048]
        g_ref[...] = proj[:, 2048:2560]
        z = proj[:, 2560:2688] + bf_ref[...]
        lane = lax.broadcasted_iota(jnp.int32, (ts, 128), 1)
        logf = jnp.minimum(z, 0.0) - jnp.log(1.0 + jnp.exp(-jnp.abs(z)))
        logf = jnp.where(lane < N_HEADS, logf, 0.0)
        lf_ref[...] = logf
        r = lax.broadcasted_iota(jnp.int32, (ts, ts), 0)
        cc = lax.broadcasted_iota(jnp.int32, (ts, ts), 1)
        tri = (cc <= r).astype(BF16)
        fc = _tri_dot(tri, logf) + carry[...]
        carry[...] = fc[ts - 1:ts, :]
        for h in range(N_HEADS):
            pc = slice(128 * (h // 2), 128 * (h // 2) + 128)
            hc = slice(128 * h, 128 * h + 128)
            own, at, span = _aug_masks(lane, h)
            hi, mid, lo = _pieces(fc[:, h:h + 1])
            qp = proj[:, pc] * (HEAD_DIM ** -0.5)
            kp = proj[:, 512:1024][:, pc]
            vp = proj[:, 1024:1536][:, pc]
            q_aug = jnp.where(own, qp, jnp.where(at(0), hi, jnp.where(at(1), mid, jnp.where(at(2), lo,
                              jnp.where(span(3, 6), 1.0, 0.0)))))
            k_aug = jnp.where(own, kp, jnp.where(span(0, 3), 1.0, jnp.where(at(3), -hi, jnp.where(at(4), -mid,
                              jnp.where(at(5), -lo, jnp.where(span(6, 9), -1.0, 0.0))))))
            v_aug = jnp.where(own, vp, jnp.where(span(0, 3), 1.0, 0.0))
            q_ref[:, hc] = q_aug.astype(BF16)
            k_ref[:, hc] = k_aug.astype(BF16)
            v_ref[:, hc] = v_aug.astype(BF16)

    row = lambda i: (i, 0)
    full = lambda i: (0, 0)
    return pl.pallas_call(
        body, name="inproj_fwd", grid=(ns,),
        out_shape=(jax.ShapeDtypeStruct((s, N_HEADS * 128), BF16),) * 3 + (jax.ShapeDtypeStruct((s, 512), F32),) * 2
        + (jax.ShapeDtypeStruct((s, 128), F32),),
        in_specs=[pl.BlockSpec((ts, D_MODEL), row), pl.BlockSpec((6, D_MODEL), full),
                  pl.BlockSpec((D_MODEL, N_IN_PAD), full), pl.BlockSpec((1, 128), full)],
        out_specs=(pl.BlockSpec((ts, N_HEADS * 128), row),) * 3 + (pl.BlockSpec((ts, 512), row),) * 2
        + (pl.BlockSpec((ts, 128), row),),
        scratch_shapes=[pltpu.VMEM((1, 128), F32)],
        compiler_params=_cparams(1),
    )(x, ada, w_p, bf)


ROW_CHUNK = 32


def _attn_fwd(qa, ka, va, shards, tq):
    s = qa.shape[0]
    nq = s // tq
    tk = tq
    n = len(shards)
    n_chunk = tq // ROW_CHUNK

    def body(q_ref, k_ref, v_ref, *rest):
        sh_in, (o_ref, qb_ref), sh_out = rest[:n], rest[n:n + 2], rest[n + 2:2 * n + 2]
        s_sc, p_sc, m_sc, al_sc, acc_sc, lsem, ssem, rsem = rest[2 * n + 2:]
        pair = pl.program_id(0)
        i = pl.program_id(1)

        @pl.when((pair == 0) & (i == 0))
        def _():
            _gather_copies(sh_in, sh_out, lsem, ssem, rsem).start()

        lane = lax.broadcasted_iota(jnp.int32, (tq, 128), 1)
        outs = []
        for hh in range(2):
            cols = slice(128 * hh, 128 * hh + 128)
            _, at, _ = _aug_masks(lane, hh)
            a0 = HEAD_DIM if hh == 0 else 0
            m_sc[...] = jnp.full((tq, 128), -jnp.inf, F32)
            acc_sc[...] = jnp.zeros((tq, 128), F32)

            def block(j, diagonal):
                start = pl.multiple_of(j * tk, tk)
                s_sc[...] = _dot_nt(q_ref[:, cols], k_ref[pl.ds(start, tk), cols])

                def chunk(r, carry):
                    rows = pl.ds(pl.multiple_of(r * ROW_CHUNK, ROW_CHUNK), ROW_CHUNK)
                    sc = s_sc[rows, :]
                    if diagonal:
                        t_in = r * ROW_CHUNK + lax.broadcasted_iota(jnp.int32, (ROW_CHUNK, tk), 0)
                        s_in = lax.broadcasted_iota(jnp.int32, (ROW_CHUNK, tk), 1)
                        sc = jnp.where(s_in <= t_in, sc, -jnp.inf)
                    m_prev = m_sc[rows, :]
                    m_new = jnp.maximum(m_prev, jnp.max(sc, axis=-1, keepdims=True))
                    p_sc[rows, :] = jnp.exp(sc - jnp.tile(m_new, (1, tk // 128))).astype(BF16)
                    al_sc[rows, :] = jnp.exp(m_prev - m_new)
                    m_sc[rows, :] = m_new
                    return carry

                lax.fori_loop(0, n_chunk, chunk, 0)
                acc_sc[...] = acc_sc[...] * al_sc[...] + _dot(p_sc[...], v_ref[pl.ds(start, tk), cols])

            def off_diagonal(j, carry):
                block(j, False)
                return carry

            lax.fori_loop(0, i, off_diagonal, 0)
            block(i, True)
            acc = acc_sc[...]
            denom = acc[:, a0:a0 + 1]
            outs.append(acc / denom)
            lse = m_sc[:, 0:1] + jnp.log(denom)
            hi, mid, lo = _split3(lse)
            qh = q_ref[:, cols]
            qb_ref[:, cols] = jnp.where(at(6), hi, jnp.where(at(7), mid, jnp.where(at(8), lo, qh)))
        o_ref[...] = jnp.where(lane < HEAD_DIM, outs[0], outs[1])

        @pl.when((pair == N_PAIRS - 1) & (i == nq - 1))
        def _():
            _gather_copies(sh_in, sh_out, lsem, ssem, rsem).wait()

    res = pl.pallas_call(
        body, name="attn_fwd", grid=(N_PAIRS, nq),
        out_shape=(jax.ShapeDtypeStruct((s, 512), F32), jax.ShapeDtypeStruct((s, N_HEADS * 128), BF16))
        + tuple(jax.ShapeDtypeStruct((N_SHARD,) + w.shape, w.dtype) for w in shards),
        in_specs=[pl.BlockSpec((tq, 256), lambda p, i: (i, p)), pl.BlockSpec((s, 256), lambda p, i: (0, p)),
                  pl.BlockSpec((s, 256), lambda p, i: (0, p))] + [ANY] * n,
        out_specs=(pl.BlockSpec((tq, 128), lambda p, i: (i, p)), pl.BlockSpec((tq, 256), lambda p, i: (i, p)))
        + (ANY,) * n,
        scratch_shapes=[pltpu.VMEM((tq, tk), F32), pltpu.VMEM((tq, tk), BF16), pltpu.VMEM((tq, 128), F32),
                        pltpu.VMEM((tq, 128), F32), pltpu.VMEM((tq, 128), F32)] + _copy_sems(n),
        compiler_params=_cparams(2),
    )(qa, ka, va, *shards)
    return res[0], res[1], res[2:]


def _conv_branch(a, g, ah, gh, first, ugx, wdw_ref, prm_ref, gm, ts):
    sg_g = jax.nn.sigmoid(g)
    ug = (a * sg_g).astype(BF16).astype(F32)
    ugh = jnp.where(first, 0.0, (ah * jax.nn.sigmoid(gh)).astype(BF16).astype(F32))
    ugx[0:HALO, :] = ugh
    ugx[HALO:HALO + ts, :] = ug
    y = jnp.zeros((ts, CONV_W), F32) + prm_ref[0:1, :]
    for kk in range(CONV_K):
        off = HALO - (CONV_K - 1) + kk
        y = y + wdw_ref[kk:kk + 1, :] * ugx[off:off + ts, :]
    mu = _dot_f32(y, gm)
    d = y - mu
    var = _dot_f32(d * d, gm)
    rs = lax.rsqrt(var + LN_EPS)
    yhat = d * rs
    yn = yhat * prm_ref[1:2, :] + prm_ref[2:3, :]
    sg = jax.nn.sigmoid(yn)
    co = yn * sg
    return sg_g, rs, yhat, yn, sg, co


def _mix_inputs(o, co, prm_ref):
    ra = lax.rsqrt(_mean_last(o * o) + LN_EPS)
    oh = o * ra
    rc = lax.rsqrt(_mean_last(co * co) + LN_EPS)
    ch = co * rc
    mi = jnp.concatenate([oh * prm_ref[3:4, :], ch * prm_ref[4:5, :]], axis=-1).astype(BF16)
    return ra, oh, rc, ch, mi


def _layernorm_stats(r):
    mu = _mean_last(r)
    d = r - mu
    rstd = lax.rsqrt(_mean_last(d * d) + LN_EPS)
    return d * rstd, rstd


def _layernorm_bwd(dout, xh, rstd, gain):
    dxh = dout * gain
    return rstd * (dxh - _mean_last(dxh) - xh * _mean_last(dxh * xh))


def _halo_index(tile, ts):
    return jnp.maximum(tile * (ts // HALO) - 1, 0)


def _mid_fwd(a, g, o, x, ada, wdw, prm, ln1, w_out, gm, ts):
    s = x.shape[0]
    ns = s // ts

    def body(a_ref, g_ref, ah_ref, gh_ref, o_ref, x_ref, ada_ref, wdw_ref, prm_ref, ln_ref, wo_ref, gm_ref,
             x1_ref, ugx):
        i = pl.program_id(0)
        co = _conv_branch(a_ref[...], g_ref[...], ah_ref[...], gh_ref[...], i == 0, ugx, wdw_ref, prm_ref,
                          gm_ref[...], ts)[-1]
        mi = _mix_inputs(o_ref[...], co, prm_ref)[-1]
        mixed = _dot(mi, wo_ref[...])
        r1 = ALPHA * x_ref[...] + (1.0 + ada_ref[2:3, :]) * mixed
        xh, _ = _layernorm_stats(r1)
        x1_ref[...] = xh * ln_ref[0:1, :] + ln_ref[1:2, :]

    row = lambda i: (i, 0)
    full = lambda i: (0, 0)
    halo = lambda i: (_halo_index(i, ts), 0)
    return pl.pallas_call(
        body, name="mid_fwd", grid=(ns,),
        out_shape=jax.ShapeDtypeStruct((s, D_MODEL), F32),
        in_specs=[pl.BlockSpec((ts, 512), row), pl.BlockSpec((ts, 512), row),
                  pl.BlockSpec((HALO, 512), halo), pl.BlockSpec((HALO, 512), halo),
                  pl.BlockSpec((ts, 512), row), pl.BlockSpec((ts, D_MODEL), row),
                  pl.BlockSpec((6, D_MODEL), full), pl.BlockSpec((32, 512), full), pl.BlockSpec((8, 512), full),
                  pl.BlockSpec((2, D_MODEL), full), pl.BlockSpec((D_MODEL, D_MODEL), full),
                  pl.BlockSpec((512, 512), full)],
        out_specs=pl.BlockSpec((ts, D_MODEL), row),
        scratch_shapes=[pltpu.VMEM((ts + HALO, 512), F32)],
        compiler_params=_cparams(1),
    )(a, g, a, g, o, x, ada, wdw, prm, ln1, w_out, gm)


def _ffn_fwd(x1, ada, w1, w2, ln2, tgt, ts):
    s = x1.shape[0]
    ns = s // ts
    nf = N_SHARD

    def body(x1_ref, ada_ref, w1_ref, w2_ref, ln_ref, t_ref, dff_ref, dx1_ref, pg_ref, ffacc, u2):
        i = pl.program_id(0)
        f = pl.program_id(1)

        @pl.when((i == 0) & (f == 0))
        def _():
            pg_ref[...] = jnp.zeros_like(pg_ref)

        @pl.when(f == 0)
        def _():
            u2[...] = (x1_ref[...] * (1.0 + ada_ref[4:5, :]) + ada_ref[3:4, :]).astype(BF16)
            ffacc[...] = jnp.zeros_like(ffacc)

        h = _dot(u2[...], w1_ref[0])
        r = jnp.maximum(h, 0.0)
        ffacc[...] += _dot((r * r).astype(BF16), w2_ref[0])

        @pl.when(f == nf - 1)
        def _():
            ff = ffacc[...]
            r2 = ALPHA * x1_ref[...] + (1.0 + ada_ref[5:6, :]) * ff
            xh, rstd = _layernorm_stats(r2)
            yv = xh * ln_ref[0:1, :] + ln_ref[1:2, :]
            err = yv - t_ref[...]
            dy = err * (1.0 / D_MODEL)
            dr2 = _layernorm_bwd(dy, xh, rstd, ln_ref[0:1, :])
            pg_ref[0:1, :] += _rowsum(dy * xh)
            pg_ref[1:2, :] += _rowsum(dy)
            pg_ref[2:3, :] += _rowsum(dr2 * ff)
            pg_ref[3:4, :] += _rowsum(err * err) * (0.5 / D_MODEL)
            dff_ref[...] = ((1.0 + ada_ref[5:6, :]) * dr2).astype(BF16)
            dx1_ref[...] = ALPHA * dr2

    row = lambda i, f: (i, 0)
    full = lambda i, f: (0, 0)
    chunk = lambda i, f: (f, 0, 0)
    return pl.pallas_call(
        body, name="ffn_fwd", grid=(ns, nf),
        out_shape=(jax.ShapeDtypeStruct((s, D_MODEL), BF16), jax.ShapeDtypeStruct((s, D_MODEL), F32),
                   jax.ShapeDtypeStruct((8, D_MODEL), F32)),
        in_specs=[pl.BlockSpec((ts, D_MODEL), row), pl.BlockSpec((6, D_MODEL), full),
                  pl.BlockSpec((1, D_MODEL, FF_CHUNK), chunk), pl.BlockSpec((1, FF_CHUNK, D_MODEL), chunk),
                  pl.BlockSpec((2, D_MODEL), full), pl.BlockSpec((ts, D_MODEL), row)],
        out_specs=(pl.BlockSpec((ts, D_MODEL), row), pl.BlockSpec((ts, D_MODEL), row),
                   pl.BlockSpec((8, D_MODEL), full)),
        scratch_shapes=[pltpu.VMEM((ts, D_MODEL), F32), pltpu.VMEM((ts, D_MODEL), BF16)],
        compiler_params=_cparams(2),
    )(x1, ada, w1, w2, ln2, tgt)


def _ffn_bwd_chunk(f, x1, ada, w1, w2, dff, dx1, ts):
    s = x1.shape[0]
    ns = s // ts

    def body(x1_ref, ada_ref, w1_ref, w2_ref, dff_ref, dx1_in, dx1_out, dw1_ref, dw2_ref, pg_ref):
        i = pl.program_id(0)

        @pl.when(i == 0)
        def _():
            pg_ref[...] = jnp.zeros_like(pg_ref)
            dw1_ref[...] = jnp.zeros_like(dw1_ref)
            dw2_ref[...] = jnp.zeros_like(dw2_ref)

        x1v = x1_ref[...]
        u2 = (x1v * (1.0 + ada_ref[4:5, :]) + ada_ref[3:4, :]).astype(BF16)
        h = _dot(u2, w1_ref[0])
        r = jnp.maximum(h, 0.0)
        hid = (r * r).astype(BF16)
        dffv = dff_ref[...]
        dh = (_dot_nt(dffv, w2_ref[0]) * (2.0 * r)).astype(BF16)
        dw2_ref[0] += _dot_tn(hid, dffv)
        dw1_ref[0] += _dot_tn(u2, dh)
        du2 = _dot_nt(dh, w1_ref[0])
        dx1_out[...] = dx1_in[...] + du2 * (1.0 + ada_ref[4:5, :])
        pg_ref[0:1, :] += _rowsum(du2 * x1v)
        pg_ref[1:2, :] += _rowsum(du2)

    row = lambda i: (i, 0)
    full = lambda i: (0, 0)
    full3 = lambda i: (0, 0, 0)
    chunk = lambda i: (f, 0, 0)
    return pl.pallas_call(
        body, name=f"ffn_bwd_{f}", grid=(ns,),
        out_shape=(jax.ShapeDtypeStruct((s, D_MODEL), F32), jax.ShapeDtypeStruct((1, D_MODEL, FF_CHUNK), F32),
                   jax.ShapeDtypeStruct((1, FF_CHUNK, D_MODEL), F32), jax.ShapeDtypeStruct((8, D_MODEL), F32)),
        in_specs=[pl.BlockSpec((ts, D_MODEL), row), pl.BlockSpec((6, D_MODEL), full),
                  pl.BlockSpec((1, D_MODEL, FF_CHUNK), chunk), pl.BlockSpec((1, FF_CHUNK, D_MODEL), chunk),
                  pl.BlockSpec((ts, D_MODEL), row), pl.BlockSpec((ts, D_MODEL), row)],
        out_specs=(pl.BlockSpec((ts, D_MODEL), row), pl.BlockSpec((1, D_MODEL, FF_CHUNK), full3),
                   pl.BlockSpec((1, FF_CHUNK, D_MODEL), full3), pl.BlockSpec((8, D_MODEL), full)),
        compiler_params=_cparams(1),
    )(x1, ada, w1, w2, dff, dx1)


def _mid_bwd(a, g, o, x, dx1, ada, wdw, prm, ln1, w_out, gm, sel, dw1, dw2, ts):
    s = x.shape[0]
    ns = s // ts

    def body(a_ref, g_ref, ah_ref, gh_ref, o_ref, x_ref, dx1_ref, ada_ref, wdw_ref, prm_ref, ln_ref, wo_ref,
             gm_ref, sel_ref, dw1_hbm, dw2_hbm,
             do_ref, da_ref, dg_ref, dxa_ref, dwo_ref, pgm_ref, pgc_ref, dwdw_ref, land_hbm,
             ugx, dyx, lsem, ssem, rsem):
        i = pl.program_id(0)
        tile = ns - 1 - i

        def exchange():
            return _scatter_copies((dw1_hbm, dw2_hbm), land_hbm, _STACK_FF, lsem, ssem, rsem)

        @pl.when(i == 0)
        def _():
            exchange().start()
            dwo_ref[...] = jnp.zeros_like(dwo_ref)
            pgm_ref[...] = jnp.zeros_like(pgm_ref)
            pgc_ref[...] = jnp.zeros_like(pgc_ref)
            dwdw_ref[...] = jnp.zeros_like(dwdw_ref)
            dyx[ts:ts + HALO, :] = jnp.zeros((HALO, 512), F32)

        gmv = gm_ref[...]
        av = a_ref[...]
        ov = o_ref[...]
        sg_g, rs, yhat, yn, sg, co = _conv_branch(av, g_ref[...], ah_ref[...], gh_ref[...], tile == 0, ugx,
                                                  wdw_ref, prm_ref, gmv, ts)
        ra, oh, rc, ch, mi = _mix_inputs(ov, co, prm_ref)
        mixed = _dot(mi, wo_ref[...])
        gt1 = 1.0 + ada_ref[2:3, :]
        r1 = ALPHA * x_ref[...] + gt1 * mixed
        xh, rstd = _layernorm_stats(r1)
        dx1 = dx1_ref[...]
        pgm_ref[0:1, :] += _rowsum(dx1 * xh)
        pgm_ref[1:2, :] += _rowsum(dx1)
        dr1 = _layernorm_bwd(dx1, xh, rstd, ln_ref[0:1, :])
        dxa_ref[...] = ALPHA * dr1
        pgm_ref[2:3, :] += _rowsum(dr1 * mixed)
        dmixed = (gt1 * dr1).astype(BF16)
        dmi = _dot_nt(dmixed, wo_ref[...])
        dwo_ref[...] += _dot_tn(mi, dmixed)
        dna = dmi[:, 0:512]
        dnc = dmi[:, 512:1024]
        pgc_ref[3:4, :] += _rowsum(dna * oh)
        doh = dna * prm_ref[3:4, :]
        do = ra * (doh - oh * _mean_last(doh * oh))
        lane = lax.broadcasted_iota(jnp.int32, (ts, 128), 1)
        delta = _dot_f32(do * ov, sel_ref[...])
        for h in range(N_HEADS):
            own, at, _ = _aug_masks(lane, h)
            hi, mid, lo = _pieces(-delta[:, h:h + 1])
            dop = do[:, 128 * (h // 2):128 * (h // 2) + 128]
            do_ref[:, 128 * h:128 * h + 128] = jnp.where(
                own, dop, jnp.where(at(0), hi, jnp.where(at(1), mid, jnp.where(at(2), lo, 0.0)))).astype(BF16)
        pgc_ref[4:5, :] += _rowsum(dnc * ch)
        dch = dnc * prm_ref[4:5, :]
        dco = rc * (dch - ch * _mean_last(dch * ch))
        dyn = dco * (sg * (1.0 + yn * (1.0 - sg)))
        pgc_ref[1:2, :] += _rowsum(dyn * yhat)
        pgc_ref[2:3, :] += _rowsum(dyn)
        dyh = dyn * prm_ref[1:2, :]
        dy = rs * (dyh - _dot_f32(dyh, gmv) - yhat * _dot_f32(dyh * yhat, gmv))
        pgc_ref[0:1, :] += _rowsum(dy)
        dyr = dy.astype(BF16).astype(F32)
        dyx[0:ts, :] = dyr
        dug = jnp.zeros((ts, CONV_W), F32)
        for kk in range(CONV_K):
            off = HALO - (CONV_K - 1) + kk
            dwdw_ref[kk:kk + 1, :] += _rowsum(dyr * ugx[off:off + ts, :])
            back = CONV_K - 1 - kk
            dug = dug + wdw_ref[kk:kk + 1, :] * dyx[back:back + ts, :]
        dyx[ts:ts + HALO, :] = dyr[0:HALO, :]
        da_ref[...] = (dug * sg_g).astype(BF16)
        dg_ref[...] = (dug * av * sg_g * (1.0 - sg_g)).astype(BF16)

        @pl.when(i == ns - 1)
        def _():
            exchange().wait()

    row = lambda i: (ns - 1 - i, 0)
    full = lambda i: (0, 0)
    halo = lambda i: (_halo_index(ns - 1 - i, ts), 0)
    return pl.pallas_call(
        body, name="mid_bwd", grid=(ns,),
        out_shape=(jax.ShapeDtypeStruct((s, N_HEADS * 128), BF16), jax.ShapeDtypeStruct((s, 512), BF16),
                   jax.ShapeDtypeStruct((s, 512), BF16), jax.ShapeDtypeStruct((s, D_MODEL), F32),
                   jax.ShapeDtypeStruct((D_MODEL, D_MODEL), F32),
                   jax.ShapeDtypeStruct((8, D_MODEL), F32), jax.ShapeDtypeStruct((8, 512), F32),
                   jax.ShapeDtypeStruct((32, 512), F32), jax.ShapeDtypeStruct((N_SHARD, 2 * FF_CHUNK, D_MODEL), F32)),
        in_specs=[pl.BlockSpec((ts, 512), row), pl.BlockSpec((ts, 512), row),
                  pl.BlockSpec((HALO, 512), halo), pl.BlockSpec((HALO, 512), halo),
                  pl.BlockSpec((ts, 512), row), pl.BlockSpec((ts, D_MODEL), row), pl.BlockSpec((ts, D_MODEL), row),
                  pl.BlockSpec((6, D_MODEL), full), pl.BlockSpec((32, 512), full), pl.BlockSpec((8, 512), full),
                  pl.BlockSpec((2, D_MODEL), full), pl.BlockSpec((D_MODEL, D_MODEL), full),
                  pl.BlockSpec((512, 512), full), pl.BlockSpec((512, 128), full), ANY, ANY],
        out_specs=(pl.BlockSpec((ts, N_HEADS * 128), row), pl.BlockSpec((ts, 512), row), pl.BlockSpec((ts, 512), row),
                   pl.BlockSpec((ts, D_MODEL), row),
                   pl.BlockSpec((D_MODEL, D_MODEL), full), pl.BlockSpec((8, D_MODEL), full),
                   pl.BlockSpec((8, 512), full), pl.BlockSpec((32, 512), full), ANY),
        scratch_shapes=[pltpu.VMEM((ts + HALO, 512), F32), pltpu.VMEM((ts + HALO, 512), F32)] + _copy_sems(2),
        compiler_params=_cparams(1),
    )(a, g, a, g, o, x, dx1, ada, wdw, prm, ln1, w_out, gm, sel, dw1, dw2)


def _attn_bwd(qb, ka, va, doa, tk):
    s = qb.shape[0]
    nk = s // tk
    tq = tk
    n_chunk = tq // ROW_CHUNK

    def body(q_ref, do_ref, k_ref, v_ref, dq_ref, rs_ref, dk_ref, dv_ref, cs_ref,
             s_sc, d_sc, p_sc, ds_sc, dk_sc, dv_sc):
        pair = pl.program_id(0)
        j = pl.program_id(1)

        @pl.when(j == 0)
        def _():
            dq_ref[...] = jnp.zeros_like(dq_ref)

        @pl.when((pair == 0) & (j == 0))
        def _():
            rs_ref[...] = jnp.zeros_like(rs_ref)
            cs_ref[...] = jnp.zeros_like(cs_ref)

        lane = lax.broadcasted_iota(jnp.int32, (tk, 128), 1)
        low = lane < HEAD_DIM
        dk_sc[...] = jnp.zeros_like(dk_sc)
        dv_sc[...] = jnp.zeros_like(dv_sc)

        def block(i, diagonal):
            rows_q = pl.ds(pl.multiple_of(i * tq, tq), tq)
            dq_h = []
            for hh in range(2):
                cols = slice(128 * hh, 128 * hh + 128)
                s_sc[...] = _dot_nt(q_ref[rows_q, cols], k_ref[:, cols])
                d_sc[...] = _dot_nt(do_ref[rows_q, cols], v_ref[:, cols])

                def chunk(r, carry):
                    rows = pl.ds(pl.multiple_of(r * ROW_CHUNK, ROW_CHUNK), ROW_CHUNK)
                    pr = jnp.exp(s_sc[rows, :])
                    if diagonal:
                        t_in = r * ROW_CHUNK + lax.broadcasted_iota(jnp.int32, (ROW_CHUNK, tk), 0)
                        s_in = lax.broadcasted_iota(jnp.int32, (ROW_CHUNK, tk), 1)
                        pr = jnp.where(s_in <= t_in, pr, 0.0)
                    p_sc[rows, :] = pr.astype(BF16)
                    ds_sc[rows, :] = (pr * d_sc[rows, :]).astype(BF16)
                    return carry

                lax.fori_loop(0, n_chunk, chunk, 0)
                dq_h.append(_dot(ds_sc[...], k_ref[:, cols]))
                dk_sc[hh] += _dot_tn(ds_sc[...], q_ref[rows_q, cols])
                dv_sc[hh] += _dot_tn(p_sc[...], do_ref[rows_q, cols])
            dq_ref[rows_q, :] += jnp.where(low, dq_h[0], dq_h[1])
            rs_ref[rows_q, :] += (jnp.where(lane == 2 * pair, dq_h[0][:, HEAD_DIM:HEAD_DIM + 1], 0.0)
                                  + jnp.where(lane == 2 * pair + 1, dq_h[1][:, 0:1], 0.0))

        def off_diagonal(i, carry):
            block(i, False)
            return carry

        block(j, True)
        lax.fori_loop(j + 1, nk, off_diagonal, 0)
        dk0, dk1 = dk_sc[0], dk_sc[1]
        dk_ref[...] = jnp.where(low, dk0, dk1).astype(BF16)
        dv_ref[...] = jnp.where(low, dv_sc[0], dv_sc[1]).astype(BF16)
        rows_k = pl.ds(pl.multiple_of(j * tk, tk), tk)
        cs_ref[rows_k, :] += (jnp.where(lane == 2 * pair, dk0[:, HEAD_DIM + 3:HEAD_DIM + 4], 0.0)
                              + jnp.where(lane == 2 * pair + 1, dk1[:, 3:4], 0.0))

    whole = lambda p, j: (0, 0)
    return pl.pallas_call(
        body, name="attn_bwd", grid=(N_PAIRS, nk),
        out_shape=(jax.ShapeDtypeStruct((s, 512), F32), jax.ShapeDtypeStruct((s, 128), F32),
                   jax.ShapeDtypeStruct((s, 512), BF16), jax.ShapeDtypeStruct((s, 512), BF16),
                   jax.ShapeDtypeStruct((s, 128), F32)),
        in_specs=[pl.BlockSpec((s, 256), lambda p, j: (0, p)), pl.BlockSpec((s, 256), lambda p, j: (0, p)),
                  pl.BlockSpec((tk, 256), lambda p, j: (j, p)), pl.BlockSpec((tk, 256), lambda p, j: (j, p))],
        out_specs=(pl.BlockSpec((s, 128), lambda p, j: (0, p)), pl.BlockSpec((s, 128), whole),
                   pl.BlockSpec((tk, 128), lambda p, j: (j, p)), pl.BlockSpec((tk, 128), lambda p, j: (j, p)),
                   pl.BlockSpec((s, 128), whole)),
        scratch_shapes=[pltpu.VMEM((tq, tk), F32), pltpu.VMEM((tq, tk), F32), pltpu.VMEM((tq, tk), BF16),
                        pltpu.VMEM((tq, tk), BF16), pltpu.VMEM((2, tk, 128), F32), pltpu.VMEM((2, tk, 128), F32)],
        compiler_params=_cparams(2),
    )(qb, doa, ka, va)


def _inproj_bwd(x, ada, w_p, dq, dk, dv, da, dg, dfc, drs, logf, dxa, dwo, ts):
    s = x.shape[0]
    ns = s // ts

    def body(x_ref, ada_ref, w_ref, dq_ref, dk_ref, dv_ref, da_ref, dg_ref, dfc_ref, drs_ref, lf_ref, dxa_ref,
             dwo_hbm, gx_ref, dw_hbm, pgi_ref, dbf_ref, land_hbm, carry, dw_vm, sem, lsem, ssem, rsem):
        i = pl.program_id(0)

        def exchange():
            return _scatter_copies((dwo_hbm,), land_hbm, _STACK_OUT, lsem, ssem, rsem)

        @pl.when(i == 0)
        def _():
            exchange().start()
            carry[...] = jnp.zeros_like(carry)
            dw_vm[...] = jnp.zeros_like(dw_vm)
            pgi_ref[...] = jnp.zeros_like(pgi_ref)
            dbf_ref[...] = jnp.zeros_like(dbf_ref)

        r = lax.broadcasted_iota(jnp.int32, (ts, ts), 0)
        cc = lax.broadcasted_iota(jnp.int32, (ts, ts), 1)
        tri = (cc >= r).astype(BF16)
        dlogf = carry[...] + _tri_dot(tri, drs_ref[...] - dfc_ref[...])
        carry[...] = dlogf[0:1, :]
        lane = lax.broadcasted_iota(jnp.int32, (ts, 128), 1)
        dz = jnp.where(lane < N_HEADS, dlogf * (1.0 - jnp.exp(lf_ref[...])), 0.0)
        dbf_ref[0:1, :] += _rowsum(dz)
        dproj = jnp.concatenate(
            [(dq_ref[...] * (HEAD_DIM ** -0.5)).astype(BF16), dk_ref[...], dv_ref[...], da_ref[...], dg_ref[...],
             dz.astype(BF16)], axis=-1)
        xv = x_ref[...]
        sc1 = 1.0 + ada_ref[1:2, :]
        u = (xv * sc1 + ada_ref[0:1, :]).astype(BF16)
        du = _dot_nt(dproj, w_ref[...])
        dw_vm[...] += _dot_tn(u, dproj)
        gx_ref[...] = dxa_ref[...] + du * sc1
        pgi_ref[0:1, :] += _rowsum(du * xv)
        pgi_ref[1:2, :] += _rowsum(du)

        @pl.when(i == ns - 1)
        def _():
            cp = pltpu.make_async_copy(dw_vm, dw_hbm, sem.at[0])
            cp.start()
            cp.wait()
            exchange().wait()

    row = lambda i: (ns - 1 - i, 0)
    full = lambda i: (0, 0)
    return pl.pallas_call(
        body, name="inproj_bwd", grid=(ns,),
        out_shape=(jax.ShapeDtypeStruct((s, D_MODEL), F32), jax.ShapeDtypeStruct((D_MODEL, N_IN_PAD), F32),
                   jax.ShapeDtypeStruct((8, D_MODEL), F32), jax.ShapeDtypeStruct((8, 128), F32),
                   jax.ShapeDtypeStruct(dwo.shape, F32)),
        in_specs=[pl.BlockSpec((ts, D_MODEL), row), pl.BlockSpec((6, D_MODEL), full),
                  pl.BlockSpec((D_MODEL, N_IN_PAD), full)]
        + [pl.BlockSpec((ts, 512), row)] * 5 + [pl.BlockSpec((ts, 128), row)] * 3
        + [pl.BlockSpec((ts, D_MODEL), row), ANY],
        out_specs=(pl.BlockSpec((ts, D_MODEL), row), ANY, pl.BlockSpec((8, D_MODEL), full),
                   pl.BlockSpec((8, 128), full), ANY),
        scratch_shapes=[pltpu.VMEM((1, 128), F32), pltpu.VMEM((D_MODEL, N_IN_PAD), F32),
                        pltpu.SemaphoreType.DMA((1,))] + _copy_sems(1),
        compiler_params=_cparams(1),
    )(x, ada, w_p, dq, dk, dv, da, dg, dfc, drs, logf, dxa, dwo)


def _small_reduce(packed):
    def body(p_ref, sum_ref, all_ref, ssem, rsem):
        x, y, c = _position()
        me = 4 * x + 2 * y + c
        all_ref[me] = p_ref[...]
        sends = []
        for k in range(1, 8):
            peer = (x ^ ((k >> 2) & 1), y ^ ((k >> 1) & 1), c ^ (k & 1))
            cp = pltpu.make_async_remote_copy(
                src_ref=p_ref, dst_ref=all_ref.at[me], send_sem=ssem.at[k], recv_sem=rsem.at[k],
                device_id=peer, device_id_type=MESH)
            cp.start()
            sends.append(cp)
        for k in range(1, 8):
            pltpu.make_async_remote_copy(
                src_ref=p_ref, dst_ref=all_ref.at[me ^ k], send_sem=ssem.at[k], recv_sem=rsem.at[k],
                device_id=(x, y, c), device_id_type=MESH).wait_recv()
        for cp in sends:
            cp.wait_send()
        total = all_ref[0]
        for dev in range(1, 8):
            total = total + all_ref[dev]
        sum_ref[...] = total
        loss = jnp.sum(total[SMALL_ROWS - 1:SMALL_ROWS, :], axis=-1, keepdims=True)
        sum_ref[SMALL_ROWS - 1:SMALL_ROWS, :] = jnp.broadcast_to(loss, (1, D_MODEL))

    vm = pl.BlockSpec(memory_space=pltpu.VMEM)
    return pl.pallas_call(
        body, name="small_reduce",
        out_shape=(jax.ShapeDtypeStruct((SMALL_ROWS, D_MODEL), F32), jax.ShapeDtypeStruct((8, SMALL_ROWS, D_MODEL), F32)),
        in_specs=[vm], out_specs=(vm, vm),
        scratch_shapes=[pltpu.SemaphoreType.DMA((8,)), pltpu.SemaphoreType.DMA((8,))],
        compiler_params=pltpu.CompilerParams(vmem_limit_bytes=VMEM_LIMIT),
    )(packed)


def _adam_math(gv, wv, mv, vv):
    m_new = B1 * mv + (1.0 - B1) * gv
    v_new = B2 * vv + (1.0 - B2) * (gv * gv)
    m_hat = m_new / (1.0 - B1 ** STEP)
    v_hat = v_new / (1.0 - B2 ** STEP)
    delta = -LR * (m_hat / (jnp.sqrt(v_hat) + ADAM_EPS) + WD * wv)
    return delta, m_new, v_new


def _adamw(gv, wv, mv, vv, name):
    rows, cols = gv.shape
    tr = rows
    for cand in (256, 128, 64, 32, 16, 8):
        if rows % cand == 0 and rows > cand:
            tr = cand
            break

    def body(g_ref, w_ref, m_ref, v_ref, d_ref, mo_ref, vo_ref):
        d_ref[...], mo_ref[...], vo_ref[...] = _adam_math(g_ref[...], w_ref[...], m_ref[...], v_ref[...])

    spec = pl.BlockSpec((tr, cols), lambda i: (i, 0))
    return pl.pallas_call(
        body, name=name, grid=(rows // tr,),
        out_shape=(jax.ShapeDtypeStruct((rows, cols), F32),) * 3,
        in_specs=[spec] * 4, out_specs=(spec,) * 3,
        compiler_params=_cparams(1),
    )(gv, wv, mv, vv)


def _w_ada_update(sct, dd, wv, mv, vv):
    rows, cols = wv.shape
    tr = 128

    def body(s_ref, d_ref, w_ref, m_ref, v_ref, g_ref, dl_ref, mo_ref, vo_ref):
        sv = s_ref[...]
        dv = d_ref[...]
        gv = sv[:, 0:1] * dv[0:1, :]
        for b in range(1, 8):
            gv = gv + sv[:, b:b + 1] * dv[b:b + 1, :]
        g_ref[...] = gv
        dl_ref[...], mo_ref[...], vo_ref[...] = _adam_math(gv, w_ref[...], m_ref[...], v_ref[...])

    spec = pl.BlockSpec((tr, cols), lambda i: (i, 0))
    return pl.pallas_call(
        body, name="w_ada_update", grid=(rows // tr,),
        out_shape=(jax.ShapeDtypeStruct((rows, cols), F32),) * 4,
        in_specs=[pl.BlockSpec((tr, 8), lambda i: (i, 0)), pl.BlockSpec((8, cols), lambda i: (0, 0))] + [spec] * 3,
        out_specs=(spec,) * 4,
        compiler_params=_cparams(1),
    )(sct, dd, wv, mv, vv)


def _grad_exchange(part):
    rows = part.shape[1]

    def body(p_hbm, land_hbm, lsem, ssem, rsem):
        cps = _scatter_copies((p_hbm,), land_hbm, ((0, rows),), lsem, ssem, rsem)
        cps.start()
        cps.wait()

    return pl.pallas_call(
        body, name="grad_exchange",
        out_shape=jax.ShapeDtypeStruct(part.shape, F32),
        in_specs=[ANY], out_specs=ANY,
        scratch_shapes=_copy_sems(1),
    )(part)


def _sum_chips(land_in, land_out, land_ff):
    tr = 256
    n_in, n_out = 768 // tr, 256 // tr

    def body(in_ref, out_ref, ff_ref, s_ref):
        i = pl.program_id(0)

        def total(ref):
            s_ref[...] = ((ref[0] + ref[1]) + ref[2]) + ref[3]

        pl.when(i < n_in)(lambda: total(in_ref))
        pl.when((i >= n_in) & (i < n_in + n_out))(lambda: total(out_ref))
        pl.when(i >= n_in + n_out)(lambda: total(ff_ref))

    return pl.pallas_call(
        body, name="sum_chips", grid=(STACK_ROWS // tr,),
        out_shape=jax.ShapeDtypeStruct((STACK_ROWS, D_MODEL), F32),
        in_specs=[pl.BlockSpec((N_SHARD, tr, D_MODEL), lambda i: (0, jnp.minimum(i, n_in - 1), 0)),
                  pl.BlockSpec((N_SHARD, tr, D_MODEL), lambda i: (0, jnp.clip(i - n_in, 0, n_out - 1), 0)),
                  pl.BlockSpec((N_SHARD, tr, D_MODEL), lambda i: (0, jnp.maximum(i - n_in - n_out, 0), 0))],
        out_specs=pl.BlockSpec((tr, D_MODEL), lambda i: (i, 0)),
        compiler_params=_cparams(1),
    )(land_in, land_out, land_ff)


def _core_swap(part):
    def body(p_ref, o_ref, ssem, rsem):
        x, y, c = _position()
        cp = pltpu.make_async_remote_copy(src_ref=p_ref, dst_ref=o_ref, send_sem=ssem, recv_sem=rsem,
                                          device_id=(x, y, 1 - c), device_id_type=MESH)
        cp.start()
        cp.wait()

    return pl.pallas_call(
        body, name="core_swap",
        out_shape=jax.ShapeDtypeStruct(part.shape, part.dtype),
        in_specs=[ANY], out_specs=ANY,
        scratch_shapes=[pltpu.SemaphoreType.DMA, pltpu.SemaphoreType.DMA],
    )(part)


def _add_pair(mine, other):
    tr = 256

    def body(a_ref, b_ref, o_ref):
        o_ref[...] = a_ref[...] + b_ref[...]

    spec = pl.BlockSpec((tr, D_MODEL), lambda i: (i, 0))
    return pl.pallas_call(
        body, name="add_pair", grid=(STACK_ROWS // tr,),
        out_shape=jax.ShapeDtypeStruct((STACK_ROWS, D_MODEL), F32),
        in_specs=[spec, spec], out_specs=spec,
        compiler_params=_cparams(1),
    )(mine, other)


def _pad_lanes(v, width=D_MODEL):
    v = v.reshape(1, -1)
    return jnp.pad(v, ((0, 0), (0, width - v.shape[1])))


def _pack_small(b_ada, ln1_g, ln1_b, ln2_g, ln2_b, b_dw, gn_g, gn_b, g_attn, g_conv, b_forget, w_dw_full, last):
    rows = [b_ada.reshape(6, D_MODEL)] + [_pad_lanes(v) for v in
                                          (ln1_g, ln1_b, ln2_g, ln2_b, b_dw, gn_g, gn_b, g_attn, g_conv, b_forget)]
    rows.append(jnp.pad(w_dw_full.reshape(CONV_K, -1), ((0, 0), (0, D_MODEL - w_dw_full.reshape(CONV_K, -1).shape[1]))))
    rows.append(_pad_lanes(last))
    return jnp.concatenate(rows, axis=0)


def _unpack_small(p):
    return dict(b_ada=p[0:6].reshape(1, 6 * D_MODEL), ln1_g=p[6:7], ln1_b=p[7:8], ln2_g=p[8:9], ln2_b=p[9:10],
                b_dw=p[10:11, :512], gn_g=p[11:12, :512], gn_b=p[12:13, :512], g_attn_out=p[13:14, :512],
                g_conv_out=p[14:15, :512], b_forget=p[15:16, :N_HEADS])


def kernel(x, c, w_ada, b_ada, w_in, b_forget, w_dw, b_dw, gn_g, gn_b, g_attn_out, g_conv_out, w_out, ln1_g, ln1_b, w_ff1, w_ff2, ln2_g, ln2_b, loss_target, m_w_ada, m_b_ada, m_w_in, m_b_forget, m_w_dw, m_b_dw, m_gn_g, m_gn_b, m_g_attn_out, m_g_conv_out, m_w_out, m_ln1_g, m_ln1_b, m_w_ff1, m_w_ff2, m_ln2_g, m_ln2_b, v_w_ada, v_b_ada, v_w_in, v_b_forget, v_w_dw, v_b_dw, v_gn_g, v_gn_b, v_g_attn_out, v_g_conv_out, v_w_out, v_ln1_g, v_ln1_b, v_w_ff1, v_w_ff2, v_ln2_g, v_ln2_b):
    seq = x.shape[1]
    ts = min(512, seq // 2)
    tq = min(512, seq // 2)
    ts_mid = min(256, seq // 2)
    q_idx = 2 * lax.axis_index("x") + lax.axis_index("y")
    xs = x[0]
    tgt = loss_target[0]

    sc_all, ada = _ada_fwd(c, w_ada[0], b_ada)
    w_in_sh = jnp.pad(w_in[0], ((0, 0), (0, IN_SHARD_PAD - IN_SHARD))).astype(BF16)
    wdw_rows = jnp.pad(w_dw[0, :, 0, :], ((0, 1), (0, 0)))
    win_all, wdw_all = _weight_gather([w_in_sh, wdw_rows])
    w_in_full = jnp.transpose(win_all[:, :, :IN_SHARD], (1, 0, 2)).reshape(D_MODEL, N_IN)
    w_p = jnp.concatenate([w_in_full[:, 0:1536], w_in_full[:, 1544:2568], w_in_full[:, 1536:1544],
                           jnp.zeros((D_MODEL, 120), BF16)], axis=1)
    bf = _pad_lanes(b_forget, 128)
    wdw_full = lax.reduce_precision(jnp.transpose(wdw_all, (1, 0, 2)).reshape(32, 512), 8, 7)

    prm = jnp.concatenate([b_dw, gn_g, gn_b, g_attn_out, g_conv_out, jnp.zeros((3, 512), F32)], axis=0)
    ln1 = jnp.concatenate([ln1_g, ln1_b], axis=0)
    ln2 = jnp.concatenate([ln2_g, ln2_b], axis=0)
    ch = jnp.arange(512)
    gm = ((ch[:, None] // HEAD_DIM == ch[None, :] // HEAD_DIM).astype(F32) / HEAD_DIM).astype(BF16)
    sel = (ch[:, None] // HEAD_DIM == jnp.arange(128)[None, :]).astype(BF16)

    qa, ka, va, a, g, logf = _inproj_fwd(xs, ada, w_p, bf, ts)
    o, qb, (wout_all, w1_all, w2_all) = _attn_fwd(
        qa, ka, va, [w_out[0].astype(BF16), w_ff1[0].astype(BF16), w_ff2[0].astype(BF16)], tq)
    w_out_full = wout_all.reshape(D_MODEL, D_MODEL)
    x1 = _mid_fwd(a, g, o, xs, ada, wdw_full, prm, ln1, w_out_full, gm, ts_mid)
    dff, dx1, pg_f = _ffn_fwd(x1, ada, w1_all, w2_all, ln2, tgt, ts)

    dw1, dw2, pg_b = [], [], jnp.zeros((8, D_MODEL), F32)
    for f in range(N_SHARD):
        dx1, dw1_f, dw2_f, pg_bf = _ffn_bwd_chunk(f, x1, ada, w1_all, w2_all, dff, dx1, ts)
        dw1.append(dw1_f)
        dw2.append(dw2_f)
        pg_b = pg_b + pg_bf
    dw1 = jnp.concatenate(dw1, axis=0)
    dw2 = jnp.concatenate(dw2, axis=0)
    doa, da, dg, dxa, dwo, pgm, pgc, dwdw, land_ff = _mid_bwd(
        a, g, o, xs, dx1, ada, wdw_full, prm, ln1, w_out_full, gm, sel, dw1, dw2, ts_mid)
    dq, drs, dk, dv, dfc = _attn_bwd(qb, ka, va, doa, tq)
    gx, dwp, pgi, dbf, land_out = _inproj_bwd(xs, ada, w_p, dq, dk, dv, da, dg, dfc, drs, logf, dxa,
                                              dwo.reshape(N_SHARD, 256, D_MODEL), ts)

    d_ada = jnp.concatenate([pgi[1:2], pgi[0:1], pgm[2:3], pg_b[1:2], pg_b[0:1], pg_f[2:3]], axis=0)
    packed = _pack_small(d_ada, pgm[0:1], pgm[1:2], pg_f[0:1], pg_f[1:2], pgc[0:1], pgc[1:2], pgc[2:3], pgc[3:4],
                         pgc[4:5], dbf[0:1, :N_HEADS], dwdw[0:CONV_K], pg_f[3:4])
    small_sum, small_all = _small_reduce(packed)
    loss = small_sum[SMALL_ROWS - 1, 0]
    gsm = _unpack_small(small_sum)
    g_wdw = lax.dynamic_slice(small_sum[16:16 + CONV_K, :512], (0, q_idx * 128), (CONV_K, 128))

    zrow = jnp.zeros((CONV_K + 1, D_MODEL), F32)
    w_small = _pack_small(b_ada, ln1_g, ln1_b, ln2_g, ln2_b, b_dw, gn_g, gn_b, g_attn_out,
                          g_conv_out, b_forget, zrow[:CONV_K, :512], zrow[0])
    m_small = _pack_small(m_b_ada, m_ln1_g, m_ln1_b, m_ln2_g, m_ln2_b, m_b_dw, m_gn_g, m_gn_b, m_g_attn_out,
                          m_g_conv_out, m_b_forget, zrow[:CONV_K, :512], zrow[0])
    v_small = _pack_small(v_b_ada, v_ln1_g, v_ln1_b, v_ln2_g, v_ln2_b, v_b_dw, v_gn_g, v_gn_b, v_g_attn_out,
                          v_g_conv_out, v_b_forget, zrow[:CONV_K, :512], zrow[0])
    d_small, mn_small, vn_small = (_unpack_small(t) for t in _adamw(small_sum, w_small, m_small, v_small, "adamw_small"))
    d_wdw, mn_wdw, vn_wdw = _adamw(g_wdw, w_dw[0, :, 0, :], m_w_dw[0, :, 0, :], v_w_dw[0, :, 0, :], "adamw_wdw")

    dd = lax.dynamic_slice(small_all[:, 0:6, :].reshape(8, 6 * D_MODEL), (0, q_idx * 1536), (8, 1536))
    g_wada, d_wada, mn_wada, vn_wada = _w_ada_update(sc_all.T, dd, w_ada[0], m_w_ada[0], v_w_ada[0])

    dw_in_cols = jnp.concatenate([dwp[:, 0:1536], dwp[:, 2560:2568], dwp[:, 1536:2560]], axis=1)
    dw_in_sh = jnp.pad(jnp.transpose(dw_in_cols.reshape(D_MODEL, N_SHARD, IN_SHARD), (1, 0, 2)),
                       ((0, 0), (0, 0), (0, IN_SHARD_PAD - IN_SHARD))).reshape(N_SHARD, 768, D_MODEL)
    part = _sum_chips(_grad_exchange(dw_in_sh), land_out, land_ff)
    total = _add_pair(part, _core_swap(part))
    g_win = total[0:768].reshape(D_MODEL, IN_SHARD_PAD)[:, :IN_SHARD]
    g_wout = total[768:1024]
    g_w1 = total[1024:2048]
    g_w2 = total[2048:3072]
    d_win, mn_win, vn_win = _adamw(g_win, w_in[0], m_w_in[0], v_w_in[0], "adamw_w_in")
    d_wout, mn_wout, vn_wout = _adamw(g_wout, w_out[0], m_w_out[0], v_w_out[0], "adamw_w_out")
    d_w1, mn_w1, vn_w1 = _adamw(g_w1, w_ff1[0], m_w_ff1[0], v_w_ff1[0], "adamw_w_ff1")
    d_w2, mn_w2, vn_w2 = _adamw(g_w2, w_ff2[0], m_w_ff2[0], v_w_ff2[0], "adamw_w_ff2")

    def group(wada, sm, win, wdw, wout, w1, w2):
        return (wada[None], sm["b_ada"], win[None], sm["b_forget"], wdw[None, :, None, :], sm["b_dw"], sm["gn_g"],
                sm["gn_b"], sm["g_attn_out"], sm["g_conv_out"], wout[None], sm["ln1_g"], sm["ln1_b"], w1[None],
                w2[None], sm["ln2_g"], sm["ln2_b"])

    return ((loss, gx[None])
            + group(g_wada, gsm, g_win, g_wdw, g_wout, g_w1, g_w2)
            + group(d_wada, d_small, d_win, d_wdw, d_wout, d_w1, d_w2)
            + group(mn_wada, mn_small, mn_win, mn_wdw, mn_wout, mn_w1, mn_w2)
            + group(vn_wada, vn_small, vn_win, vn_wdw, vn_wout, vn_w1, vn_w2))
```

```python
import functools

import jax
import jax.numpy as jnp
from jax import lax
from jax.experimental import pallas as pl
from jax.experimental.pallas import tpu as pltpu

F32 = jnp.float32
BF16 = jnp.bfloat16
MESH = pl.DeviceIdType.MESH
ANY = pl.BlockSpec(memory_space=pl.ANY)

D_MODEL = 1024
HEAD_DIM = 64
ATTN_W = 512
CONV_W = 512
N_HEADS = 8
N_PAIRS = 4
CONV_K = 31
HALO = 32
D_FF = 4096
N_SHARD = 4
FF_CHUNK = D_FF // N_SHARD
N_IN = 2568
IN_SHARD = N_IN // N_SHARD
IN_SHARD_PAD = 768
N_IN_PAD = 5 * 512 + 128
LN_EPS = 1e-5
ALPHA = 2.0 ** 0.25
LR, B1, B2, ADAM_EPS, WD, STEP = 0.001, 0.9, 0.999, 1e-08, 0.01, 10
VMEM_LIMIT = 56 * 1024 * 1024
SMALL_ROWS = 48
STACK_ROWS = 768 + 256 + 1024 + 1024
_STACK_OUT = ((0, 256),)
_STACK_FF = ((0, 1024), (1024, 1024))


def _cparams(n_axes):
    return pltpu.CompilerParams(dimension_semantics=("arbitrary",) * n_axes, vmem_limit_bytes=VMEM_LIMIT)


def _dot(a, b):
    return jnp.dot(a, b, preferred_element_type=F32)


def _dot_nt(a, b):
    return lax.dot_general(a, b, (((1,), (1,)), ((), ())), preferred_element_type=F32)


def _dot_tn(a, b):
    return lax.dot_general(a, b, (((0,), (0,)), ((), ())), preferred_element_type=F32)


def _dot_f32(a, b):
    hi, mid, lo = _split3(a)
    return _dot(hi, b) + _dot(mid, b) + _dot(lo, b)


def _split3(x):
    hi = x.astype(BF16)
    r = x - hi.astype(F32)
    mid = r.astype(BF16)
    lo = (r - mid.astype(F32)).astype(BF16)
    return hi, mid, lo


def _tri_dot(tri, x):
    hi, mid, lo = _split3(x)
    return _dot(tri, hi) + _dot(tri, mid) + _dot(tri, lo)


def _rowsum(x):
    return jnp.sum(x, axis=0, keepdims=True)


def _mean_last(x):
    return jnp.mean(x, axis=-1, keepdims=True)


def _position():
    x, y, c = lax.axis_index("x"), lax.axis_index("y"), lax.axis_index("c")
    return x, y, c


def _ada_fwd(c_row, w_ada, b_ada):
    n_col = w_ada.shape[1]

    def body(c_ref, w_ref, b_ref, sc_ref, ada_ref, call_ref, part_ref, pall_ref, s1, r1, s2, r2):
        x, y, c = _position()
        me = 4 * x + 2 * y + c
        q = 2 * x + y
        call_ref[me] = jnp.broadcast_to(c_ref[...], (8, D_MODEL))

        def c_copy(k):
            peer = (x ^ ((k >> 2) & 1), y ^ ((k >> 1) & 1), c ^ (k & 1))
            return pltpu.make_async_remote_copy(
                src_ref=call_ref.at[me], dst_ref=call_ref.at[me], send_sem=s1.at[k], recv_sem=r1.at[k],
                device_id=peer, device_id_type=MESH)

        def c_recv(k):
            src = me ^ k
            return pltpu.make_async_remote_copy(
                src_ref=call_ref.at[src], dst_ref=call_ref.at[src], send_sem=s1.at[k], recv_sem=r1.at[k],
                device_id=(x, y, c), device_id_type=MESH)

        sends = [c_copy(k) for k in range(1, 8)]
        for cp in sends:
            cp.start()
        for k in range(1, 8):
            c_recv(k).wait_recv()
        for cp in sends:
            cp.wait_send()

        row = lax.broadcasted_iota(jnp.int32, (8, D_MODEL), 0)
        c_all = jnp.zeros((8, D_MODEL), F32)
        for j in range(8):
            c_all = jnp.where(row == j, call_ref[j], c_all)
        sc_all = c_all * jax.nn.sigmoid(c_all)
        sc_ref[...] = sc_all
        b_slice = b_ref[:, pl.ds(pl.multiple_of(q * n_col, 128), n_col)]
        part = _dot(sc_all.astype(BF16), w_ref[...].astype(BF16)) + b_slice
        part_ref[...] = part
        pall_ref[q] = part

        def p_copy(j):
            peer = (x ^ ((j >> 1) & 1), y ^ (j & 1), c)
            return pltpu.make_async_remote_copy(
                src_ref=part_ref, dst_ref=pall_ref.at[q], send_sem=s2.at[j], recv_sem=r2.at[j],
                device_id=peer, device_id_type=MESH)

        def p_recv(j):
            src_q = q ^ j
            return pltpu.make_async_remote_copy(
                src_ref=part_ref, dst_ref=pall_ref.at[src_q], send_sem=s2.at[j], recv_sem=r2.at[j],
                device_id=(x, y, c), device_id_type=MESH)

        sends2 = [p_copy(j) for j in range(1, 4)]
        for cp in sends2:
            cp.start()
        for j in range(1, 4):
            p_recv(j).wait_recv()
        for cp in sends2:
            cp.wait_send()
        for qq in range(N_SHARD):
            ada_ref[qq] = pall_ref[qq, pl.ds(me, 1), :]

    vm = pl.BlockSpec(memory_space=pltpu.VMEM)
    sc_all, ada = pl.pallas_call(
        body, name="ada_fwd",
        out_shape=(jax.ShapeDtypeStruct((8, D_MODEL), F32), jax.ShapeDtypeStruct((N_SHARD, 1, n_col), F32)),
        in_specs=[vm, vm, vm], out_specs=(vm, vm),
        scratch_shapes=[pltpu.VMEM((8, 8, D_MODEL), F32), pltpu.VMEM((8, n_col), F32),
                        pltpu.VMEM((N_SHARD, 8, n_col), F32),
                        pltpu.SemaphoreType.DMA((8,)), pltpu.SemaphoreType.DMA((8,)),
                        pltpu.SemaphoreType.DMA((4,)), pltpu.SemaphoreType.DMA((4,))],
        compiler_params=pltpu.CompilerParams(vmem_limit_bytes=VMEM_LIMIT),
    )(c_row, w_ada, b_ada)
    return sc_all, ada.reshape(6, D_MODEL)


class _ChipCopies:
    def __init__(self, lsem, ssem, rsem):
        self.x, self.y, self.c = _position()
        self.q = 2 * self.x + self.y
        self.lsem, self.ssem, self.rsem = lsem, ssem, rsem
        self.local, self.send, self.recv = [], [], []

    def _remote(self, a, j, src, dst, peer):
        return pltpu.make_async_remote_copy(src_ref=src, dst_ref=dst, send_sem=self.ssem.at[a, j],
                                            recv_sem=self.rsem.at[a, j], device_id=peer, device_id_type=MESH)

    def add(self, a, own_src, own_dst, src_for, dst_mine, dst_from):
        x, y, c, q = self.x, self.y, self.c, self.q
        self.local.append(pltpu.make_async_copy(own_src, own_dst, self.lsem.at[a]))
        for j in range(1, 4):
            peer = (x ^ ((j >> 1) & 1), y ^ (j & 1), c)
            self.send.append(self._remote(a, j, src_for(q ^ j), dst_mine, peer))
            self.recv.append(self._remote(a, j, own_src, dst_from(q ^ j), (x, y, c)))

    def start(self):
        for cp in self.local + self.send:
            cp.start()

    def wait(self):
        for cp in self.recv:
            cp.wait_recv()
        for cp in self.send:
            cp.wait_send()
        for cp in self.local:
            cp.wait()


def _gather_copies(ins, outs, lsem, ssem, rsem):
    cps = _ChipCopies(lsem, ssem, rsem)
    for a in range(len(ins)):
        cps.add(a, ins[a], outs[a].at[cps.q], lambda chip, a=a: ins[a], outs[a].at[cps.q],
                lambda chip, a=a: outs[a].at[chip])
    return cps


def _scatter_copies(ins, land, offs, lsem, ssem, rsem):
    cps = _ChipCopies(lsem, ssem, rsem)
    for a, (off, rows) in enumerate(offs):
        cps.add(a, ins[a].at[cps.q], land.at[cps.q, pl.ds(off, rows)], lambda chip, a=a: ins[a].at[chip],
                land.at[cps.q, pl.ds(off, rows)], lambda chip, off=off, rows=rows: land.at[chip, pl.ds(off, rows)])
    return cps


def _copy_sems(n):
    return [pltpu.SemaphoreType.DMA((n,)), pltpu.SemaphoreType.DMA((n, 4)), pltpu.SemaphoreType.DMA((n, 4))]


def _weight_gather(shards):
    n = len(shards)

    def body(*refs):
        cps = _gather_copies(refs[:n], refs[n:2 * n], *refs[2 * n:])
        cps.start()
        cps.wait()

    return pl.pallas_call(
        body, name="weight_gather",
        out_shape=tuple(jax.ShapeDtypeStruct((N_SHARD,) + s.shape, s.dtype) for s in shards),
        in_specs=[ANY] * n, out_specs=tuple([ANY] * n),
        scratch_shapes=[pltpu.SemaphoreType.DMA((n,)), pltpu.SemaphoreType.DMA((n, 4)),
                        pltpu.SemaphoreType.DMA((n, 4))],
    )(*shards)


def _aug_masks(lane, h):
    a0 = HEAD_DIM if h % 2 == 0 else 0
    own = (lane < HEAD_DIM) if h % 2 == 0 else (lane >= HEAD_DIM)
    return own, (lambda k: lane == a0 + k), (lambda k0, k1: (lane >= a0 + k0) & (lane < a0 + k1))


def _pieces(x):
    hi, mid, lo = _split3(x)
    return hi.astype(F32), mid.astype(F32), lo.astype(F32)


def _inproj_fwd(x, ada, w_p, bf, ts):
    s = x.shape[0]
    ns = s // ts

    def body(x_ref, ada_ref, w_ref, bf_ref, q_ref, k_ref, v_ref, a_ref, g_ref, lf_ref, carry):
        i = pl.program_id(0)

        @pl.when(i == 0)
        def _():
            carry[...] = jnp.zeros_like(carry)

        u = (x_ref[...] * (1.0 + ada_ref[1:2, :]) + ada_ref[0:1, :]).astype(BF16)
        proj = _dot(u, w_ref[...])
        a_ref[...] = proj[:, 1536:2048]
        g_ref[...] = proj[:, 2048:2560]
        z = proj[:, 2560:2688] + bf_ref[...]
        lane = lax.broadcasted_iota(jnp.int32, (ts, 128), 1)
        logf = jnp.minimum(z, 0.0) - jnp.log(1.0 + jnp.exp(-jnp.abs(z)))
        logf = jnp.where(lane < N_HEADS, logf, 0.0)
        lf_ref[...] = logf
        r = lax.broadcasted_iota(jnp.int32, (ts, ts), 0)
        cc = lax.broadcasted_iota(jnp.int32, (ts, ts), 1)
        tri = (cc <= r).astype(BF16)
        fc = _tri_dot(tri, logf) + carry[...]
        carry[...] = fc[ts - 1:ts, :]
        for h in range(N_HEADS):
            pc = slice(128 * (h // 2), 128 * (h // 2) + 128)
            hc = slice(128 * h, 128 * h + 128)
            own, at, span = _aug_masks(lane, h)
            hi, mid, lo = _pieces(fc[:, h:h + 1])
            qp = proj[:, pc] * (HEAD_DIM ** -0.5)
            kp = proj[:, 512:1024][:, pc]
            vp = proj[:, 1024:1536][:, pc]
            q_aug = jnp.where(own, qp, jnp.where(at(0), hi, jnp.where(at(1), mid, jnp.where(at(2), lo,
                              jnp.where(span(3, 6), 1.0, 0.0)))))
            k_aug = jnp.where(own, kp, jnp.where(span(0, 3), 1.0, jnp.where(at(3), -hi, jnp.where(at(4), -mid,
                              jnp.where(at(5), -lo, jnp.where(span(6, 9), -1.0, 0.0))))))
            v_aug = jnp.where(own, vp, jnp.where(span(0, 3), 1.0, 0.0))
            q_ref[:, hc] = q_aug.astype(BF16)
            k_ref[:, hc] = k_aug.astype(BF16)
            v_ref[:, hc] = v_aug.astype(BF16)

    row = lambda i: (i, 0)
    full = lambda i: (0, 0)
    return pl.pallas_call(
        body, name="inproj_fwd", grid=(ns,),
        out_shape=(jax.ShapeDtypeStruct((s, N_HEADS * 128), BF16),) * 3 + (jax.ShapeDtypeStruct((s, 512), F32),) * 2
        + (jax.ShapeDtypeStruct((s, 128), F32),),
        in_specs=[pl.BlockSpec((ts, D_MODEL), row), pl.BlockSpec((6, D_MODEL), full),
                  pl.BlockSpec((D_MODEL, N_IN_PAD), full), pl.BlockSpec((1, 128), full)],
        out_specs=(pl.BlockSpec((ts, N_HEADS * 128), row),) * 3 + (pl.BlockSpec((ts, 512), row),) * 2
        + (pl.BlockSpec((ts, 128), row),),
        scratch_shapes=[pltpu.VMEM((1, 128), F32)],
        compiler_params=_cparams(1),
    )(x, ada, w_p, bf)


def _attn_fwd(qa, ka, va, shards, tq):
    s = qa.shape[0]
    nq = s // tq
    tk = tq
    n = len(shards)

    def body(q_ref, k_ref, v_ref, *rest):
        sh_in, (o_ref, qb_ref), sh_out = rest[:n], rest[n:n + 2], rest[n + 2:2 * n + 2]
        lsem, ssem, rsem = rest[2 * n + 2:]
        pair = pl.program_id(0)
        i = pl.program_id(1)

        @pl.when((pair == 0) & (i == 0))
        def _():
            _gather_copies(sh_in, sh_out, lsem, ssem, rsem).start()

        lane = lax.broadcasted_iota(jnp.int32, (tq, 128), 1)
        t_off = lax.broadcasted_iota(jnp.int32, (tq, tk), 0)
        s_off = lax.broadcasted_iota(jnp.int32, (tq, tk), 1)

        def block(j, carry, diagonal):
            start = pl.multiple_of(j * tk, tk)
            out = []
            for hh in range(2):
                cols = slice(128 * hh, 128 * hh + 128)
                m, acc = carry[2 * hh:2 * hh + 2]
                sc = _dot_nt(q_ref[:, cols], k_ref[pl.ds(start, tk), cols])
                if diagonal:
                    sc = jnp.where(s_off <= t_off, sc, -jnp.inf)
                m_new = jnp.maximum(m, jnp.max(sc, axis=-1, keepdims=True))
                pr = jnp.exp(sc - m_new).astype(BF16)
                acc = acc * jnp.exp(m - m_new) + _dot(pr, v_ref[pl.ds(start, tk), cols])
                out += [m_new, acc]
            return tuple(out)

        init = (jnp.full((tq, 1), -jnp.inf, F32), jnp.zeros((tq, 128), F32)) * 2
        res = lax.fori_loop(0, i, functools.partial(block, diagonal=False), init)
        res = block(i, res, diagonal=True)
        outs = []
        for hh in range(2):
            cols = slice(128 * hh, 128 * hh + 128)
            _, at, _ = _aug_masks(lane, hh)
            a0 = HEAD_DIM if hh == 0 else 0
            m, acc = res[2 * hh:2 * hh + 2]
            denom = acc[:, a0:a0 + 1]
            outs.append(acc / denom)
            hi, mid, lo = _split3(m + jnp.log(denom))
            qb_ref[:, cols] = jnp.where(at(6), hi, jnp.where(at(7), mid, jnp.where(at(8), lo, q_ref[:, cols])))
        o_ref[...] = jnp.where(lane < HEAD_DIM, outs[0], outs[1])

        @pl.when((pair == N_PAIRS - 1) & (i == nq - 1))
        def _():
            _gather_copies(sh_in, sh_out, lsem, ssem, rsem).wait()

    res = pl.pallas_call(
        body, name="attn_fwd", grid=(N_PAIRS, nq),
        out_shape=(jax.ShapeDtypeStruct((s, 512), F32), jax.ShapeDtypeStruct((s, N_HEADS * 128), BF16))
        + tuple(jax.ShapeDtypeStruct((N_SHARD,) + w.shape, w.dtype) for w in shards),
        in_specs=[pl.BlockSpec((tq, 256), lambda p, i: (i, p)), pl.BlockSpec((s, 256), lambda p, i: (0, p)),
                  pl.BlockSpec((s, 256), lambda p, i: (0, p))] + [ANY] * n,
        out_specs=(pl.BlockSpec((tq, 128), lambda p, i: (i, p)), pl.BlockSpec((tq, 256), lambda p, i: (i, p)))
        + (ANY,) * n,
        scratch_shapes=_copy_sems(n),
        compiler_params=_cparams(2),
    )(qa, ka, va, *shards)
    return res[0], res[1], res[2:]


def _conv_branch(a, g, ah, gh, first, ugx, wdw_ref, prm_ref, gm, ts):
    sg_g = jax.nn.sigmoid(g)
    ug = (a * sg_g).astype(BF16).astype(F32)
    ugh = jnp.where(first, 0.0, (ah * jax.nn.sigmoid(gh)).astype(BF16).astype(F32))
    ugx[0:HALO, :] = ugh
    ugx[HALO:HALO + ts, :] = ug
    y = jnp.zeros((ts, CONV_W), F32) + prm_ref[0:1, :]
    for kk in range(CONV_K):
        off = HALO - (CONV_K - 1) + kk
        y = y + wdw_ref[kk:kk + 1, :] * ugx[off:off + ts, :]
    mu = _dot_f32(y, gm)
    d = y - mu
    var = _dot_f32(d * d, gm)
    rs = lax.rsqrt(var + LN_EPS)
    yhat = d * rs
    yn = yhat * prm_ref[1:2, :] + prm_ref[2:3, :]
    sg = jax.nn.sigmoid(yn)
    co = yn * sg
    return sg_g, rs, yhat, yn, sg, co


def _mix_inputs(o, co, prm_ref):
    ra = lax.rsqrt(_mean_last(o * o) + LN_EPS)
    oh = o * ra
    rc = lax.rsqrt(_mean_last(co * co) + LN_EPS)
    ch = co * rc
    mi = jnp.concatenate([oh * prm_ref[3:4, :], ch * prm_ref[4:5, :]], axis=-1).astype(BF16)
    return ra, oh, rc, ch, mi


def _layernorm_stats(r):
    mu = _mean_last(r)
    d = r - mu
    rstd = lax.rsqrt(_mean_last(d * d) + LN_EPS)
    return d * rstd, rstd


def _layernorm_bwd(dout, xh, rstd, gain):
    dxh = dout * gain
    return rstd * (dxh - _mean_last(dxh) - xh * _mean_last(dxh * xh))


def _halo_index(tile, ts):
    return jnp.maximum(tile * (ts // HALO) - 1, 0)


def _mid_fwd(a, g, o, x, ada, wdw, prm, ln1, w_out, gm, ts):
    s = x.shape[0]
    ns = s // ts

    def body(a_ref, g_ref, ah_ref, gh_ref, o_ref, x_ref, ada_ref, wdw_ref, prm_ref, ln_ref, wo_ref, gm_ref,
             x1_ref, ugx):
        i = pl.program_id(0)
        co = _conv_branch(a_ref[...], g_ref[...], ah_ref[...], gh_ref[...], i == 0, ugx, wdw_ref, prm_ref,
                          gm_ref[...], ts)[-1]
        mi = _mix_inputs(o_ref[...], co, prm_ref)[-1]
        mixed = _dot(mi, wo_ref[...])
        r1 = ALPHA * x_ref[...] + (1.0 + ada_ref[2:3, :]) * mixed
        xh, _ = _layernorm_stats(r1)
        x1_ref[...] = xh * ln_ref[0:1, :] + ln_ref[1:2, :]

    row = lambda i: (i, 0)
    full = lambda i: (0, 0)
    halo = lambda i: (_halo_index(i, ts), 0)
    return pl.pallas_call(
        body, name="mid_fwd", grid=(ns,),
        out_shape=jax.ShapeDtypeStruct((s, D_MODEL), F32),
        in_specs=[pl.BlockSpec((ts, 512), row), pl.BlockSpec((ts, 512), row),
                  pl.BlockSpec((HALO, 512), halo), pl.BlockSpec((HALO, 512), halo),
                  pl.BlockSpec((ts, 512), row), pl.BlockSpec((ts, D_MODEL), row),
                  pl.BlockSpec((6, D_MODEL), full), pl.BlockSpec((32, 512), full), pl.BlockSpec((8, 512), full),
                  pl.BlockSpec((2, D_MODEL), full), pl.BlockSpec((D_MODEL, D_MODEL), full),
                  pl.BlockSpec((512, 512), full)],
        out_specs=pl.BlockSpec((ts, D_MODEL), row),
        scratch_shapes=[pltpu.VMEM((ts + HALO, 512), F32)],
        compiler_params=_cparams(1),
    )(a, g, a, g, o, x, ada, wdw, prm, ln1, w_out, gm)


def _ffn_fwd(x1, ada, w1, w2, ln2, tgt, ts):
    s = x1.shape[0]
    ns = s // ts
    nf = N_SHARD

    def body(x1_ref, ada_ref, w1_ref, w2_ref, ln_ref, t_ref, dff_ref, dx1_ref, pg_ref, ffacc, u2):
        i = pl.program_id(0)
        f = pl.program_id(1)

        @pl.when((i == 0) & (f == 0))
        def _():
            pg_ref[...] = jnp.zeros_like(pg_ref)

        @pl.when(f == 0)
        def _():
            u2[...] = (x1_ref[...] * (1.0 + ada_ref[4:5, :]) + ada_ref[3:4, :]).astype(BF16)
            ffacc[...] = jnp.zeros_like(ffacc)

        h = _dot(u2[...], w1_ref[0])
        r = jnp.maximum(h, 0.0)
        ffacc[...] += _dot((r * r).astype(BF16), w2_ref[0])

        @pl.when(f == nf - 1)
        def _():
            ff = ffacc[...]
            r2 = ALPHA * x1_ref[...] + (1.0 + ada_ref[5:6, :]) * ff
            xh, rstd = _layernorm_stats(r2)
            yv = xh * ln_ref[0:1, :] + ln_ref[1:2, :]
            err = yv - t_ref[...]
            dy = err * (1.0 / D_MODEL)
            dr2 = _layernorm_bwd(dy, xh, rstd, ln_ref[0:1, :])
            pg_ref[0:1, :] += _rowsum(dy * xh)
            pg_ref[1:2, :] += _rowsum(dy)
            pg_ref[2:3, :] += _rowsum(dr2 * ff)
            pg_ref[3:4, :] += _rowsum(err * err) * (0.5 / D_MODEL)
            dff_ref[...] = ((1.0 + ada_ref[5:6, :]) * dr2).astype(BF16)
            dx1_ref[...] = ALPHA * dr2

    row = lambda i, f: (i, 0)
    full = lambda i, f: (0, 0)
    chunk = lambda i, f: (f, 0, 0)
    return pl.pallas_call(
        body, name="ffn_fwd", grid=(ns, nf),
        out_shape=(jax.ShapeDtypeStruct((s, D_MODEL), BF16), jax.ShapeDtypeStruct((s, D_MODEL), F32),
                   jax.ShapeDtypeStruct((8, D_MODEL), F32)),
        in_specs=[pl.BlockSpec((ts, D_MODEL), row), pl.BlockSpec((6, D_MODEL), full),
                  pl.BlockSpec((1, D_MODEL, FF_CHUNK), chunk), pl.BlockSpec((1, FF_CHUNK, D_MODEL), chunk),
                  pl.BlockSpec((2, D_MODEL), full), pl.BlockSpec((ts, D_MODEL), row)],
        out_specs=(pl.BlockSpec((ts, D_MODEL), row), pl.BlockSpec((ts, D_MODEL), row),
                   pl.BlockSpec((8, D_MODEL), full)),
        scratch_shapes=[pltpu.VMEM((ts, D_MODEL), F32), pltpu.VMEM((ts, D_MODEL), BF16)],
        compiler_params=_cparams(2),
    )(x1, ada, w1, w2, ln2, tgt)


def _ffn_bwd_chunk(f, x1, ada, w1, w2, dff, dx1, ts):
    s = x1.shape[0]
    ns = s // ts

    def body(x1_ref, ada_ref, w1_ref, w2_ref, dff_ref, dx1_in, dx1_out, dw1_ref, dw2_ref, pg_ref):
        i = pl.program_id(0)

        @pl.when(i == 0)
        def _():
            pg_ref[...] = jnp.zeros_like(pg_ref)
            dw1_ref[...] = jnp.zeros_like(dw1_ref)
            dw2_ref[...] = jnp.zeros_like(dw2_ref)

        x1v = x1_ref[...]
        u2 = (x1v * (1.0 + ada_ref[4:5, :]) + ada_ref[3:4, :]).astype(BF16)
        h = _dot(u2, w1_ref[0])
        r = jnp.maximum(h, 0.0)
        hid = (r * r).astype(BF16)
        dffv = dff_ref[...]
        dh = (_dot_nt(dffv, w2_ref[0]) * (2.0 * r)).astype(BF16)
        dw2_ref[0] += _dot_tn(hid, dffv)
        dw1_ref[0] += _dot_tn(u2, dh)
        du2 = _dot_nt(dh, w1_ref[0])
        dx1_out[...] = dx1_in[...] + du2 * (1.0 + ada_ref[4:5, :])
        pg_ref[0:1, :] += _rowsum(du2 * x1v)
        pg_ref[1:2, :] += _rowsum(du2)

    row = lambda i: (i, 0)
    full = lambda i: (0, 0)
    full3 = lambda i: (0, 0, 0)
    chunk = lambda i: (f, 0, 0)
    return pl.pallas_call(
        body, name=f"ffn_bwd_{f}", grid=(ns,),
        out_shape=(jax.ShapeDtypeStruct((s, D_MODEL), F32), jax.ShapeDtypeStruct((1, D_MODEL, FF_CHUNK), F32),
                   jax.ShapeDtypeStruct((1, FF_CHUNK, D_MODEL), F32), jax.ShapeDtypeStruct((8, D_MODEL), F32)),
        in_specs=[pl.BlockSpec((ts, D_MODEL), row), pl.BlockSpec((6, D_MODEL), full),
                  pl.BlockSpec((1, D_MODEL, FF_CHUNK), chunk), pl.BlockSpec((1, FF_CHUNK, D_MODEL), chunk),
                  pl.BlockSpec((ts, D_MODEL), row), pl.BlockSpec((ts, D_MODEL), row)],
        out_specs=(pl.BlockSpec((ts, D_MODEL), row), pl.BlockSpec((1, D_MODEL, FF_CHUNK), full3),
                   pl.BlockSpec((1, FF_CHUNK, D_MODEL), full3), pl.BlockSpec((8, D_MODEL), full)),
        compiler_params=_cparams(1),
    )(x1, ada, w1, w2, dff, dx1)


def _mid_bwd(a, g, o, x, dx1, ada, wdw, prm, ln1, w_out, gm, sel, dw1, dw2, ts):
    s = x.shape[0]
    ns = s // ts

    def body(a_ref, g_ref, ah_ref, gh_ref, o_ref, x_ref, dx1_ref, ada_ref, wdw_ref, prm_ref, ln_ref, wo_ref,
             gm_ref, sel_ref, dw1_hbm, dw2_hbm,
             do_ref, da_ref, dg_ref, dxa_ref, dwo_ref, pgm_ref, pgc_ref, dwdw_ref, land_hbm,
             ugx, dyx, lsem, ssem, rsem):
        i = pl.program_id(0)
        tile = ns - 1 - i

        def exchange():
            return _scatter_copies((dw1_hbm, dw2_hbm), land_hbm, _STACK_FF, lsem, ssem, rsem)

        @pl.when(i == 0)
        def _():
            exchange().start()
            dwo_ref[...] = jnp.zeros_like(dwo_ref)
            pgm_ref[...] = jnp.zeros_like(pgm_ref)
            pgc_ref[...] = jnp.zeros_like(pgc_ref)
            dwdw_ref[...] = jnp.zeros_like(dwdw_ref)
            dyx[ts:ts + HALO, :] = jnp.zeros((HALO, 512), F32)

        gmv = gm_ref[...]
        av = a_ref[...]
        ov = o_ref[...]
        sg_g, rs, yhat, yn, sg, co = _conv_branch(av, g_ref[...], ah_ref[...], gh_ref[...], tile == 0, ugx,
                                                  wdw_ref, prm_ref, gmv, ts)
        ra, oh, rc, ch, mi = _mix_inputs(ov, co, prm_ref)
        mixed = _dot(mi, wo_ref[...])
        gt1 = 1.0 + ada_ref[2:3, :]
        r1 = ALPHA * x_ref[...] + gt1 * mixed
        xh, rstd = _layernorm_stats(r1)
        dx1 = dx1_ref[...]
        pgm_ref[0:1, :] += _rowsum(dx1 * xh)
        pgm_ref[1:2, :] += _rowsum(dx1)
        dr1 = _layernorm_bwd(dx1, xh, rstd, ln_ref[0:1, :])
        dxa_ref[...] = ALPHA * dr1
        pgm_ref[2:3, :] += _rowsum(dr1 * mixed)
        dmixed = (gt1 * dr1).astype(BF16)
        dmi = _dot_nt(dmixed, wo_ref[...])
        dwo_ref[...] += _dot_tn(mi, dmixed)
        dna = dmi[:, 0:512]
        dnc = dmi[:, 512:1024]
        pgc_ref[3:4, :] += _rowsum(dna * oh)
        doh = dna * prm_ref[3:4, :]
        do = ra * (doh - oh * _mean_last(doh * oh))
        lane = lax.broadcasted_iota(jnp.int32, (ts, 128), 1)
        delta = _dot_f32(do * ov, sel_ref[...])
        for h in range(N_HEADS):
            own, at, _ = _aug_masks(lane, h)
            hi, mid, lo = _pieces(-delta[:, h:h + 1])
            dop = do[:, 128 * (h // 2):128 * (h // 2) + 128]
            do_ref[:, 128 * h:128 * h + 128] = jnp.where(
                own, dop, jnp.where(at(0), hi, jnp.where(at(1), mid, jnp.where(at(2), lo, 0.0)))).astype(BF16)
        pgc_ref[4:5, :] += _rowsum(dnc * ch)
        dch = dnc * prm_ref[4:5, :]
        dco = rc * (dch - ch * _mean_last(dch * ch))
        dyn = dco * (sg * (1.0 + yn * (1.0 - sg)))
        pgc_ref[1:2, :] += _rowsum(dyn * yhat)
        pgc_ref[2:3, :] += _rowsum(dyn)
        dyh = dyn * prm_ref[1:2, :]
        dy = rs * (dyh - _dot_f32(dyh, gmv) - yhat * _dot_f32(dyh * yhat, gmv))
        pgc_ref[0:1, :] += _rowsum(dy)
        dyr = dy.astype(BF16).astype(F32)
        dyx[0:ts, :] = dyr
        dug = jnp.zeros((ts, CONV_W), F32)
        for kk in range(CONV_K):
            off = HALO - (CONV_K - 1) + kk
            dwdw_ref[kk:kk + 1, :] += _rowsum(dyr * ugx[off:off + ts, :])
            back = CONV_K - 1 - kk
            dug = dug + wdw_ref[kk:kk + 1, :] * dyx[back:back + ts, :]
        dyx[ts:ts + HALO, :] = dyr[0:HALO, :]
        da_ref[...] = (dug * sg_g).astype(BF16)
        dg_ref[...] = (dug * av * sg_g * (1.0 - sg_g)).astype(BF16)

        @pl.when(i == ns - 1)
        def _():
            exchange().wait()

    row = lambda i: (ns - 1 - i, 0)
    full = lambda i: (0, 0)
    halo = lambda i: (_halo_index(ns - 1 - i, ts), 0)
    return pl.pallas_call(
        body, name="mid_bwd", grid=(ns,),
        out_shape=(jax.ShapeDtypeStruct((s, N_HEADS * 128), BF16), jax.ShapeDtypeStruct((s, 512), BF16),
                   jax.ShapeDtypeStruct((s, 512), BF16), jax.ShapeDtypeStruct((s, D_MODEL), F32),
                   jax.ShapeDtypeStruct((D_MODEL, D_MODEL), F32),
                   jax.ShapeDtypeStruct((8, D_MODEL), F32), jax.ShapeDtypeStruct((8, 512), F32),
                   jax.ShapeDtypeStruct((32, 512), F32), jax.ShapeDtypeStruct((N_SHARD, 2 * FF_CHUNK, D_MODEL), F32)),
        in_specs=[pl.BlockSpec((ts, 512), row), pl.BlockSpec((ts, 512), row),
                  pl.BlockSpec((HALO, 512), halo), pl.BlockSpec((HALO, 512), halo),
                  pl.BlockSpec((ts, 512), row), pl.BlockSpec((ts, D_MODEL), row), pl.BlockSpec((ts, D_MODEL), row),
                  pl.BlockSpec((6, D_MODEL), full), pl.BlockSpec((32, 512), full), pl.BlockSpec((8, 512), full),
                  pl.BlockSpec((2, D_MODEL), full), pl.BlockSpec((D_MODEL, D_MODEL), full),
                  pl.BlockSpec((512, 512), full), pl.BlockSpec((512, 128), full), ANY, ANY],
        out_specs=(pl.BlockSpec((ts, N_HEADS * 128), row), pl.BlockSpec((ts, 512), row), pl.BlockSpec((ts, 512), row),
                   pl.BlockSpec((ts, D_MODEL), row),
                   pl.BlockSpec((D_MODEL, D_MODEL), full), pl.BlockSpec((8, D_MODEL), full),
                   pl.BlockSpec((8, 512), full), pl.BlockSpec((32, 512), full), ANY),
        scratch_shapes=[pltpu.VMEM((ts + HALO, 512), F32), pltpu.VMEM((ts + HALO, 512), F32)] + _copy_sems(2),
        compiler_params=_cparams(1),
    )(a, g, a, g, o, x, dx1, ada, wdw, prm, ln1, w_out, gm, sel, dw1, dw2)


def _attn_bwd(qb, ka, va, doa, tk):
    s = qb.shape[0]
    nk = s // tk
    tq = tk

    def body(q_ref, do_ref, k_ref, v_ref, dq_ref, rs_ref, dk_ref, dv_ref, cs_ref):
        pair = pl.program_id(0)
        j = pl.program_id(1)

        @pl.when(j == 0)
        def _():
            dq_ref[...] = jnp.zeros_like(dq_ref)

        @pl.when((pair == 0) & (j == 0))
        def _():
            rs_ref[...] = jnp.zeros_like(rs_ref)
            cs_ref[...] = jnp.zeros_like(cs_ref)

        lane = lax.broadcasted_iota(jnp.int32, (tk, 128), 1)
        low = lane < HEAD_DIM
        t_off = lax.broadcasted_iota(jnp.int32, (tq, tk), 0)
        s_off = lax.broadcasted_iota(jnp.int32, (tq, tk), 1)

        def block(i, carry, diagonal):
            rows_q = pl.ds(pl.multiple_of(i * tq, tq), tq)
            dq_h, out = [], []
            for hh in range(2):
                cols = slice(128 * hh, 128 * hh + 128)
                dk_acc, dv_acc = carry[2 * hh:2 * hh + 2]
                qh = q_ref[rows_q, cols]
                dh = do_ref[rows_q, cols]
                pr = jnp.exp(_dot_nt(qh, k_ref[:, cols]))
                if diagonal:
                    pr = jnp.where(s_off <= t_off, pr, 0.0)
                ds = (pr * _dot_nt(dh, v_ref[:, cols])).astype(BF16)
                dq_h.append(_dot(ds, k_ref[:, cols]))
                out += [dk_acc + _dot_tn(ds, qh), dv_acc + _dot_tn(pr.astype(BF16), dh)]
            dq_ref[rows_q, :] += jnp.where(low, dq_h[0], dq_h[1])
            rs_ref[rows_q, :] += (jnp.where(lane == 2 * pair, dq_h[0][:, HEAD_DIM:HEAD_DIM + 1], 0.0)
                                  + jnp.where(lane == 2 * pair + 1, dq_h[1][:, 0:1], 0.0))
            return tuple(out)

        first = block(j, (jnp.zeros((tk, 128), F32),) * 4, diagonal=True)
        dk0, dv0, dk1, dv1 = lax.fori_loop(j + 1, nk, functools.partial(block, diagonal=False), first)
        dk_ref[...] = jnp.where(low, dk0, dk1).astype(BF16)
        dv_ref[...] = jnp.where(low, dv0, dv1).astype(BF16)
        rows_k = pl.ds(pl.multiple_of(j * tk, tk), tk)
        cs_ref[rows_k, :] += (jnp.where(lane == 2 * pair, dk0[:, HEAD_DIM + 3:HEAD_DIM + 4], 0.0)
                              + jnp.where(lane == 2 * pair + 1, dk1[:, 3:4], 0.0))

    whole = lambda p, j: (0, 0)
    return pl.pallas_call(
        body, name="attn_bwd", grid=(N_PAIRS, nk),
        out_shape=(jax.ShapeDtypeStruct((s, 512), F32), jax.ShapeDtypeStruct((s, 128), F32),
                   jax.ShapeDtypeStruct((s, 512), BF16), jax.ShapeDtypeStruct((s, 512), BF16),
                   jax.ShapeDtypeStruct((s, 128), F32)),
        in_specs=[pl.BlockSpec((s, 256), lambda p, j: (0, p)), pl.BlockSpec((s, 256), lambda p, j: (0, p)),
                  pl.BlockSpec((tk, 256), lambda p, j: (j, p)), pl.BlockSpec((tk, 256), lambda p, j: (j, p))],
        out_specs=(pl.BlockSpec((s, 128), lambda p, j: (0, p)), pl.BlockSpec((s, 128), whole),
                   pl.BlockSpec((tk, 128), lambda p, j: (j, p)), pl.BlockSpec((tk, 128), lambda p, j: (j, p)),
                   pl.BlockSpec((s, 128), whole)),
        compiler_params=_cparams(2),
    )(qb, doa, ka, va)


def _inproj_bwd(x, ada, w_p, dq, dk, dv, da, dg, dfc, drs, logf, dxa, dwo, ts):
    s = x.shape[0]
    ns = s // ts

    def body(x_ref, ada_ref, w_ref, dq_ref, dk_ref, dv_ref, da_ref, dg_ref, dfc_ref, drs_ref, lf_ref, dxa_ref,
             dwo_hbm, gx_ref, dw_hbm, pgi_ref, dbf_ref, land_hbm, carry, dw_vm, sem, lsem, ssem, rsem):
        i = pl.program_id(0)

        def exchange():
            return _scatter_copies((dwo_hbm,), land_hbm, _STACK_OUT, lsem, ssem, rsem)

        @pl.when(i == 0)
        def _():
            exchange().start()
            carry[...] = jnp.zeros_like(carry)
            dw_vm[...] = jnp.zeros_like(dw_vm)
            pgi_ref[...] = jnp.zeros_like(pgi_ref)
            dbf_ref[...] = jnp.zeros_like(dbf_ref)

        r = lax.broadcasted_iota(jnp.int32, (ts, ts), 0)
        cc = lax.broadcasted_iota(jnp.int32, (ts, ts), 1)
        tri = (cc >= r).astype(BF16)
        dlogf = carry[...] + _tri_dot(tri, drs_ref[...] - dfc_ref[...])
        carry[...] = dlogf[0:1, :]
        lane = lax.broadcasted_iota(jnp.int32, (ts, 128), 1)
        dz = jnp.where(lane < N_HEADS, dlogf * (1.0 - jnp.exp(lf_ref[...])), 0.0)
        dbf_ref[0:1, :] += _rowsum(dz)
        dproj = jnp.concatenate(
            [(dq_ref[...] * (HEAD_DIM ** -0.5)).astype(BF16), dk_ref[...], dv_ref[...], da_ref[...], dg_ref[...],
             dz.astype(BF16)], axis=-1)
        xv = x_ref[...]
        sc1 = 1.0 + ada_ref[1:2, :]
        u = (xv * sc1 + ada_ref[0:1, :]).astype(BF16)
        du = _dot_nt(dproj, w_ref[...])
        dw_vm[...] += _dot_tn(u, dproj)
        gx_ref[...] = dxa_ref[...] + du * sc1
        pgi_ref[0:1, :] += _rowsum(du * xv)
        pgi_ref[1:2, :] += _rowsum(du)

        @pl.when(i == ns - 1)
        def _():
            cp = pltpu.make_async_copy(dw_vm, dw_hbm, sem.at[0])
            cp.start()
            cp.wait()
            exchange().wait()

    row = lambda i: (ns - 1 - i, 0)
    full = lambda i: (0, 0)
    return pl.pallas_call(
        body, name="inproj_bwd", grid=(ns,),
        out_shape=(jax.ShapeDtypeStruct((s, D_MODEL), F32), jax.ShapeDtypeStruct((D_MODEL, N_IN_PAD), F32),
                   jax.ShapeDtypeStruct((8, D_MODEL), F32), jax.ShapeDtypeStruct((8, 128), F32),
                   jax.ShapeDtypeStruct(dwo.shape, F32)),
        in_specs=[pl.BlockSpec((ts, D_MODEL), row), pl.BlockSpec((6, D_MODEL), full),
                  pl.BlockSpec((D_MODEL, N_IN_PAD), full)]
        + [pl.BlockSpec((ts, 512), row)] * 5 + [pl.BlockSpec((ts, 128), row)] * 3
        + [pl.BlockSpec((ts, D_MODEL), row), ANY],
        out_specs=(pl.BlockSpec((ts, D_MODEL), row), ANY, pl.BlockSpec((8, D_MODEL), full),
                   pl.BlockSpec((8, 128), full), ANY),
        scratch_shapes=[pltpu.VMEM((1, 128), F32), pltpu.VMEM((D_MODEL, N_IN_PAD), F32),
                        pltpu.SemaphoreType.DMA((1,))] + _copy_sems(1),
        compiler_params=_cparams(1),
    )(x, ada, w_p, dq, dk, dv, da, dg, dfc, drs, logf, dxa, dwo)


def _small_reduce(packed):
    def body(p_ref, sum_ref, all_ref, ssem, rsem):
        x, y, c = _position()
        me = 4 * x + 2 * y + c
        all_ref[me] = p_ref[...]
        sends = []
        for k in range(1, 8):
            peer = (x ^ ((k >> 2) & 1), y ^ ((k >> 1) & 1), c ^ (k & 1))
            cp = pltpu.make_async_remote_copy(
                src_ref=p_ref, dst_ref=all_ref.at[me], send_sem=ssem.at[k], recv_sem=rsem.at[k],
                device_id=peer, device_id_type=MESH)
            cp.start()
            sends.append(cp)
        for k in range(1, 8):
            pltpu.make_async_remote_copy(
                src_ref=p_ref, dst_ref=all_ref.at[me ^ k], send_sem=ssem.at[k], recv_sem=rsem.at[k],
                device_id=(x, y, c), device_id_type=MESH).wait_recv()
        for cp in sends:
            cp.wait_send()
        total = all_ref[0]
        for dev in range(1, 8):
            total = total + all_ref[dev]
        sum_ref[...] = total
        loss = jnp.sum(total[SMALL_ROWS - 1:SMALL_ROWS, :], axis=-1, keepdims=True)
        sum_ref[SMALL_ROWS - 1:SMALL_ROWS, :] = jnp.broadcast_to(loss, (1, D_MODEL))

    vm = pl.BlockSpec(memory_space=pltpu.VMEM)
    return pl.pallas_call(
        body, name="small_reduce",
        out_shape=(jax.ShapeDtypeStruct((SMALL_ROWS, D_MODEL), F32), jax.ShapeDtypeStruct((8, SMALL_ROWS, D_MODEL), F32)),
        in_specs=[vm], out_specs=(vm, vm),
        scratch_shapes=[pltpu.SemaphoreType.DMA((8,)), pltpu.SemaphoreType.DMA((8,))],
        compiler_params=pltpu.CompilerParams(vmem_limit_bytes=VMEM_LIMIT),
    )(packed)


def _adam_math(gv, wv, mv, vv):
    m_new = B1 * mv + (1.0 - B1) * gv
    v_new = B2 * vv + (1.0 - B2) * (gv * gv)
    m_hat = m_new / (1.0 - B1 ** STEP)
    v_hat = v_new / (1.0 - B2 ** STEP)
    delta = -LR * (m_hat / (jnp.sqrt(v_hat) + ADAM_EPS) + WD * wv)
    return delta, m_new, v_new


def _adamw(gv, wv, mv, vv, name):
    rows, cols = gv.shape
    tr = rows
    for cand in (256, 128, 64, 32, 16, 8):
        if rows % cand == 0 and rows > cand:
            tr = cand
            break

    def body(g_ref, w_ref, m_ref, v_ref, d_ref, mo_ref, vo_ref):
        d_ref[...], mo_ref[...], vo_ref[...] = _adam_math(g_ref[...], w_ref[...], m_ref[...], v_ref[...])

    spec = pl.BlockSpec((tr, cols), lambda i: (i, 0))
    return pl.pallas_call(
        body, name=name, grid=(rows // tr,),
        out_shape=(jax.ShapeDtypeStruct((rows, cols), F32),) * 3,
        in_specs=[spec] * 4, out_specs=(spec,) * 3,
        compiler_params=_cparams(1),
    )(gv, wv, mv, vv)


def _w_ada_update(sct, dd, wv, mv, vv):
    rows, cols = wv.shape
    tr = 128

    def body(s_ref, d_ref, w_ref, m_ref, v_ref, g_ref, dl_ref, mo_ref, vo_ref):
        sv = s_ref[...]
        dv = d_ref[...]
        gv = sv[:, 0:1] * dv[0:1, :]
        for b in range(1, 8):
            gv = gv + sv[:, b:b + 1] * dv[b:b + 1, :]
        g_ref[...] = gv
        dl_ref[...], mo_ref[...], vo_ref[...] = _adam_math(gv, w_ref[...], m_ref[...], v_ref[...])

    spec = pl.BlockSpec((tr, cols), lambda i: (i, 0))
    return pl.pallas_call(
        body, name="w_ada_update", grid=(rows // tr,),
        out_shape=(jax.ShapeDtypeStruct((rows, cols), F32),) * 4,
        in_specs=[pl.BlockSpec((tr, 8), lambda i: (i, 0)), pl.BlockSpec((8, cols), lambda i: (0, 0))] + [spec] * 3,
        out_specs=(spec,) * 4,
        compiler_params=_cparams(1),
    )(sct, dd, wv, mv, vv)


def _grad_exchange(part):
    rows = part.shape[1]

    def body(p_hbm, land_hbm, lsem, ssem, rsem):
        cps = _scatter_copies((p_hbm,), land_hbm, ((0, rows),), lsem, ssem, rsem)
        cps.start()
        cps.wait()

    return pl.pallas_call(
        body, name="grad_exchange",
        out_shape=jax.ShapeDtypeStruct(part.shape, part.dtype),
        in_specs=[ANY], out_specs=ANY,
        scratch_shapes=_copy_sems(1),
    )(part)


def _sum_chips(land_in, land_out, land_ff):
    tr = 256
    n_in, n_out = 768 // tr, 256 // tr

    def body(in_ref, out_ref, ff_ref, s_ref):
        i = pl.program_id(0)

        def total(ref):
            s_ref[...] = ((ref[0].astype(F32) + ref[1].astype(F32)) + ref[2].astype(F32)) + ref[3].astype(F32)

        pl.when(i < n_in)(lambda: total(in_ref))
        pl.when((i >= n_in) & (i < n_in + n_out))(lambda: total(out_ref))
        pl.when(i >= n_in + n_out)(lambda: total(ff_ref))

    return pl.pallas_call(
        body, name="sum_chips", grid=(STACK_ROWS // tr,),
        out_shape=jax.ShapeDtypeStruct((STACK_ROWS, D_MODEL), F32),
        in_specs=[pl.BlockSpec((N_SHARD, tr, D_MODEL), lambda i: (0, jnp.minimum(i, n_in - 1), 0)),
                  pl.BlockSpec((N_SHARD, tr, D_MODEL), lambda i: (0, jnp.clip(i - n_in, 0, n_out - 1), 0)),
                  pl.BlockSpec((N_SHARD, tr, D_MODEL), lambda i: (0, jnp.maximum(i - n_in - n_out, 0), 0))],
        out_specs=pl.BlockSpec((tr, D_MODEL), lambda i: (i, 0)),
        compiler_params=_cparams(1),
    )(land_in, land_out, land_ff)


def _core_swap(part):
    def body(p_ref, o_ref, ssem, rsem):
        x, y, c = _position()
        cp = pltpu.make_async_remote_copy(src_ref=p_ref, dst_ref=o_ref, send_sem=ssem, recv_sem=rsem,
                                          device_id=(x, y, 1 - c), device_id_type=MESH)
        cp.start()
        cp.wait()

    return pl.pallas_call(
        body, name="core_swap",
        out_shape=jax.ShapeDtypeStruct(part.shape, part.dtype),
        in_specs=[ANY], out_specs=ANY,
        scratch_shapes=[pltpu.SemaphoreType.DMA, pltpu.SemaphoreType.DMA],
    )(part)


def _add_pair(mine, other):
    tr = 256

    def body(a_ref, b_ref, o_ref):
        o_ref[...] = a_ref[...] + b_ref[...]

    spec = pl.BlockSpec((tr, D_MODEL), lambda i: (i, 0))
    return pl.pallas_call(
        body, name="add_pair", grid=(STACK_ROWS // tr,),
        out_shape=jax.ShapeDtypeStruct((STACK_ROWS, D_MODEL), F32),
        in_specs=[spec, spec], out_specs=spec,
        compiler_params=_cparams(1),
    )(mine, other)


def _pad_lanes(v, width=D_MODEL):
    v = v.reshape(1, -1)
    return jnp.pad(v, ((0, 0), (0, width - v.shape[1])))


def _pack_small(b_ada, ln1_g, ln1_b, ln2_g, ln2_b, b_dw, gn_g, gn_b, g_attn, g_conv, b_forget, w_dw_full, last):
    rows = [b_ada.reshape(6, D_MODEL)] + [_pad_lanes(v) for v in
                                          (ln1_g, ln1_b, ln2_g, ln2_b, b_dw, gn_g, gn_b, g_attn, g_conv, b_forget)]
    rows.append(jnp.pad(w_dw_full.reshape(CONV_K, -1), ((0, 0), (0, D_MODEL - w_dw_full.reshape(CONV_K, -1).shape[1]))))
    rows.append(_pad_lanes(last))
    return jnp.concatenate(rows, axis=0)


def _unpack_small(p):
    return dict(b_ada=p[0:6].reshape(1, 6 * D_MODEL), ln1_g=p[6:7], ln1_b=p[7:8], ln2_g=p[8:9], ln2_b=p[9:10],
                b_dw=p[10:11, :512], gn_g=p[11:12, :512], gn_b=p[12:13, :512], g_attn_out=p[13:14, :512],
                g_conv_out=p[14:15, :512], b_forget=p[15:16, :N_HEADS])


def kernel(x, c, w_ada, b_ada, w_in, b_forget, w_dw, b_dw, gn_g, gn_b, g_attn_out, g_conv_out, w_out, ln1_g, ln1_b, w_ff1, w_ff2, ln2_g, ln2_b, loss_target, m_w_ada, m_b_ada, m_w_in, m_b_forget, m_w_dw, m_b_dw, m_gn_g, m_gn_b, m_g_attn_out, m_g_conv_out, m_w_out, m_ln1_g, m_ln1_b, m_w_ff1, m_w_ff2, m_ln2_g, m_ln2_b, v_w_ada, v_b_ada, v_w_in, v_b_forget, v_w_dw, v_b_dw, v_gn_g, v_gn_b, v_g_attn_out, v_g_conv_out, v_w_out, v_ln1_g, v_ln1_b, v_w_ff1, v_w_ff2, v_ln2_g, v_ln2_b):
    seq = x.shape[1]
    ts = min(512, seq // 2)
    tq = min(512, seq // 2)
    ts_mid = min(256, seq // 2)
    q_idx = 2 * lax.axis_index("x") + lax.axis_index("y")
    xs = x[0]
    tgt = loss_target[0]

    sc_all, ada = _ada_fwd(c, w_ada[0], b_ada)
    w_in_sh = jnp.pad(w_in[0], ((0, 0), (0, IN_SHARD_PAD - IN_SHARD))).astype(BF16)
    wdw_rows = jnp.pad(w_dw[0, :, 0, :], ((0, 1), (0, 0)))
    win_all, wdw_all = _weight_gather([w_in_sh, wdw_rows])
    w_in_full = jnp.transpose(win_all[:, :, :IN_SHARD], (1, 0, 2)).reshape(D_MODEL, N_IN)
    w_p = jnp.concatenate([w_in_full[:, 0:1536], w_in_full[:, 1544:2568], w_in_full[:, 1536:1544],
                           jnp.zeros((D_MODEL, 120), BF16)], axis=1)
    bf = _pad_lanes(b_forget, 128)
    wdw_full = lax.reduce_precision(jnp.transpose(wdw_all, (1, 0, 2)).reshape(32, 512), 8, 7)

    prm = jnp.concatenate([b_dw, gn_g, gn_b, g_attn_out, g_conv_out, jnp.zeros((3, 512), F32)], axis=0)
    ln1 = jnp.concatenate([ln1_g, ln1_b], axis=0)
    ln2 = jnp.concatenate([ln2_g, ln2_b], axis=0)
    ch = jnp.arange(512)
    gm = ((ch[:, None] // HEAD_DIM == ch[None, :] // HEAD_DIM).astype(F32) / HEAD_DIM).astype(BF16)
    sel = (ch[:, None] // HEAD_DIM == jnp.arange(128)[None, :]).astype(BF16)

    qa, ka, va, a, g, logf = _inproj_fwd(xs, ada, w_p, bf, ts)
    o, qb, (wout_all, w1_all, w2_all) = _attn_fwd(
        qa, ka, va, [w_out[0].astype(BF16), w_ff1[0].astype(BF16), w_ff2[0].astype(BF16)], tq)
    w_out_full = wout_all.reshape(D_MODEL, D_MODEL)
    x1 = _mid_fwd(a, g, o, xs, ada, wdw_full, prm, ln1, w_out_full, gm, ts_mid)
    dff, dx1, pg_f = _ffn_fwd(x1, ada, w1_all, w2_all, ln2, tgt, ts)

    dw1, dw2, pg_b = [], [], jnp.zeros((8, D_MODEL), F32)
    for f in range(N_SHARD):
        dx1, dw1_f, dw2_f, pg_bf = _ffn_bwd_chunk(f, x1, ada, w1_all, w2_all, dff, dx1, ts)
        dw1.append(dw1_f)
        dw2.append(dw2_f)
        pg_b = pg_b + pg_bf
    dw1 = jnp.concatenate(dw1, axis=0)
    dw2 = jnp.concatenate(dw2, axis=0)
    doa, da, dg, dxa, dwo, pgm, pgc, dwdw, land_ff = _mid_bwd(
        a, g, o, xs, dx1, ada, wdw_full, prm, ln1, w_out_full, gm, sel, dw1, dw2, ts_mid)
    dq, drs, dk, dv, dfc = _attn_bwd(qb, ka, va, doa, tq)
    gx, dwp, pgi, dbf, land_out = _inproj_bwd(xs, ada, w_p, dq, dk, dv, da, dg, dfc, drs, logf, dxa,
                                              dwo.reshape(N_SHARD, 256, D_MODEL), ts)

    d_ada = jnp.concatenate([pgi[1:2], pgi[0:1], pgm[2:3], pg_b[1:2], pg_b[0:1], pg_f[2:3]], axis=0)
    packed = _pack_small(d_ada, pgm[0:1], pgm[1:2], pg_f[0:1], pg_f[1:2], pgc[0:1], pgc[1:2], pgc[2:3], pgc[3:4],
                         pgc[4:5], dbf[0:1, :N_HEADS], dwdw[0:CONV_K], pg_f[3:4])
    small_sum, small_all = _small_reduce(packed)
    loss = small_sum[SMALL_ROWS - 1, 0]
    gsm = _unpack_small(small_sum)
    g_wdw = lax.dynamic_slice(small_sum[16:16 + CONV_K, :512], (0, q_idx * 128), (CONV_K, 128))

    zrow = jnp.zeros((CONV_K + 1, D_MODEL), F32)
    w_small = _pack_small(b_ada, ln1_g, ln1_b, ln2_g, ln2_b, b_dw, gn_g, gn_b, g_attn_out,
                          g_conv_out, b_forget, zrow[:CONV_K, :512], zrow[0])
    m_small = _pack_small(m_b_ada, m_ln1_g, m_ln1_b, m_ln2_g, m_ln2_b, m_b_dw, m_gn_g, m_gn_b, m_g_attn_out,
                          m_g_conv_out, m_b_forget, zrow[:CONV_K, :512], zrow[0])
    v_small = _pack_small(v_b_ada, v_ln1_g, v_ln1_b, v_ln2_g, v_ln2_b, v_b_dw, v_gn_g, v_gn_b, v_g_attn_out,
                          v_g_conv_out, v_b_forget, zrow[:CONV_K, :512], zrow[0])
    d_small, mn_small, vn_small = (_unpack_small(t) for t in _adamw(small_sum, w_small, m_small, v_small, "adamw_small"))
    d_wdw, mn_wdw, vn_wdw = _adamw(g_wdw, w_dw[0, :, 0, :], m_w_dw[0, :, 0, :], v_w_dw[0, :, 0, :], "adamw_wdw")

    dd = lax.dynamic_slice(small_all[:, 0:6, :].reshape(8, 6 * D_MODEL), (0, q_idx * 1536), (8, 1536))
    g_wada, d_wada, mn_wada, vn_wada = _w_ada_update(sc_all.T, dd, w_ada[0], m_w_ada[0], v_w_ada[0])

    dw_in_cols = jnp.concatenate([dwp[:, 0:1536], dwp[:, 2560:2568], dwp[:, 1536:2560]], axis=1)
    dw_in_sh = jnp.pad(jnp.transpose(dw_in_cols.reshape(D_MODEL, N_SHARD, IN_SHARD), (1, 0, 2)),
                       ((0, 0), (0, 0), (0, IN_SHARD_PAD - IN_SHARD))).reshape(N_SHARD, 768, D_MODEL)
    part = _sum_chips(_grad_exchange(dw_in_sh.astype(BF16)), land_out, land_ff)
    total = _add_pair(part, _core_swap(part))
    g_win = total[0:768].reshape(D_MODEL, IN_SHARD_PAD)[:, :IN_SHARD]
    g_wout = total[768:1024]
    g_w1 = total[1024:2048]
    g_w2 = total[2048:3072]
    d_win, mn_win, vn_win = _adamw(g_win, w_in[0], m_w_in[0], v_w_in[0], "adamw_w_in")
    d_wout, mn_wout, vn_wout = _adamw(g_wout, w_out[0], m_w_out[0], v_w_out[0], "adamw_w_out")
    d_w1, mn_w1, vn_w1 = _adamw(g_w1, w_ff1[0], m_w_ff1[0], v_w_ff1[0], "adamw_w_ff1")
    d_w2, mn_w2, vn_w2 = _adamw(g_w2, w_ff2[0], m_w_ff2[0], v_w_ff2[0], "adamw_w_ff2")

    def group(wada, sm, win, wdw, wout, w1, w2):
        return (wada[None], sm["b_ada"], win[None], sm["b_forget"], wdw[None, :, None, :], sm["b_dw"], sm["gn_g"],
                sm["gn_b"], sm["g_attn_out"], sm["g_conv_out"], wout[None], sm["ln1_g"], sm["ln1_b"], w1[None],
                w2[None], sm["ln2_g"], sm["ln2_b"])

    return ((loss, gx[None])
            + group(g_wada, gsm, g_win, g_wdw, g_wout, g_w1, g_w2)
            + group(d_wada, d_small, d_win, d_wdw, d_wout, d_w1, d_w2)
            + group(mn_wada, mn_small, mn_win, mn_wdw, mn_wout, mn_w1, mn_w2)
            + group(vn_wada, vn_small, vn_win, vn_wdw, vn_wout, vn_w1, vn_w2))
```

```python
import functools

import jax
import jax.numpy as jnp
from jax import lax
from jax.experimental import pallas as pl
from jax.experimental.pallas import tpu as pltpu

F32 = jnp.float32
BF16 = jnp.bfloat16
MESH = pl.DeviceIdType.MESH
ANY = pl.BlockSpec(memory_space=pl.ANY)

D_MODEL = 1024
HEAD_DIM = 64
ATTN_W = 512
CONV_W = 512
N_HEADS = 8
N_PAIRS = 4
CONV_K = 31
HALO = 32
D_FF = 4096
N_SHARD = 4
FF_CHUNK = D_FF // N_SHARD
N_IN = 2568
IN_SHARD = N_IN // N_SHARD
IN_SHARD_PAD = 768
N_IN_PAD = 5 * 512 + 128
LN_EPS = 1e-5
ALPHA = 2.0 ** 0.25
LR, B1, B2, ADAM_EPS, WD, STEP = 0.001, 0.9, 0.999, 1e-08, 0.01, 10
VMEM_LIMIT = 56 * 1024 * 1024
SMALL_ROWS = 48
STACK_ROWS = 768 + 256 + 1024 + 1024
_STACK_OUT = ((0, 256),)
_STACK_FF = ((0, 1024), (1024, 1024))


def _cparams(n_axes):
    return pltpu.CompilerParams(dimension_semantics=("arbitrary",) * n_axes, vmem_limit_bytes=VMEM_LIMIT)


def _dot(a, b):
    return jnp.dot(a, b, preferred_element_type=F32)


def _dot_nt(a, b):
    return lax.dot_general(a, b, (((1,), (1,)), ((), ())), preferred_element_type=F32)


def _dot_tn(a, b):
    return lax.dot_general(a, b, (((0,), (0,)), ((), ())), preferred_element_type=F32)


def _dot_f32(a, b):
    hi, mid, lo = _split3(a)
    return _dot(hi, b) + _dot(mid, b) + _dot(lo, b)


def _split3(x):
    hi = x.astype(BF16)
    r = x - hi.astype(F32)
    mid = r.astype(BF16)
    lo = (r - mid.astype(F32)).astype(BF16)
    return hi, mid, lo


def _tri_dot(tri, x):
    hi, mid, lo = _split3(x)
    return _dot(tri, hi) + _dot(tri, mid) + _dot(tri, lo)


def _rowsum(x):
    return jnp.sum(x, axis=0, keepdims=True)


def _mean_last(x):
    return jnp.mean(x, axis=-1, keepdims=True)


def _position():
    x, y, c = lax.axis_index("x"), lax.axis_index("y"), lax.axis_index("c")
    return x, y, c


def _ada_fwd(c_row, w_ada, b_ada):
    n_col = w_ada.shape[1]

    def body(c_ref, w_ref, b_ref, sc_ref, ada_ref, call_ref, part_ref, pall_ref, s1, r1, s2, r2):
        x, y, c = _position()
        me = 4 * x + 2 * y + c
        q = 2 * x + y
        call_ref[me] = jnp.broadcast_to(c_ref[...], (8, D_MODEL))

        def c_copy(k):
            peer = (x ^ ((k >> 2) & 1), y ^ ((k >> 1) & 1), c ^ (k & 1))
            return pltpu.make_async_remote_copy(
                src_ref=call_ref.at[me], dst_ref=call_ref.at[me], send_sem=s1.at[k], recv_sem=r1.at[k],
                device_id=peer, device_id_type=MESH)

        def c_recv(k):
            src = me ^ k
            return pltpu.make_async_remote_copy(
                src_ref=call_ref.at[src], dst_ref=call_ref.at[src], send_sem=s1.at[k], recv_sem=r1.at[k],
                device_id=(x, y, c), device_id_type=MESH)

        sends = [c_copy(k) for k in range(1, 8)]
        for cp in sends:
            cp.start()
        for k in range(1, 8):
            c_recv(k).wait_recv()
        for cp in sends:
            cp.wait_send()

        row = lax.broadcasted_iota(jnp.int32, (8, D_MODEL), 0)
        c_all = jnp.zeros((8, D_MODEL), F32)
        for j in range(8):
            c_all = jnp.where(row == j, call_ref[j], c_all)
        sc_all = c_all * jax.nn.sigmoid(c_all)
        sc_ref[...] = sc_all
        b_slice = b_ref[:, pl.ds(pl.multiple_of(q * n_col, 128), n_col)]
        part = _dot(sc_all.astype(BF16), w_ref[...].astype(BF16)) + b_slice
        part_ref[...] = part
        pall_ref[q] = part

        def p_copy(j):
            peer = (x ^ ((j >> 1) & 1), y ^ (j & 1), c)
            return pltpu.make_async_remote_copy(
                src_ref=part_ref, dst_ref=pall_ref.at[q], send_sem=s2.at[j], recv_sem=r2.at[j],
                device_id=peer, device_id_type=MESH)

        def p_recv(j):
            src_q = q ^ j
            return pltpu.make_async_remote_copy(
                src_ref=part_ref, dst_ref=pall_ref.at[src_q], send_sem=s2.at[j], recv_sem=r2.at[j],
                device_id=(x, y, c), device_id_type=MESH)

        sends2 = [p_copy(j) for j in range(1, 4)]
        for cp in sends2:
            cp.start()
        for j in range(1, 4):
            p_recv(j).wait_recv()
        for cp in sends2:
            cp.wait_send()
        for qq in range(N_SHARD):
            ada_ref[qq] = pall_ref[qq, pl.ds(me, 1), :]

    vm = pl.BlockSpec(memory_space=pltpu.VMEM)
    sc_all, ada = pl.pallas_call(
        body, name="ada_fwd",
        out_shape=(jax.ShapeDtypeStruct((8, D_MODEL), F32), jax.ShapeDtypeStruct((N_SHARD, 1, n_col), F32)),
        in_specs=[vm, vm, vm], out_specs=(vm, vm),
        scratch_shapes=[pltpu.VMEM((8, 8, D_MODEL), F32), pltpu.VMEM((8, n_col), F32),
                        pltpu.VMEM((N_SHARD, 8, n_col), F32),
                        pltpu.SemaphoreType.DMA((8,)), pltpu.SemaphoreType.DMA((8,)),
                        pltpu.SemaphoreType.DMA((4,)), pltpu.SemaphoreType.DMA((4,))],
        compiler_params=pltpu.CompilerParams(vmem_limit_bytes=VMEM_LIMIT),
    )(c_row, w_ada, b_ada)
    return sc_all, ada.reshape(6, D_MODEL)


class _ChipCopies:
    def __init__(self, lsem, ssem, rsem):
        self.x, self.y, self.c = _position()
        self.q = 2 * self.x + self.y
        self.lsem, self.ssem, self.rsem = lsem, ssem, rsem
        self.local, self.send, self.recv = [], [], []

    def _remote(self, a, j, src, dst, peer):
        return pltpu.make_async_remote_copy(src_ref=src, dst_ref=dst, send_sem=self.ssem.at[a, j],
                                            recv_sem=self.rsem.at[a, j], device_id=peer, device_id_type=MESH)

    def add(self, a, own_src, own_dst, src_for, dst_mine, dst_from):
        x, y, c, q = self.x, self.y, self.c, self.q
        self.local.append(pltpu.make_async_copy(own_src, own_dst, self.lsem.at[a]))
        for j in range(1, 4):
            peer = (x ^ ((j >> 1) & 1), y ^ (j & 1), c)
            self.send.append(self._remote(a, j, src_for(q ^ j), dst_mine, peer))
            self.recv.append(self._remote(a, j, own_src, dst_from(q ^ j), (x, y, c)))

    def start(self):
        for cp in self.local + self.send:
            cp.start()

    def wait(self):
        for cp in self.recv:
            cp.wait_recv()
        for cp in self.send:
            cp.wait_send()
        for cp in self.local:
            cp.wait()


def _gather_copies(ins, outs, lsem, ssem, rsem):
    cps = _ChipCopies(lsem, ssem, rsem)
    for a in range(len(ins)):
        cps.add(a, ins[a], outs[a].at[cps.q], lambda chip, a=a: ins[a], outs[a].at[cps.q],
                lambda chip, a=a: outs[a].at[chip])
    return cps


def _scatter_copies(ins, land, offs, lsem, ssem, rsem):
    cps = _ChipCopies(lsem, ssem, rsem)
    for a, (off, rows) in enumerate(offs):
        cps.add(a, ins[a].at[cps.q], land.at[cps.q, pl.ds(off, rows)], lambda chip, a=a: ins[a].at[chip],
                land.at[cps.q, pl.ds(off, rows)], lambda chip, off=off, rows=rows: land.at[chip, pl.ds(off, rows)])
    return cps


def _copy_sems(n):
    return [pltpu.SemaphoreType.DMA((n,)), pltpu.SemaphoreType.DMA((n, 4)), pltpu.SemaphoreType.DMA((n, 4))]


def _weight_gather(shards):
    n = len(shards)

    def body(*refs):
        cps = _gather_copies(refs[:n], refs[n:2 * n], *refs[2 * n:])
        cps.start()
        cps.wait()

    return pl.pallas_call(
        body, name="weight_gather",
        out_shape=tuple(jax.ShapeDtypeStruct((N_SHARD,) + s.shape, s.dtype) for s in shards),
        in_specs=[ANY] * n, out_specs=tuple([ANY] * n),
        scratch_shapes=[pltpu.SemaphoreType.DMA((n,)), pltpu.SemaphoreType.DMA((n, 4)),
                        pltpu.SemaphoreType.DMA((n, 4))],
    )(*shards)


def _aug_masks(lane, h):
    a0 = HEAD_DIM if h % 2 == 0 else 0
    own = (lane < HEAD_DIM) if h % 2 == 0 else (lane >= HEAD_DIM)
    return own, (lambda k: lane == a0 + k), (lambda k0, k1: (lane >= a0 + k0) & (lane < a0 + k1))


def _pieces(x):
    hi, mid, lo = _split3(x)
    return hi.astype(F32), mid.astype(F32), lo.astype(F32)


def _inproj_fwd(x, ada, w_p, bf, ts):
    s = x.shape[0]
    ns = s // ts

    def body(x_ref, ada_ref, w_ref, bf_ref, q_ref, k_ref, v_ref, a_ref, g_ref, lf_ref, carry):
        i = pl.program_id(0)

        @pl.when(i == 0)
        def _():
            carry[...] = jnp.zeros_like(carry)

        u = (x_ref[...] * (1.0 + ada_ref[1:2, :]) + ada_ref[0:1, :]).astype(BF16)
        proj = _dot(u, w_ref[...])
        a_ref[...] = proj[:, 1536:2048]
        g_ref[...] = proj[:, 2048:2560]
        z = proj[:, 2560:2688] + bf_ref[...]
        lane = lax.broadcasted_iota(jnp.int32, (ts, 128), 1)
        logf = jnp.minimum(z, 0.0) - jnp.log(1.0 + jnp.exp(-jnp.abs(z)))
        logf = jnp.where(lane < N_HEADS, logf, 0.0)
        lf_ref[...] = logf
        r = lax.broadcasted_iota(jnp.int32, (ts, ts), 0)
        cc = lax.broadcasted_iota(jnp.int32, (ts, ts), 1)
        tri = (cc <= r).astype(BF16)
        fc = _tri_dot(tri, logf) + carry[...]
        carry[...] = fc[ts - 1:ts, :]
        for h in range(N_HEADS):
            pc = slice(128 * (h // 2), 128 * (h // 2) + 128)
            hc = slice(128 * h, 128 * h + 128)
            own, at, span = _aug_masks(lane, h)
            hi, mid, lo = _pieces(fc[:, h:h + 1])
            qp = proj[:, pc] * (HEAD_DIM ** -0.5)
            kp = proj[:, 512:1024][:, pc]
            vp = proj[:, 1024:1536][:, pc]
            q_aug = jnp.where(own, qp, jnp.where(at(0), hi, jnp.where(at(1), mid, jnp.where(at(2), lo,
                              jnp.where(span(3, 6), 1.0, 0.0)))))
            k_aug = jnp.where(own, kp, jnp.where(span(0, 3), 1.0, jnp.where(at(3), -hi, jnp.where(at(4), -mid,
                              jnp.where(at(5), -lo, jnp.where(span(6, 9), -1.0, 0.0))))))
            v_aug = jnp.where(own, vp, jnp.where(span(0, 3), 1.0, 0.0))
            q_ref[:, hc] = q_aug.astype(BF16)
            k_ref[:, hc] = k_aug.astype(BF16)
            v_ref[:, hc] = v_aug.astype(BF16)

    row = lambda i: (i, 0)
    full = lambda i: (0, 0)
    return pl.pallas_call(
        body, name="inproj_fwd", grid=(ns,),
        out_shape=(jax.ShapeDtypeStruct((s, N_HEADS * 128), BF16),) * 3 + (jax.ShapeDtypeStruct((s, 512), F32),) * 2
        + (jax.ShapeDtypeStruct((s, 128), F32),),
        in_specs=[pl.BlockSpec((ts, D_MODEL), row), pl.BlockSpec((6, D_MODEL), full),
                  pl.BlockSpec((D_MODEL, N_IN_PAD), full), pl.BlockSpec((1, 128), full)],
        out_specs=(pl.BlockSpec((ts, N_HEADS * 128), row),) * 3 + (pl.BlockSpec((ts, 512), row),) * 2
        + (pl.BlockSpec((ts, 128), row),),
        scratch_shapes=[pltpu.VMEM((1, 128), F32)],
        compiler_params=_cparams(1),
    )(x, ada, w_p, bf)


def _attn_fwd(qa, ka, va, shards, tq):
    s = qa.shape[0]
    nq = s // tq
    tk = tq
    n = len(shards)

    def body(q_ref, k_ref, v_ref, *rest):
        sh_in, (o_ref, qb_ref), sh_out = rest[:n], rest[n:n + 2], rest[n + 2:2 * n + 2]
        lsem, ssem, rsem = rest[2 * n + 2:]
        pair = pl.program_id(0)
        i = pl.program_id(1)

        @pl.when((pair == 0) & (i == 0))
        def _():
            _gather_copies(sh_in, sh_out, lsem, ssem, rsem).start()

        lane = lax.broadcasted_iota(jnp.int32, (tq, 128), 1)
        t_off = lax.broadcasted_iota(jnp.int32, (tq, tk), 0)
        s_off = lax.broadcasted_iota(jnp.int32, (tq, tk), 1)

        def block(j, carry, diagonal):
            start = pl.multiple_of(j * tk, tk)
            out = []
            for hh in range(2):
                cols = slice(128 * hh, 128 * hh + 128)
                m, acc = carry[2 * hh:2 * hh + 2]
                sc = _dot_nt(q_ref[:, cols], k_ref[pl.ds(start, tk), cols])
                if diagonal:
                    sc = jnp.where(s_off <= t_off, sc, -jnp.inf)
                m_new = jnp.maximum(m, jnp.max(sc, axis=-1, keepdims=True))
                pr = jnp.exp(sc - m_new).astype(BF16)
                acc = acc * jnp.exp(m - m_new) + _dot(pr, v_ref[pl.ds(start, tk), cols])
                out += [m_new, acc]
            return tuple(out)

        init = (jnp.full((tq, 1), -jnp.inf, F32), jnp.zeros((tq, 128), F32)) * 2
        res = lax.fori_loop(0, i, functools.partial(block, diagonal=False), init)
        res = block(i, res, diagonal=True)
        outs = []
        for hh in range(2):
            cols = slice(128 * hh, 128 * hh + 128)
            _, at, _ = _aug_masks(lane, hh)
            a0 = HEAD_DIM if hh == 0 else 0
            m, acc = res[2 * hh:2 * hh + 2]
            denom = acc[:, a0:a0 + 1]
            outs.append(acc / denom)
            hi, mid, lo = _split3(m + jnp.log(denom))
            qb_ref[:, cols] = jnp.where(at(6), hi, jnp.where(at(7), mid, jnp.where(at(8), lo, q_ref[:, cols])))
        o_ref[...] = jnp.where(lane < HEAD_DIM, outs[0], outs[1])

        @pl.when((pair == N_PAIRS - 1) & (i == nq - 1))
        def _():
            _gather_copies(sh_in, sh_out, lsem, ssem, rsem).wait()

    res = pl.pallas_call(
        body, name="attn_fwd", grid=(N_PAIRS, nq),
        out_shape=(jax.ShapeDtypeStruct((s, 512), F32), jax.ShapeDtypeStruct((s, N_HEADS * 128), BF16))
        + tuple(jax.ShapeDtypeStruct((N_SHARD,) + w.shape, w.dtype) for w in shards),
        in_specs=[pl.BlockSpec((tq, 256), lambda p, i: (i, p)), pl.BlockSpec((s, 256), lambda p, i: (0, p)),
                  pl.BlockSpec((s, 256), lambda p, i: (0, p))] + [ANY] * n,
        out_specs=(pl.BlockSpec((tq, 128), lambda p, i: (i, p)), pl.BlockSpec((tq, 256), lambda p, i: (i, p)))
        + (ANY,) * n,
        scratch_shapes=_copy_sems(n),
        compiler_params=_cparams(2),
    )(qa, ka, va, *shards)
    return res[0], res[1], res[2:]


def _fill_shifts(buf, shifted, ts):
    for s in range(1, 8):
        shifted[s - 1] = buf[s:s + ts + HALO - 8, :]


def _rows_at(buf, shifted, off, ts):
    s = off % 8
    if s == 0:
        return buf[off:off + ts, :]
    return shifted[s - 1, off - s:off - s + ts, :]


def _shift_scratch(ts):
    return pltpu.VMEM((7, ts + HALO - 8, 512), F32)


def _conv_branch(a, g, ah, gh, first, ugx, ush, wdw_ref, prm_ref, gm, ts):
    sg_g = jax.nn.sigmoid(g)
    ug = (a * sg_g).astype(BF16).astype(F32)
    ugh = jnp.where(first, 0.0, (ah * jax.nn.sigmoid(gh)).astype(BF16).astype(F32))
    ugx[0:HALO, :] = ugh
    ugx[HALO:HALO + ts, :] = ug
    _fill_shifts(ugx, ush, ts)
    y = jnp.zeros((ts, CONV_W), F32) + prm_ref[0:1, :]
    for kk in range(CONV_K):
        off = HALO - (CONV_K - 1) + kk
        y = y + wdw_ref[kk:kk + 1, :] * _rows_at(ugx, ush, off, ts)
    mu = _dot_f32(y, gm)
    d = y - mu
    var = _dot_f32(d * d, gm)
    rs = lax.rsqrt(var + LN_EPS)
    yhat = d * rs
    yn = yhat * prm_ref[1:2, :] + prm_ref[2:3, :]
    sg = jax.nn.sigmoid(yn)
    co = yn * sg
    return sg_g, rs, yhat, yn, sg, co


def _mix_inputs(o, co, prm_ref):
    ra = lax.rsqrt(_mean_last(o * o) + LN_EPS)
    oh = o * ra
    rc = lax.rsqrt(_mean_last(co * co) + LN_EPS)
    ch = co * rc
    mi = jnp.concatenate([oh * prm_ref[3:4, :], ch * prm_ref[4:5, :]], axis=-1).astype(BF16)
    return ra, oh, rc, ch, mi


def _layernorm_stats(r):
    mu = _mean_last(r)
    d = r - mu
    rstd = lax.rsqrt(_mean_last(d * d) + LN_EPS)
    return d * rstd, rstd


def _layernorm_bwd(dout, xh, rstd, gain):
    dxh = dout * gain
    return rstd * (dxh - _mean_last(dxh) - xh * _mean_last(dxh * xh))


def _halo_index(tile, ts):
    return jnp.maximum(tile * (ts // HALO) - 1, 0)


def _mid_fwd(a, g, o, x, ada, wdw, prm, ln1, w_out, gm, ts):
    s = x.shape[0]
    ns = s // ts

    def body(a_ref, g_ref, ah_ref, gh_ref, o_ref, x_ref, ada_ref, wdw_ref, prm_ref, ln_ref, wo_ref, gm_ref,
             x1_ref, ugx, ush):
        i = pl.program_id(0)
        co = _conv_branch(a_ref[...], g_ref[...], ah_ref[...], gh_ref[...], i == 0, ugx, ush, wdw_ref, prm_ref,
                          gm_ref[...], ts)[-1]
        mi = _mix_inputs(o_ref[...], co, prm_ref)[-1]
        mixed = _dot(mi, wo_ref[...])
        r1 = ALPHA * x_ref[...] + (1.0 + ada_ref[2:3, :]) * mixed
        xh, _ = _layernorm_stats(r1)
        x1_ref[...] = xh * ln_ref[0:1, :] + ln_ref[1:2, :]

    row = lambda i: (i, 0)
    full = lambda i: (0, 0)
    halo = lambda i: (_halo_index(i, ts), 0)
    return pl.pallas_call(
        body, name="mid_fwd", grid=(ns,),
        out_shape=jax.ShapeDtypeStruct((s, D_MODEL), F32),
        in_specs=[pl.BlockSpec((ts, 512), row), pl.BlockSpec((ts, 512), row),
                  pl.BlockSpec((HALO, 512), halo), pl.BlockSpec((HALO, 512), halo),
                  pl.BlockSpec((ts, 512), row), pl.BlockSpec((ts, D_MODEL), row),
                  pl.BlockSpec((6, D_MODEL), full), pl.BlockSpec((32, 512), full), pl.BlockSpec((8, 512), full),
                  pl.BlockSpec((2, D_MODEL), full), pl.BlockSpec((D_MODEL, D_MODEL), full),
                  pl.BlockSpec((512, 512), full)],
        out_specs=pl.BlockSpec((ts, D_MODEL), row),
        scratch_shapes=[pltpu.VMEM((ts + HALO, 512), F32), _shift_scratch(ts)],
        compiler_params=_cparams(1),
    )(a, g, a, g, o, x, ada, wdw, prm, ln1, w_out, gm)


def _ffn_fwd(x1, ada, w1, w2, ln2, tgt, ts):
    s = x1.shape[0]
    ns = s // ts
    nf = N_SHARD

    def body(x1_ref, ada_ref, w1_ref, w2_ref, ln_ref, t_ref, dff_ref, dx1_ref, pg_ref, ffacc, u2):
        i = pl.program_id(0)
        f = pl.program_id(1)

        @pl.when((i == 0) & (f == 0))
        def _():
            pg_ref[...] = jnp.zeros_like(pg_ref)

        @pl.when(f == 0)
        def _():
            u2[...] = (x1_ref[...] * (1.0 + ada_ref[4:5, :]) + ada_ref[3:4, :]).astype(BF16)
            ffacc[...] = jnp.zeros_like(ffacc)

        h = _dot(u2[...], w1_ref[0])
        r = jnp.maximum(h, 0.0)
        ffacc[...] += _dot((r * r).astype(BF16), w2_ref[0])

        @pl.when(f == nf - 1)
        def _():
            ff = ffacc[...]
            r2 = ALPHA * x1_ref[...] + (1.0 + ada_ref[5:6, :]) * ff
            xh, rstd = _layernorm_stats(r2)
            yv = xh * ln_ref[0:1, :] + ln_ref[1:2, :]
            err = yv - t_ref[...]
            dy = err * (1.0 / D_MODEL)
            dr2 = _layernorm_bwd(dy, xh, rstd, ln_ref[0:1, :])
            pg_ref[0:1, :] += _rowsum(dy * xh)
            pg_ref[1:2, :] += _rowsum(dy)
            pg_ref[2:3, :] += _rowsum(dr2 * ff)
            pg_ref[3:4, :] += _rowsum(err * err) * (0.5 / D_MODEL)
            dff_ref[...] = ((1.0 + ada_ref[5:6, :]) * dr2).astype(BF16)
            dx1_ref[...] = ALPHA * dr2

    row = lambda i, f: (i, 0)
    full = lambda i, f: (0, 0)
    chunk = lambda i, f: (f, 0, 0)
    return pl.pallas_call(
        body, name="ffn_fwd", grid=(ns, nf),
        out_shape=(jax.ShapeDtypeStruct((s, D_MODEL), BF16), jax.ShapeDtypeStruct((s, D_MODEL), F32),
                   jax.ShapeDtypeStruct((8, D_MODEL), F32)),
        in_specs=[pl.BlockSpec((ts, D_MODEL), row), pl.BlockSpec((6, D_MODEL), full),
                  pl.BlockSpec((1, D_MODEL, FF_CHUNK), chunk), pl.BlockSpec((1, FF_CHUNK, D_MODEL), chunk),
                  pl.BlockSpec((2, D_MODEL), full), pl.BlockSpec((ts, D_MODEL), row)],
        out_specs=(pl.BlockSpec((ts, D_MODEL), row), pl.BlockSpec((ts, D_MODEL), row),
                   pl.BlockSpec((8, D_MODEL), full)),
        scratch_shapes=[pltpu.VMEM((ts, D_MODEL), F32), pltpu.VMEM((ts, D_MODEL), BF16)],
        compiler_params=_cparams(2),
    )(x1, ada, w1, w2, ln2, tgt)


def _ffn_bwd_chunk(f, x1, ada, w1, w2, dff, dx1, ts):
    s = x1.shape[0]
    ns = s // ts

    def body(x1_ref, ada_ref, w1_ref, w2_ref, dff_ref, dx1_in, dx1_out, dw1_ref, dw2_ref, pg_ref):
        i = pl.program_id(0)

        @pl.when(i == 0)
        def _():
            pg_ref[...] = jnp.zeros_like(pg_ref)
            dw1_ref[...] = jnp.zeros_like(dw1_ref)
            dw2_ref[...] = jnp.zeros_like(dw2_ref)

        x1v = x1_ref[...]
        u2 = (x1v * (1.0 + ada_ref[4:5, :]) + ada_ref[3:4, :]).astype(BF16)
        h = _dot(u2, w1_ref[0])
        r = jnp.maximum(h, 0.0)
        hid = (r * r).astype(BF16)
        dffv = dff_ref[...]
        dh = (_dot_nt(dffv, w2_ref[0]) * (2.0 * r)).astype(BF16)
        dw2_ref[0] += _dot_tn(hid, dffv)
        dw1_ref[0] += _dot_tn(u2, dh)
        du2 = _dot_nt(dh, w1_ref[0])
        dx1_out[...] = dx1_in[...] + du2 * (1.0 + ada_ref[4:5, :])
        pg_ref[0:1, :] += _rowsum(du2 * x1v)
        pg_ref[1:2, :] += _rowsum(du2)

    row = lambda i: (i, 0)
    full = lambda i: (0, 0)
    full3 = lambda i: (0, 0, 0)
    chunk = lambda i: (f, 0, 0)
    return pl.pallas_call(
        body, name=f"ffn_bwd_{f}", grid=(ns,),
        out_shape=(jax.ShapeDtypeStruct((s, D_MODEL), F32), jax.ShapeDtypeStruct((1, D_MODEL, FF_CHUNK), F32),
                   jax.ShapeDtypeStruct((1, FF_CHUNK, D_MODEL), F32), jax.ShapeDtypeStruct((8, D_MODEL), F32)),
        in_specs=[pl.BlockSpec((ts, D_MODEL), row), pl.BlockSpec((6, D_MODEL), full),
                  pl.BlockSpec((1, D_MODEL, FF_CHUNK), chunk), pl.BlockSpec((1, FF_CHUNK, D_MODEL), chunk),
                  pl.BlockSpec((ts, D_MODEL), row), pl.BlockSpec((ts, D_MODEL), row)],
        out_specs=(pl.BlockSpec((ts, D_MODEL), row), pl.BlockSpec((1, D_MODEL, FF_CHUNK), full3),
                   pl.BlockSpec((1, FF_CHUNK, D_MODEL), full3), pl.BlockSpec((8, D_MODEL), full)),
        compiler_params=_cparams(1),
    )(x1, ada, w1, w2, dff, dx1)


def _mid_bwd(a, g, o, x, dx1, ada, wdw, prm, ln1, w_out, gm, sel, dw1, dw2, ts):
    s = x.shape[0]
    ns = s // ts

    def body(a_ref, g_ref, ah_ref, gh_ref, o_ref, x_ref, dx1_ref, ada_ref, wdw_ref, prm_ref, ln_ref, wo_ref,
             gm_ref, sel_ref, dw1_hbm, dw2_hbm,
             do_ref, da_ref, dg_ref, dxa_ref, dwo_ref, pgm_ref, pgc_ref, dwdw_ref, land_hbm,
             ugx, dyx, ush, dsh, lsem, ssem, rsem):
        i = pl.program_id(0)
        tile = ns - 1 - i

        def exchange():
            return _scatter_copies((dw1_hbm, dw2_hbm), land_hbm, _STACK_FF, lsem, ssem, rsem)

        @pl.when(i == 0)
        def _():
            exchange().start()
            dwo_ref[...] = jnp.zeros_like(dwo_ref)
            pgm_ref[...] = jnp.zeros_like(pgm_ref)
            pgc_ref[...] = jnp.zeros_like(pgc_ref)
            dwdw_ref[...] = jnp.zeros_like(dwdw_ref)
            dyx[ts:ts + HALO, :] = jnp.zeros((HALO, 512), F32)

        gmv = gm_ref[...]
        av = a_ref[...]
        ov = o_ref[...]
        sg_g, rs, yhat, yn, sg, co = _conv_branch(av, g_ref[...], ah_ref[...], gh_ref[...], tile == 0, ugx, ush,
                                                  wdw_ref, prm_ref, gmv, ts)
        ra, oh, rc, ch, mi = _mix_inputs(ov, co, prm_ref)
        mixed = _dot(mi, wo_ref[...])
        gt1 = 1.0 + ada_ref[2:3, :]
        r1 = ALPHA * x_ref[...] + gt1 * mixed
        xh, rstd = _layernorm_stats(r1)
        dx1 = dx1_ref[...]
        pgm_ref[0:1, :] += _rowsum(dx1 * xh)
        pgm_ref[1:2, :] += _rowsum(dx1)
        dr1 = _layernorm_bwd(dx1, xh, rstd, ln_ref[0:1, :])
        dxa_ref[...] = ALPHA * dr1
        pgm_ref[2:3, :] += _rowsum(dr1 * mixed)
        dmixed = (gt1 * dr1).astype(BF16)
        dmi = _dot_nt(dmixed, wo_ref[...])
        dwo_ref[...] += _dot_tn(mi, dmixed)
        dna = dmi[:, 0:512]
        dnc = dmi[:, 512:1024]
        pgc_ref[3:4, :] += _rowsum(dna * oh)
        doh = dna * prm_ref[3:4, :]
        do = ra * (doh - oh * _mean_last(doh * oh))
        lane = lax.broadcasted_iota(jnp.int32, (ts, 128), 1)
        delta = _dot_f32(do * ov, sel_ref[...])
        for h in range(N_HEADS):
            own, at, _ = _aug_masks(lane, h)
            hi, mid, lo = _pieces(-delta[:, h:h + 1])
            dop = do[:, 128 * (h // 2):128 * (h // 2) + 128]
            do_ref[:, 128 * h:128 * h + 128] = jnp.where(
                own, dop, jnp.where(at(0), hi, jnp.where(at(1), mid, jnp.where(at(2), lo, 0.0)))).astype(BF16)
        pgc_ref[4:5, :] += _rowsum(dnc * ch)
        dch = dnc * prm_ref[4:5, :]
        dco = rc * (dch - ch * _mean_last(dch * ch))
        dyn = dco * (sg * (1.0 + yn * (1.0 - sg)))
        pgc_ref[1:2, :] += _rowsum(dyn * yhat)
        pgc_ref[2:3, :] += _rowsum(dyn)
        dyh = dyn * prm_ref[1:2, :]
        dy = rs * (dyh - _dot_f32(dyh, gmv) - yhat * _dot_f32(dyh * yhat, gmv))
        pgc_ref[0:1, :] += _rowsum(dy)
        dyr = dy.astype(BF16).astype(F32)
        dyx[0:ts, :] = dyr
        _fill_shifts(dyx, dsh, ts)
        dug = jnp.zeros((ts, CONV_W), F32)
        for kk in range(CONV_K):
            off = HALO - (CONV_K - 1) + kk
            dwdw_ref[kk:kk + 1, :] += _rowsum(dyr * _rows_at(ugx, ush, off, ts))
            back = CONV_K - 1 - kk
            dug = dug + wdw_ref[kk:kk + 1, :] * _rows_at(dyx, dsh, back, ts)
        dyx[ts:ts + HALO, :] = dyr[0:HALO, :]
        da_ref[...] = (dug * sg_g).astype(BF16)
        dg_ref[...] = (dug * av * sg_g * (1.0 - sg_g)).astype(BF16)

        @pl.when(i == ns - 1)
        def _():
            exchange().wait()

    row = lambda i: (ns - 1 - i, 0)
    full = lambda i: (0, 0)
    halo = lambda i: (_halo_index(ns - 1 - i, ts), 0)
    return pl.pallas_call(
        body, name="mid_bwd", grid=(ns,),
        out_shape=(jax.ShapeDtypeStruct((s, N_HEADS * 128), BF16), jax.ShapeDtypeStruct((s, 512), BF16),
                   jax.ShapeDtypeStruct((s, 512), BF16), jax.ShapeDtypeStruct((s, D_MODEL), F32),
                   jax.ShapeDtypeStruct((D_MODEL, D_MODEL), F32),
                   jax.ShapeDtypeStruct((8, D_MODEL), F32), jax.ShapeDtypeStruct((8, 512), F32),
                   jax.ShapeDtypeStruct((32, 512), F32), jax.ShapeDtypeStruct((N_SHARD, 2 * FF_CHUNK, D_MODEL), F32)),
        in_specs=[pl.BlockSpec((ts, 512), row), pl.BlockSpec((ts, 512), row),
                  pl.BlockSpec((HALO, 512), halo), pl.BlockSpec((HALO, 512), halo),
                  pl.BlockSpec((ts, 512), row), pl.BlockSpec((ts, D_MODEL), row), pl.BlockSpec((ts, D_MODEL), row),
                  pl.BlockSpec((6, D_MODEL), full), pl.BlockSpec((32, 512), full), pl.BlockSpec((8, 512), full),
                  pl.BlockSpec((2, D_MODEL), full), pl.BlockSpec((D_MODEL, D_MODEL), full),
                  pl.BlockSpec((512, 512), full), pl.BlockSpec((512, 128), full), ANY, ANY],
        out_specs=(pl.BlockSpec((ts, N_HEADS * 128), row), pl.BlockSpec((ts, 512), row), pl.BlockSpec((ts, 512), row),
                   pl.BlockSpec((ts, D_MODEL), row),
                   pl.BlockSpec((D_MODEL, D_MODEL), full), pl.BlockSpec((8, D_MODEL), full),
                   pl.BlockSpec((8, 512), full), pl.BlockSpec((32, 512), full), ANY),
        scratch_shapes=[pltpu.VMEM((ts + HALO, 512), F32), pltpu.VMEM((ts + HALO, 512), F32), _shift_scratch(ts),
                        _shift_scratch(ts)] + _copy_sems(2),
        compiler_params=_cparams(1),
    )(a, g, a, g, o, x, dx1, ada, wdw, prm, ln1, w_out, gm, sel, dw1, dw2)


def _attn_bwd(qb, ka, va, doa, dwo, tk):
    s = qb.shape[0]
    nk = s // tk
    tq = tk

    def body(q_ref, do_ref, k_ref, v_ref, dwo_hbm, dq_ref, rs_ref, dk_ref, dv_ref, cs_ref, land_hbm,
             lsem, ssem, rsem):
        pair = pl.program_id(0)
        j = pl.program_id(1)

        def exchange():
            return _scatter_copies((dwo_hbm,), land_hbm, _STACK_OUT, lsem, ssem, rsem)

        @pl.when((pair == 0) & (j == 0))
        def _():
            exchange().start()

        @pl.when(j == 0)
        def _():
            dq_ref[...] = jnp.zeros_like(dq_ref)

        @pl.when((pair == 0) & (j == 0))
        def _():
            rs_ref[...] = jnp.zeros_like(rs_ref)
            cs_ref[...] = jnp.zeros_like(cs_ref)

        lane = lax.broadcasted_iota(jnp.int32, (tk, 128), 1)
        low = lane < HEAD_DIM
        t_off = lax.broadcasted_iota(jnp.int32, (tq, tk), 0)
        s_off = lax.broadcasted_iota(jnp.int32, (tq, tk), 1)

        def block(i, carry, diagonal):
            rows_q = pl.ds(pl.multiple_of(i * tq, tq), tq)
            dq_h, out = [], []
            for hh in range(2):
                cols = slice(128 * hh, 128 * hh + 128)
                dk_acc, dv_acc = carry[2 * hh:2 * hh + 2]
                qh = q_ref[rows_q, cols]
                dh = do_ref[rows_q, cols]
                pr = jnp.exp(_dot_nt(qh, k_ref[:, cols]))
                if diagonal:
                    pr = jnp.where(s_off <= t_off, pr, 0.0)
                ds = (pr * _dot_nt(dh, v_ref[:, cols])).astype(BF16)
                dq_h.append(_dot(ds, k_ref[:, cols]))
                out += [dk_acc + _dot_tn(ds, qh), dv_acc + _dot_tn(pr.astype(BF16), dh)]
            dq_ref[rows_q, :] += jnp.where(low, dq_h[0], dq_h[1])
            rs_ref[rows_q, :] += (jnp.where(lane == 2 * pair, dq_h[0][:, HEAD_DIM:HEAD_DIM + 1], 0.0)
                                  + jnp.where(lane == 2 * pair + 1, dq_h[1][:, 0:1], 0.0))
            return tuple(out)

        first = block(j, (jnp.zeros((tk, 128), F32),) * 4, diagonal=True)
        dk0, dv0, dk1, dv1 = lax.fori_loop(j + 1, nk, functools.partial(block, diagonal=False), first)
        dk_ref[...] = jnp.where(low, dk0, dk1).astype(BF16)
        dv_ref[...] = jnp.where(low, dv0, dv1).astype(BF16)
        rows_k = pl.ds(pl.multiple_of(j * tk, tk), tk)
        cs_ref[rows_k, :] += (jnp.where(lane == 2 * pair, dk0[:, HEAD_DIM + 3:HEAD_DIM + 4], 0.0)
                              + jnp.where(lane == 2 * pair + 1, dk1[:, 3:4], 0.0))

        @pl.when((pair == N_PAIRS - 1) & (j == nk - 1))
        def _():
            exchange().wait()

    whole = lambda p, j: (0, 0)
    return pl.pallas_call(
        body, name="attn_bwd", grid=(N_PAIRS, nk),
        out_shape=(jax.ShapeDtypeStruct((s, 512), F32), jax.ShapeDtypeStruct((s, 128), F32),
                   jax.ShapeDtypeStruct((s, 512), BF16), jax.ShapeDtypeStruct((s, 512), BF16),
                   jax.ShapeDtypeStruct((s, 128), F32), jax.ShapeDtypeStruct(dwo.shape, F32)),
        in_specs=[pl.BlockSpec((s, 256), lambda p, j: (0, p)), pl.BlockSpec((s, 256), lambda p, j: (0, p)),
                  pl.BlockSpec((tk, 256), lambda p, j: (j, p)), pl.BlockSpec((tk, 256), lambda p, j: (j, p)), ANY],
        out_specs=(pl.BlockSpec((s, 128), lambda p, j: (0, p)), pl.BlockSpec((s, 128), whole),
                   pl.BlockSpec((tk, 128), lambda p, j: (j, p)), pl.BlockSpec((tk, 128), lambda p, j: (j, p)),
                   pl.BlockSpec((s, 128), whole), ANY),
        scratch_shapes=_copy_sems(1),
        compiler_params=_cparams(2),
    )(qb, doa, ka, va, dwo)


def _inproj_bwd(x, ada, w_p, dq, dk, dv, da, dg, dfc, drs, logf, dxa, ts):
    s = x.shape[0]
    ns = s // ts

    def body(x_ref, ada_ref, w_ref, dq_ref, dk_ref, dv_ref, da_ref, dg_ref, dfc_ref, drs_ref, lf_ref, dxa_ref,
             gx_ref, dw_hbm, pgi_ref, dbf_ref, carry, dw_vm, sem):
        i = pl.program_id(0)

        @pl.when(i == 0)
        def _():
            carry[...] = jnp.zeros_like(carry)
            dw_vm[...] = jnp.zeros_like(dw_vm)
            pgi_ref[...] = jnp.zeros_like(pgi_ref)
            dbf_ref[...] = jnp.zeros_like(dbf_ref)

        r = lax.broadcasted_iota(jnp.int32, (ts, ts), 0)
        cc = lax.broadcasted_iota(jnp.int32, (ts, ts), 1)
        tri = (cc >= r).astype(BF16)
        dlogf = carry[...] + _tri_dot(tri, drs_ref[...] - dfc_ref[...])
        carry[...] = dlogf[0:1, :]
        lane = lax.broadcasted_iota(jnp.int32, (ts, 128), 1)
        dz = jnp.where(lane < N_HEADS, dlogf * (1.0 - jnp.exp(lf_ref[...])), 0.0)
        dbf_ref[0:1, :] += _rowsum(dz)
        dproj = jnp.concatenate(
            [(dq_ref[...] * (HEAD_DIM ** -0.5)).astype(BF16), dk_ref[...], dv_ref[...], da_ref[...], dg_ref[...],
             dz.astype(BF16)], axis=-1)
        xv = x_ref[...]
        sc1 = 1.0 + ada_ref[1:2, :]
        u = (xv * sc1 + ada_ref[0:1, :]).astype(BF16)
        du = _dot_nt(dproj, w_ref[...])
        dw_vm[...] += _dot_tn(u, dproj)
        gx_ref[...] = dxa_ref[...] + du * sc1
        pgi_ref[0:1, :] += _rowsum(du * xv)
        pgi_ref[1:2, :] += _rowsum(du)

        @pl.when(i == ns - 1)
        def _():
            cp = pltpu.make_async_copy(dw_vm, dw_hbm, sem.at[0])
            cp.start()
            cp.wait()

    row = lambda i: (ns - 1 - i, 0)
    full = lambda i: (0, 0)
    return pl.pallas_call(
        body, name="inproj_bwd", grid=(ns,),
        out_shape=(jax.ShapeDtypeStruct((s, D_MODEL), F32), jax.ShapeDtypeStruct((D_MODEL, N_IN_PAD), F32),
                   jax.ShapeDtypeStruct((8, D_MODEL), F32), jax.ShapeDtypeStruct((8, 128), F32)),
        in_specs=[pl.BlockSpec((ts, D_MODEL), row), pl.BlockSpec((6, D_MODEL), full),
                  pl.BlockSpec((D_MODEL, N_IN_PAD), full)]
        + [pl.BlockSpec((ts, 512), row)] * 5 + [pl.BlockSpec((ts, 128), row)] * 3
        + [pl.BlockSpec((ts, D_MODEL), row)],
        out_specs=(pl.BlockSpec((ts, D_MODEL), row), ANY, pl.BlockSpec((8, D_MODEL), full),
                   pl.BlockSpec((8, 128), full)),
        scratch_shapes=[pltpu.VMEM((1, 128), F32), pltpu.VMEM((D_MODEL, N_IN_PAD), F32),
                        pltpu.SemaphoreType.DMA((1,))],
        compiler_params=_cparams(1),
    )(x, ada, w_p, dq, dk, dv, da, dg, dfc, drs, logf, dxa)


def _small_reduce(packed):
    def body(p_ref, sum_ref, all_ref, ssem, rsem):
        x, y, c = _position()
        me = 4 * x + 2 * y + c
        all_ref[me] = p_ref[...]
        sends = []
        for k in range(1, 8):
            peer = (x ^ ((k >> 2) & 1), y ^ ((k >> 1) & 1), c ^ (k & 1))
            cp = pltpu.make_async_remote_copy(
                src_ref=p_ref, dst_ref=all_ref.at[me], send_sem=ssem.at[k], recv_sem=rsem.at[k],
                device_id=peer, device_id_type=MESH)
            cp.start()
            sends.append(cp)
        for k in range(1, 8):
            pltpu.make_async_remote_copy(
                src_ref=p_ref, dst_ref=all_ref.at[me ^ k], send_sem=ssem.at[k], recv_sem=rsem.at[k],
                device_id=(x, y, c), device_id_type=MESH).wait_recv()
        for cp in sends:
            cp.wait_send()
        total = all_ref[0]
        for dev in range(1, 8):
            total = total + all_ref[dev]
        sum_ref[...] = total
        loss = jnp.sum(total[SMALL_ROWS - 1:SMALL_ROWS, :], axis=-1, keepdims=True)
        sum_ref[SMALL_ROWS - 1:SMALL_ROWS, :] = jnp.broadcast_to(loss, (1, D_MODEL))

    vm = pl.BlockSpec(memory_space=pltpu.VMEM)
    return pl.pallas_call(
        body, name="small_reduce",
        out_shape=(jax.ShapeDtypeStruct((SMALL_ROWS, D_MODEL), F32), jax.ShapeDtypeStruct((8, SMALL_ROWS, D_MODEL), F32)),
        in_specs=[vm], out_specs=(vm, vm),
        scratch_shapes=[pltpu.SemaphoreType.DMA((8,)), pltpu.SemaphoreType.DMA((8,))],
        compiler_params=pltpu.CompilerParams(vmem_limit_bytes=VMEM_LIMIT),
    )(packed)


def _adam_math(gv, wv, mv, vv):
    m_new = B1 * mv + (1.0 - B1) * gv
    v_new = B2 * vv + (1.0 - B2) * (gv * gv)
    m_hat = m_new / (1.0 - B1 ** STEP)
    v_hat = v_new / (1.0 - B2 ** STEP)
    delta = -LR * (m_hat / (jnp.sqrt(v_hat) + ADAM_EPS) + WD * wv)
    return delta, m_new, v_new


def _adamw(gv, wv, mv, vv, name):
    rows, cols = gv.shape
    tr = rows
    for cand in (256, 128, 64, 32, 16, 8):
        if rows % cand == 0 and rows > cand:
            tr = cand
            break

    def body(g_ref, w_ref, m_ref, v_ref, d_ref, mo_ref, vo_ref):
        d_ref[...], mo_ref[...], vo_ref[...] = _adam_math(g_ref[...], w_ref[...], m_ref[...], v_ref[...])

    spec = pl.BlockSpec((tr, cols), lambda i: (i, 0))
    return pl.pallas_call(
        body, name=name, grid=(rows // tr,),
        out_shape=(jax.ShapeDtypeStruct((rows, cols), F32),) * 3,
        in_specs=[spec] * 4, out_specs=(spec,) * 3,
        compiler_params=_cparams(1),
    )(gv, wv, mv, vv)


def _w_ada_update(sct, dd, wv, mv, vv):
    rows, cols = wv.shape
    tr = 128

    def body(s_ref, d_ref, w_ref, m_ref, v_ref, g_ref, dl_ref, mo_ref, vo_ref):
        sv = s_ref[...]
        dv = d_ref[...]
        gv = sv[:, 0:1] * dv[0:1, :]
        for b in range(1, 8):
            gv = gv + sv[:, b:b + 1] * dv[b:b + 1, :]
        g_ref[...] = gv
        dl_ref[...], mo_ref[...], vo_ref[...] = _adam_math(gv, w_ref[...], m_ref[...], v_ref[...])

    spec = pl.BlockSpec((tr, cols), lambda i: (i, 0))
    return pl.pallas_call(
        body, name="w_ada_update", grid=(rows // tr,),
        out_shape=(jax.ShapeDtypeStruct((rows, cols), F32),) * 4,
        in_specs=[pl.BlockSpec((tr, 8), lambda i: (i, 0)), pl.BlockSpec((8, cols), lambda i: (0, 0))] + [spec] * 3,
        out_specs=(spec,) * 4,
        compiler_params=_cparams(1),
    )(sct, dd, wv, mv, vv)


def _grad_exchange(part):
    rows = part.shape[1]

    def body(p_hbm, land_hbm, lsem, ssem, rsem):
        cps = _scatter_copies((p_hbm,), land_hbm, ((0, rows),), lsem, ssem, rsem)
        cps.start()
        cps.wait()

    return pl.pallas_call(
        body, name="grad_exchange",
        out_shape=jax.ShapeDtypeStruct(part.shape, part.dtype),
        in_specs=[ANY], out_specs=ANY,
        scratch_shapes=_copy_sems(1),
    )(part)


def _sum_chips(land_in, land_out, land_ff):
    tr = 256
    n_in, n_out = 768 // tr, 256 // tr

    def body(in_ref, out_ref, ff_ref, s_ref):
        i = pl.program_id(0)

        def total(ref):
            s_ref[...] = ((ref[0].astype(F32) + ref[1].astype(F32)) + ref[2].astype(F32)) + ref[3].astype(F32)

        pl.when(i < n_in)(lambda: total(in_ref))
        pl.when((i >= n_in) & (i < n_in + n_out))(lambda: total(out_ref))
        pl.when(i >= n_in + n_out)(lambda: total(ff_ref))

    return pl.pallas_call(
        body, name="sum_chips", grid=(STACK_ROWS // tr,),
        out_shape=jax.ShapeDtypeStruct((STACK_ROWS, D_MODEL), F32),
        in_specs=[pl.BlockSpec((N_SHARD, tr, D_MODEL), lambda i: (0, jnp.minimum(i, n_in - 1), 0)),
                  pl.BlockSpec((N_SHARD, tr, D_MODEL), lambda i: (0, jnp.clip(i - n_in, 0, n_out - 1), 0)),
                  pl.BlockSpec((N_SHARD, tr, D_MODEL), lambda i: (0, jnp.maximum(i - n_in - n_out, 0), 0))],
        out_specs=pl.BlockSpec((tr, D_MODEL), lambda i: (i, 0)),
        compiler_params=_cparams(1),
    )(land_in, land_out, land_ff)


def _core_swap(part):
    def body(p_ref, o_ref, ssem, rsem):
        x, y, c = _position()
        cp = pltpu.make_async_remote_copy(src_ref=p_ref, dst_ref=o_ref, send_sem=ssem, recv_sem=rsem,
                                          device_id=(x, y, 1 - c), device_id_type=MESH)
        cp.start()
        cp.wait()

    return pl.pallas_call(
        body, name="core_swap",
        out_shape=jax.ShapeDtypeStruct(part.shape, part.dtype),
        in_specs=[ANY], out_specs=ANY,
        scratch_shapes=[pltpu.SemaphoreType.DMA, pltpu.SemaphoreType.DMA],
    )(part)


def _add_pair(mine, other):
    tr = 256

    def body(a_ref, b_ref, o_ref):
        o_ref[...] = a_ref[...] + b_ref[...]

    spec = pl.BlockSpec((tr, D_MODEL), lambda i: (i, 0))
    return pl.pallas_call(
        body, name="add_pair", grid=(STACK_ROWS // tr,),
        out_shape=jax.ShapeDtypeStruct((STACK_ROWS, D_MODEL), F32),
        in_specs=[spec, spec], out_specs=spec,
        compiler_params=_cparams(1),
    )(mine, other)


def _pad_lanes(v, width=D_MODEL):
    v = v.reshape(1, -1)
    return jnp.pad(v, ((0, 0), (0, width - v.shape[1])))


def _pack_small(b_ada, ln1_g, ln1_b, ln2_g, ln2_b, b_dw, gn_g, gn_b, g_attn, g_conv, b_forget, w_dw_full, last):
    rows = [b_ada.reshape(6, D_MODEL)] + [_pad_lanes(v) for v in
                                          (ln1_g, ln1_b, ln2_g, ln2_b, b_dw, gn_g, gn_b, g_attn, g_conv, b_forget)]
    rows.append(jnp.pad(w_dw_full.reshape(CONV_K, -1), ((0, 0), (0, D_MODEL - w_dw_full.reshape(CONV_K, -1).shape[1]))))
    rows.append(_pad_lanes(last))
    return jnp.concatenate(rows, axis=0)


def _unpack_small(p):
    return dict(b_ada=p[0:6].reshape(1, 6 * D_MODEL), ln1_g=p[6:7], ln1_b=p[7:8], ln2_g=p[8:9], ln2_b=p[9:10],
                b_dw=p[10:11, :512], gn_g=p[11:12, :512], gn_b=p[12:13, :512], g_attn_out=p[13:14, :512],
                g_conv_out=p[14:15, :512], b_forget=p[15:16, :N_HEADS])


def kernel(x, c, w_ada, b_ada, w_in, b_forget, w_dw, b_dw, gn_g, gn_b, g_attn_out, g_conv_out, w_out, ln1_g, ln1_b, w_ff1, w_ff2, ln2_g, ln2_b, loss_target, m_w_ada, m_b_ada, m_w_in, m_b_forget, m_w_dw, m_b_dw, m_gn_g, m_gn_b, m_g_attn_out, m_g_conv_out, m_w_out, m_ln1_g, m_ln1_b, m_w_ff1, m_w_ff2, m_ln2_g, m_ln2_b, v_w_ada, v_b_ada, v_w_in, v_b_forget, v_w_dw, v_b_dw, v_gn_g, v_gn_b, v_g_attn_out, v_g_conv_out, v_w_out, v_ln1_g, v_ln1_b, v_w_ff1, v_w_ff2, v_ln2_g, v_ln2_b):
    seq = x.shape[1]
    ts = min(512, seq // 2)
    tq = min(512, seq // 2)
    ts_mid = min(256, seq // 2)
    q_idx = 2 * lax.axis_index("x") + lax.axis_index("y")
    xs = x[0]
    tgt = loss_target[0]

    sc_all, ada = _ada_fwd(c, w_ada[0], b_ada)
    w_in_sh = jnp.pad(w_in[0], ((0, 0), (0, IN_SHARD_PAD - IN_SHARD))).astype(BF16)
    wdw_rows = jnp.pad(w_dw[0, :, 0, :], ((0, 1), (0, 0)))
    win_all, wdw_all = _weight_gather([w_in_sh, wdw_rows])
    w_in_full = jnp.transpose(win_all[:, :, :IN_SHARD], (1, 0, 2)).reshape(D_MODEL, N_IN)
    w_p = jnp.concatenate([w_in_full[:, 0:1536], w_in_full[:, 1544:2568], w_in_full[:, 1536:1544],
                           jnp.zeros((D_MODEL, 120), BF16)], axis=1)
    bf = _pad_lanes(b_forget, 128)
    wdw_full = lax.reduce_precision(jnp.transpose(wdw_all, (1, 0, 2)).reshape(32, 512), 8, 7)

    prm = jnp.concatenate([b_dw, gn_g, gn_b, g_attn_out, g_conv_out, jnp.zeros((3, 512), F32)], axis=0)
    ln1 = jnp.concatenate([ln1_g, ln1_b], axis=0)
    ln2 = jnp.concatenate([ln2_g, ln2_b], axis=0)
    ch = jnp.arange(512)
    gm = ((ch[:, None] // HEAD_DIM == ch[None, :] // HEAD_DIM).astype(F32) / HEAD_DIM).astype(BF16)
    sel = (ch[:, None] // HEAD_DIM == jnp.arange(128)[None, :]).astype(BF16)

    qa, ka, va, a, g, logf = _inproj_fwd(xs, ada, w_p, bf, ts)
    o, qb, (wout_all, w1_all, w2_all) = _attn_fwd(
        qa, ka, va, [w_out[0].astype(BF16), w_ff1[0].astype(BF16), w_ff2[0].astype(BF16)], tq)
    w_out_full = wout_all.reshape(D_MODEL, D_MODEL)
    x1 = _mid_fwd(a, g, o, xs, ada, wdw_full, prm, ln1, w_out_full, gm, ts_mid)
    dff, dx1, pg_f = _ffn_fwd(x1, ada, w1_all, w2_all, ln2, tgt, ts)

    dw1, dw2, pg_b = [], [], jnp.zeros((8, D_MODEL), F32)
    for f in range(N_SHARD):
        dx1, dw1_f, dw2_f, pg_bf = _ffn_bwd_chunk(f, x1, ada, w1_all, w2_all, dff, dx1, ts)
        dw1.append(dw1_f)
        dw2.append(dw2_f)
        pg_b = pg_b + pg_bf
    dw1 = jnp.concatenate(dw1, axis=0)
    dw2 = jnp.concatenate(dw2, axis=0)
    doa, da, dg, dxa, dwo, pgm, pgc, dwdw, land_ff = _mid_bwd(
        a, g, o, xs, dx1, ada, wdw_full, prm, ln1, w_out_full, gm, sel, dw1, dw2, ts_mid)
    dq, drs, dk, dv, dfc, land_out = _attn_bwd(qb, ka, va, doa, dwo.reshape(N_SHARD, 256, D_MODEL), tq)
    gx, dwp, pgi, dbf = _inproj_bwd(xs, ada, w_p, dq, dk, dv, da, dg, dfc, drs, logf, dxa, ts)

    d_ada = jnp.concatenate([pgi[1:2], pgi[0:1], pgm[2:3], pg_b[1:2], pg_b[0:1], pg_f[2:3]], axis=0)
    packed = _pack_small(d_ada, pgm[0:1], pgm[1:2], pg_f[0:1], pg_f[1:2], pgc[0:1], pgc[1:2], pgc[2:3], pgc[3:4],
                         pgc[4:5], dbf[0:1, :N_HEADS], dwdw[0:CONV_K], pg_f[3:4])
    small_sum, small_all = _small_reduce(packed)
    loss = small_sum[SMALL_ROWS - 1, 0]
    gsm = _unpack_small(small_sum)
    g_wdw = lax.dynamic_slice(small_sum[16:16 + CONV_K, :512], (0, q_idx * 128), (CONV_K, 128))

    zrow = jnp.zeros((CONV_K + 1, D_MODEL), F32)
    w_small = _pack_small(b_ada, ln1_g, ln1_b, ln2_g, ln2_b, b_dw, gn_g, gn_b, g_attn_out,
                          g_conv_out, b_forget, zrow[:CONV_K, :512], zrow[0])
    m_small = _pack_small(m_b_ada, m_ln1_g, m_ln1_b, m_ln2_g, m_ln2_b, m_b_dw, m_gn_g, m_gn_b, m_g_attn_out,
                          m_g_conv_out, m_b_forget, zrow[:CONV_K, :512], zrow[0])
    v_small = _pack_small(v_b_ada, v_ln1_g, v_ln1_b, v_ln2_g, v_ln2_b, v_b_dw, v_gn_g, v_gn_b, v_g_attn_out,
                          v_g_conv_out, v_b_forget, zrow[:CONV_K, :512], zrow[0])
    d_small, mn_small, vn_small = (_unpack_small(t) for t in _adamw(small_sum, w_small, m_small, v_small, "adamw_small"))
    d_wdw, mn_wdw, vn_wdw = _adamw(g_wdw, w_dw[0, :, 0, :], m_w_dw[0, :, 0, :], v_w_dw[0, :, 0, :], "adamw_wdw")

    dd = lax.dynamic_slice(small_all[:, 0:6, :].reshape(8, 6 * D_MODEL), (0, q_idx * 1536), (8, 1536))
    g_wada, d_wada, mn_wada, vn_wada = _w_ada_update(sc_all.T, dd, w_ada[0], m_w_ada[0], v_w_ada[0])

    dw_in_cols = jnp.concatenate([dwp[:, 0:1536], dwp[:, 2560:2568], dwp[:, 1536:2560]], axis=1)
    dw_in_sh = jnp.pad(jnp.transpose(dw_in_cols.reshape(D_MODEL, N_SHARD, IN_SHARD), (1, 0, 2)),
                       ((0, 0), (0, 0), (0, IN_SHARD_PAD - IN_SHARD))).reshape(N_SHARD, 768, D_MODEL)
    part = _sum_chips(_grad_exchange(dw_in_sh.astype(BF16)), land_out, land_ff)
    total = _add_pair(part, _core_swap(part))
    g_win = total[0:768].reshape(D_MODEL, IN_SHARD_PAD)[:, :IN_SHARD]
    g_wout = total[768:1024]
    g_w1 = total[1024:2048]
    g_w2 = total[2048:3072]
    d_win, mn_win, vn_win = _adamw(g_win, w_in[0], m_w_in[0], v_w_in[0], "adamw_w_in")
    d_wout, mn_wout, vn_wout = _adamw(g_wout, w_out[0], m_w_out[0], v_w_out[0], "adamw_w_out")
    d_w1, mn_w1, vn_w1 = _adamw(g_w1, w_ff1[0], m_w_ff1[0], v_w_ff1[0], "adamw_w_ff1")
    d_w2, mn_w2, vn_w2 = _adamw(g_w2, w_ff2[0], m_w_ff2[0], v_w_ff2[0], "adamw_w_ff2")

    def group(wada, sm, win, wdw, wout, w1, w2):
        return (wada[None], sm["b_ada"], win[None], sm["b_forget"], wdw[None, :, None, :], sm["b_dw"], sm["gn_g"],
                sm["gn_b"], sm["g_attn_out"], sm["g_conv_out"], wout[None], sm["ln1_g"], sm["ln1_b"], w1[None],
                w2[None], sm["ln2_g"], sm["ln2_b"])

    return ((loss, gx[None])
            + group(g_wada, gsm, g_win, g_wdw, g_wout, g_w1, g_w2)
            + group(d_wada, d_small, d_win, d_wdw, d_wout, d_w1, d_w2)
            + group(mn_wada, mn_small, mn_win, mn_wdw, mn_wout, mn_w1, mn_w2)
            + group(vn_wada, vn_small, vn_win, vn_wdw, vn_wout, vn_w1, vn_w2))
```

```python
import functools

import jax
import jax.numpy as jnp
from jax import lax
from jax.experimental import pallas as pl
from jax.experimental.pallas import tpu as pltpu

F32 = jnp.float32
BF16 = jnp.bfloat16
MESH = pl.DeviceIdType.MESH
ANY = pl.BlockSpec(memory_space=pl.ANY)

D_MODEL = 1024
HEAD_DIM = 64
ATTN_W = 512
CONV_W = 512
N_HEADS = 8
N_PAIRS = 4
CONV_K = 31
HALO = 32
D_FF = 4096
N_SHARD = 4
FF_CHUNK = D_FF // N_SHARD
N_IN = 2568
IN_SHARD = N_IN // N_SHARD
IN_SHARD_PAD = 768
N_IN_PAD = 5 * 512 + 128
LN_EPS = 1e-5
ALPHA = 2.0 ** 0.25
LR, B1, B2, ADAM_EPS, WD, STEP = 0.001, 0.9, 0.999, 1e-08, 0.01, 10
VMEM_LIMIT = 56 * 1024 * 1024
SMALL_ROWS = 48
STACK_ROWS = 768 + 256 + 1024 + 1024
_STACK_OUT = ((0, 256),)
_STACK_FF = ((0, 1024), (1024, 1024))


def _cparams(n_axes):
    return pltpu.CompilerParams(dimension_semantics=("arbitrary",) * n_axes, vmem_limit_bytes=VMEM_LIMIT)


def _dot(a, b):
    return jnp.dot(a, b, preferred_element_type=F32)


def _dot_nt(a, b):
    return lax.dot_general(a, b, (((1,), (1,)), ((), ())), preferred_element_type=F32)


def _dot_tn(a, b):
    return lax.dot_general(a, b, (((0,), (0,)), ((), ())), preferred_element_type=F32)


def _dot_f32(a, b):
    hi, mid, lo = _split3(a)
    return _dot(hi, b) + _dot(mid, b) + _dot(lo, b)


def _split3(x):
    hi = x.astype(BF16)
    r = x - hi.astype(F32)
    mid = r.astype(BF16)
    lo = (r - mid.astype(F32)).astype(BF16)
    return hi, mid, lo


def _tri_dot(tri, x):
    hi, mid, lo = _split3(x)
    return _dot(tri, hi) + _dot(tri, mid) + _dot(tri, lo)


def _rowsum(x):
    return jnp.sum(x, axis=0, keepdims=True)


def _mean_last(x):
    return jnp.mean(x, axis=-1, keepdims=True)


def _position():
    x, y, c = lax.axis_index("x"), lax.axis_index("y"), lax.axis_index("c")
    return x, y, c


def _ada_fwd(c_row, w_ada, b_ada, shards):
    n_col = w_ada.shape[1]
    n = len(shards)

    def body(c_ref, w_ref, b_ref, *rest):
        sh_in, (sc_ref, ada_ref), sh_out = rest[:n], rest[n:n + 2], rest[n + 2:2 * n + 2]
        call_ref, part_ref, pall_ref, s1, r1, s2, r2, lsem, ssem, rsem = rest[2 * n + 2:]
        x, y, c = _position()
        me = 4 * x + 2 * y + c
        q = 2 * x + y
        gather = _gather_copies(sh_in, sh_out, lsem, ssem, rsem)
        gather.start()
        call_ref[me] = jnp.broadcast_to(c_ref[...], (8, D_MODEL))

        def c_copy(k):
            peer = (x ^ ((k >> 2) & 1), y ^ ((k >> 1) & 1), c ^ (k & 1))
            return pltpu.make_async_remote_copy(
                src_ref=call_ref.at[me], dst_ref=call_ref.at[me], send_sem=s1.at[k], recv_sem=r1.at[k],
                device_id=peer, device_id_type=MESH)

        def c_recv(k):
            src = me ^ k
            return pltpu.make_async_remote_copy(
                src_ref=call_ref.at[src], dst_ref=call_ref.at[src], send_sem=s1.at[k], recv_sem=r1.at[k],
                device_id=(x, y, c), device_id_type=MESH)

        sends = [c_copy(k) for k in range(1, 8)]
        for cp in sends:
            cp.start()
        for k in range(1, 8):
            c_recv(k).wait_recv()
        for cp in sends:
            cp.wait_send()

        row = lax.broadcasted_iota(jnp.int32, (8, D_MODEL), 0)
        c_all = jnp.zeros((8, D_MODEL), F32)
        for j in range(8):
            c_all = jnp.where(row == j, call_ref[j], c_all)
        sc_all = c_all * jax.nn.sigmoid(c_all)
        sc_ref[...] = sc_all
        b_slice = b_ref[:, pl.ds(pl.multiple_of(q * n_col, 128), n_col)]
        part = _dot(sc_all.astype(BF16), w_ref[...].astype(BF16)) + b_slice
        part_ref[...] = part
        pall_ref[q] = part

        def p_copy(j):
            peer = (x ^ ((j >> 1) & 1), y ^ (j & 1), c)
            return pltpu.make_async_remote_copy(
                src_ref=part_ref, dst_ref=pall_ref.at[q], send_sem=s2.at[j], recv_sem=r2.at[j],
                device_id=peer, device_id_type=MESH)

        def p_recv(j):
            src_q = q ^ j
            return pltpu.make_async_remote_copy(
                src_ref=part_ref, dst_ref=pall_ref.at[src_q], send_sem=s2.at[j], recv_sem=r2.at[j],
                device_id=(x, y, c), device_id_type=MESH)

        sends2 = [p_copy(j) for j in range(1, 4)]
        for cp in sends2:
            cp.start()
        for j in range(1, 4):
            p_recv(j).wait_recv()
        for cp in sends2:
            cp.wait_send()
        for qq in range(N_SHARD):
            ada_ref[qq] = pall_ref[qq, pl.ds(me, 1), :]
        gather.wait()

    vm = pl.BlockSpec(memory_space=pltpu.VMEM)
    res = pl.pallas_call(
        body, name="ada_fwd",
        out_shape=(jax.ShapeDtypeStruct((8, D_MODEL), F32), jax.ShapeDtypeStruct((N_SHARD, 1, n_col), F32))
        + tuple(jax.ShapeDtypeStruct((N_SHARD,) + w.shape, w.dtype) for w in shards),
        in_specs=[vm, vm, vm] + [ANY] * n, out_specs=(vm, vm) + (ANY,) * n,
        scratch_shapes=[pltpu.VMEM((8, 8, D_MODEL), F32), pltpu.VMEM((8, n_col), F32),
                        pltpu.VMEM((N_SHARD, 8, n_col), F32),
                        pltpu.SemaphoreType.DMA((8,)), pltpu.SemaphoreType.DMA((8,)),
                        pltpu.SemaphoreType.DMA((4,)), pltpu.SemaphoreType.DMA((4,))] + _copy_sems(n),
        compiler_params=pltpu.CompilerParams(vmem_limit_bytes=VMEM_LIMIT),
    )(c_row, w_ada, b_ada, *shards)
    return res[0], res[1].reshape(6, D_MODEL), res[2:]


class _ChipCopies:
    def __init__(self, lsem, ssem, rsem):
        self.x, self.y, self.c = _position()
        self.q = 2 * self.x + self.y
        self.lsem, self.ssem, self.rsem = lsem, ssem, rsem
        self.local, self.send, self.recv = [], [], []

    def _remote(self, a, j, src, dst, peer):
        return pltpu.make_async_remote_copy(src_ref=src, dst_ref=dst, send_sem=self.ssem.at[a, j],
                                            recv_sem=self.rsem.at[a, j], device_id=peer, device_id_type=MESH)

    def add(self, a, own_src, own_dst, src_for, dst_mine, dst_from):
        x, y, c, q = self.x, self.y, self.c, self.q
        self.local.append(pltpu.make_async_copy(own_src, own_dst, self.lsem.at[a]))
        for j in range(1, 4):
            peer = (x ^ ((j >> 1) & 1), y ^ (j & 1), c)
            self.send.append(self._remote(a, j, src_for(q ^ j), dst_mine, peer))
            self.recv.append(self._remote(a, j, own_src, dst_from(q ^ j), (x, y, c)))

    def start(self):
        for cp in self.local + self.send:
            cp.start()

    def wait(self):
        for cp in self.recv:
            cp.wait_recv()
        for cp in self.send:
            cp.wait_send()
        for cp in self.local:
            cp.wait()


def _gather_copies(ins, outs, lsem, ssem, rsem):
    cps = _ChipCopies(lsem, ssem, rsem)
    for a in range(len(ins)):
        cps.add(a, ins[a], outs[a].at[cps.q], lambda chip, a=a: ins[a], outs[a].at[cps.q],
                lambda chip, a=a: outs[a].at[chip])
    return cps


def _scatter_copies(ins, land, offs, lsem, ssem, rsem):
    cps = _ChipCopies(lsem, ssem, rsem)
    for a, (off, rows) in enumerate(offs):
        cps.add(a, ins[a].at[cps.q], land.at[cps.q, pl.ds(off, rows)], lambda chip, a=a: ins[a].at[chip],
                land.at[cps.q, pl.ds(off, rows)], lambda chip, off=off, rows=rows: land.at[chip, pl.ds(off, rows)])
    return cps


def _copy_sems(n):
    return [pltpu.SemaphoreType.DMA((n,)), pltpu.SemaphoreType.DMA((n, 4)), pltpu.SemaphoreType.DMA((n, 4))]


def _aug_masks(lane, h):
    a0 = HEAD_DIM if h % 2 == 0 else 0
    own = (lane < HEAD_DIM) if h % 2 == 0 else (lane >= HEAD_DIM)
    return own, (lambda k: lane == a0 + k), (lambda k0, k1: (lane >= a0 + k0) & (lane < a0 + k1))


def _pieces(x):
    hi, mid, lo = _split3(x)
    return hi.astype(F32), mid.astype(F32), lo.astype(F32)


def _inproj_fwd(x, ada, w_p, bf, ts):
    s = x.shape[0]
    ns = s // ts

    def body(x_ref, ada_ref, w_ref, bf_ref, q_ref, k_ref, v_ref, a_ref, g_ref, lf_ref, carry):
        i = pl.program_id(0)

        @pl.when(i == 0)
        def _():
            carry[...] = jnp.zeros_like(carry)

        u = (x_ref[...] * (1.0 + ada_ref[1:2, :]) + ada_ref[0:1, :]).astype(BF16)
        proj = _dot(u, w_ref[...])
        a_ref[...] = proj[:, 1536:2048]
        g_ref[...] = proj[:, 2048:2560]
        z = proj[:, 2560:2688] + bf_ref[...]
        lane = lax.broadcasted_iota(jnp.int32, (ts, 128), 1)
        logf = jnp.minimum(z, 0.0) - jnp.log(1.0 + jnp.exp(-jnp.abs(z)))
        logf = jnp.where(lane < N_HEADS, logf, 0.0)
        lf_ref[...] = logf
        r = lax.broadcasted_iota(jnp.int32, (ts, ts), 0)
        cc = lax.broadcasted_iota(jnp.int32, (ts, ts), 1)
        tri = (cc <= r).astype(BF16)
        fc = _tri_dot(tri, logf) + carry[...]
        carry[...] = fc[ts - 1:ts, :]
        for h in range(N_HEADS):
            pc = slice(128 * (h // 2), 128 * (h // 2) + 128)
            hc = slice(128 * h, 128 * h + 128)
            own, at, span = _aug_masks(lane, h)
            hi, mid, lo = _pieces(fc[:, h:h + 1])
            qp = proj[:, pc] * (HEAD_DIM ** -0.5)
            kp = proj[:, 512:1024][:, pc]
            vp = proj[:, 1024:1536][:, pc]
            q_aug = jnp.where(own, qp, jnp.where(at(0), hi, jnp.where(at(1), mid, jnp.where(at(2), lo,
                              jnp.where(span(3, 6), 1.0, 0.0)))))
            k_aug = jnp.where(own, kp, jnp.where(span(0, 3), 1.0, jnp.where(at(3), -hi, jnp.where(at(4), -mid,
                              jnp.where(at(5), -lo, jnp.where(span(6, 9), -1.0, 0.0))))))
            v_aug = jnp.where(own, vp, jnp.where(span(0, 3), 1.0, 0.0))
            q_ref[:, hc] = q_aug.astype(BF16)
            k_ref[:, hc] = k_aug.astype(BF16)
            v_ref[:, hc] = v_aug.astype(BF16)

    row = lambda i: (i, 0)
    full = lambda i: (0, 0)
    return pl.pallas_call(
        body, name="inproj_fwd", grid=(ns,),
        out_shape=(jax.ShapeDtypeStruct((s, N_HEADS * 128), BF16),) * 3 + (jax.ShapeDtypeStruct((s, 512), F32),) * 2
        + (jax.ShapeDtypeStruct((s, 128), F32),),
        in_specs=[pl.BlockSpec((ts, D_MODEL), row), pl.BlockSpec((6, D_MODEL), full),
                  pl.BlockSpec((D_MODEL, N_IN_PAD), full), pl.BlockSpec((1, 128), full)],
        out_specs=(pl.BlockSpec((ts, N_HEADS * 128), row),) * 3 + (pl.BlockSpec((ts, 512), row),) * 2
        + (pl.BlockSpec((ts, 128), row),),
        scratch_shapes=[pltpu.VMEM((1, 128), F32)],
        compiler_params=_cparams(1),
    )(x, ada, w_p, bf)


def _attn_fwd(qa, ka, va, shards, tq):
    s = qa.shape[0]
    nq = s // tq
    tk = tq
    n = len(shards)

    def body(q_ref, k_ref, v_ref, *rest):
        sh_in, (o_ref, qb_ref), sh_out = rest[:n], rest[n:n + 2], rest[n + 2:2 * n + 2]
        lsem, ssem, rsem = rest[2 * n + 2:]
        pair = pl.program_id(0)
        i = pl.program_id(1)

        @pl.when((pair == 0) & (i == 0))
        def _():
            _gather_copies(sh_in, sh_out, lsem, ssem, rsem).start()

        lane = lax.broadcasted_iota(jnp.int32, (tq, 128), 1)
        t_off = lax.broadcasted_iota(jnp.int32, (tq, tk), 0)
        s_off = lax.broadcasted_iota(jnp.int32, (tq, tk), 1)

        def block(j, carry, diagonal):
            start = pl.multiple_of(j * tk, tk)
            out = []
            for hh in range(2):
                cols = slice(128 * hh, 128 * hh + 128)
                m, acc = carry[2 * hh:2 * hh + 2]
                sc = _dot_nt(q_ref[:, cols], k_ref[pl.ds(start, tk), cols])
                if diagonal:
                    sc = jnp.where(s_off <= t_off, sc, -jnp.inf)
                m_new = jnp.maximum(m, jnp.max(sc, axis=-1, keepdims=True))
                pr = jnp.exp(sc - m_new).astype(BF16)
                acc = acc * jnp.exp(m - m_new) + _dot(pr, v_ref[pl.ds(start, tk), cols])
                out += [m_new, acc]
            return tuple(out)

        init = (jnp.full((tq, 1), -jnp.inf, F32), jnp.zeros((tq, 128), F32)) * 2
        res = lax.fori_loop(0, i, functools.partial(block, diagonal=False), init)
        res = block(i, res, diagonal=True)
        outs = []
        for hh in range(2):
            cols = slice(128 * hh, 128 * hh + 128)
            _, at, _ = _aug_masks(lane, hh)
            a0 = HEAD_DIM if hh == 0 else 0
            m, acc = res[2 * hh:2 * hh + 2]
            denom = acc[:, a0:a0 + 1]
            outs.append(acc / denom)
            hi, mid, lo = _split3(m + jnp.log(denom))
            qb_ref[:, cols] = jnp.where(at(6), hi, jnp.where(at(7), mid, jnp.where(at(8), lo, q_ref[:, cols])))
        o_ref[...] = jnp.where(lane < HEAD_DIM, outs[0], outs[1])

        @pl.when((pair == N_PAIRS - 1) & (i == nq - 1))
        def _():
            _gather_copies(sh_in, sh_out, lsem, ssem, rsem).wait()

    res = pl.pallas_call(
        body, name="attn_fwd", grid=(N_PAIRS, nq),
        out_shape=(jax.ShapeDtypeStruct((s, 512), F32), jax.ShapeDtypeStruct((s, N_HEADS * 128), BF16))
        + tuple(jax.ShapeDtypeStruct((N_SHARD,) + w.shape, w.dtype) for w in shards),
        in_specs=[pl.BlockSpec((tq, 256), lambda p, i: (i, p)), pl.BlockSpec((s, 256), lambda p, i: (0, p)),
                  pl.BlockSpec((s, 256), lambda p, i: (0, p))] + [ANY] * n,
        out_specs=(pl.BlockSpec((tq, 128), lambda p, i: (i, p)), pl.BlockSpec((tq, 256), lambda p, i: (i, p)))
        + (ANY,) * n,
        scratch_shapes=_copy_sems(n),
        compiler_params=_cparams(2),
    )(qa, ka, va, *shards)
    return res[0], res[1], res[2:]


def _fill_shifts(buf, shifted, ts):
    for s in range(1, 8):
        shifted[s - 1] = buf[s:s + ts + HALO - 8, :]


def _rows_at(buf, shifted, off, ts):
    s = off % 8
    if s == 0:
        return buf[off:off + ts, :]
    return shifted[s - 1, off - s:off - s + ts, :]


def _shift_scratch(ts):
    return pltpu.VMEM((7, ts + HALO - 8, 512), F32)


def _conv_branch(a, g, ah, gh, first, ugx, ush, wdw_ref, prm_ref, gm, ts):
    sg_g = jax.nn.sigmoid(g)
    ug = (a * sg_g).astype(BF16).astype(F32)
    ugh = jnp.where(first, 0.0, (ah * jax.nn.sigmoid(gh)).astype(BF16).astype(F32))
    ugx[0:HALO, :] = ugh
    ugx[HALO:HALO + ts, :] = ug
    _fill_shifts(ugx, ush, ts)
    y = jnp.zeros((ts, CONV_W), F32) + prm_ref[0:1, :]
    for kk in range(CONV_K):
        off = HALO - (CONV_K - 1) + kk
        y = y + wdw_ref[kk:kk + 1, :] * _rows_at(ugx, ush, off, ts)
    mu = _dot_f32(y, gm)
    d = y - mu
    var = _dot_f32(d * d, gm)
    rs = lax.rsqrt(var + LN_EPS)
    yhat = d * rs
    yn = yhat * prm_ref[1:2, :] + prm_ref[2:3, :]
    sg = jax.nn.sigmoid(yn)
    co = yn * sg
    return sg_g, rs, yhat, yn, sg, co


def _mix_inputs(o, co, prm_ref):
    ra = lax.rsqrt(_mean_last(o * o) + LN_EPS)
    oh = o * ra
    rc = lax.rsqrt(_mean_last(co * co) + LN_EPS)
    ch = co * rc
    mi = jnp.concatenate([oh * prm_ref[3:4, :], ch * prm_ref[4:5, :]], axis=-1).astype(BF16)
    return ra, oh, rc, ch, mi


def _layernorm_stats(r):
    mu = _mean_last(r)
    d = r - mu
    rstd = lax.rsqrt(_mean_last(d * d) + LN_EPS)
    return d * rstd, rstd


def _layernorm_bwd(dout, xh, rstd, gain):
    dxh = dout * gain
    return rstd * (dxh - _mean_last(dxh) - xh * _mean_last(dxh * xh))


def _halo_index(tile, ts):
    return jnp.maximum(tile * (ts // HALO) - 1, 0)


def _mid_fwd(a, g, o, x, ada, wdw, prm, ln1, w_out, gm, ts):
    s = x.shape[0]
    ns = s // ts

    def body(a_ref, g_ref, ah_ref, gh_ref, o_ref, x_ref, ada_ref, wdw_ref, prm_ref, ln_ref, wo_ref, gm_ref,
             x1_ref, ugx, ush):
        i = pl.program_id(0)
        co = _conv_branch(a_ref[...], g_ref[...], ah_ref[...], gh_ref[...], i == 0, ugx, ush, wdw_ref, prm_ref,
                          gm_ref[...], ts)[-1]
        mi = _mix_inputs(o_ref[...], co, prm_ref)[-1]
        mixed = _dot(mi, wo_ref[...])
        r1 = ALPHA * x_ref[...] + (1.0 + ada_ref[2:3, :]) * mixed
        xh, _ = _layernorm_stats(r1)
        x1_ref[...] = xh * ln_ref[0:1, :] + ln_ref[1:2, :]

    row = lambda i: (i, 0)
    full = lambda i: (0, 0)
    halo = lambda i: (_halo_index(i, ts), 0)
    return pl.pallas_call(
        body, name="mid_fwd", grid=(ns,),
        out_shape=jax.ShapeDtypeStruct((s, D_MODEL), F32),
        in_specs=[pl.BlockSpec((ts, 512), row), pl.BlockSpec((ts, 512), row),
                  pl.BlockSpec((HALO, 512), halo), pl.BlockSpec((HALO, 512), halo),
                  pl.BlockSpec((ts, 512), row), pl.BlockSpec((ts, D_MODEL), row),
                  pl.BlockSpec((6, D_MODEL), full), pl.BlockSpec((32, 512), full), pl.BlockSpec((8, 512), full),
                  pl.BlockSpec((2, D_MODEL), full), pl.BlockSpec((D_MODEL, D_MODEL), full),
                  pl.BlockSpec((512, 512), full)],
        out_specs=pl.BlockSpec((ts, D_MODEL), row),
        scratch_shapes=[pltpu.VMEM((ts + HALO, 512), F32), _shift_scratch(ts)],
        compiler_params=_cparams(1),
    )(a, g, a, g, o, x, ada, wdw, prm, ln1, w_out, gm)


def _ffn_fwd(x1, ada, w1, w2, ln2, tgt, ts):
    s = x1.shape[0]
    ns = s // ts
    nf = N_SHARD

    def body(x1_ref, ada_ref, w1_ref, w2_ref, ln_ref, t_ref, dff_ref, dx1_ref, pg_ref, ffacc, u2):
        i = pl.program_id(0)
        f = pl.program_id(1)

        @pl.when((i == 0) & (f == 0))
        def _():
            pg_ref[...] = jnp.zeros_like(pg_ref)

        @pl.when(f == 0)
        def _():
            u2[...] = (x1_ref[...] * (1.0 + ada_ref[4:5, :]) + ada_ref[3:4, :]).astype(BF16)
            ffacc[...] = jnp.zeros_like(ffacc)

        h = _dot(u2[...], w1_ref[0])
        r = jnp.maximum(h, 0.0)
        ffacc[...] += _dot((r * r).astype(BF16), w2_ref[0])

        @pl.when(f == nf - 1)
        def _():
            ff = ffacc[...]
            r2 = ALPHA * x1_ref[...] + (1.0 + ada_ref[5:6, :]) * ff
            xh, rstd = _layernorm_stats(r2)
            yv = xh * ln_ref[0:1, :] + ln_ref[1:2, :]
            err = yv - t_ref[...]
            dy = err * (1.0 / D_MODEL)
            dr2 = _layernorm_bwd(dy, xh, rstd, ln_ref[0:1, :])
            pg_ref[0:1, :] += _rowsum(dy * xh)
            pg_ref[1:2, :] += _rowsum(dy)
            pg_ref[2:3, :] += _rowsum(dr2 * ff)
            pg_ref[3:4, :] += _rowsum(err * err) * (0.5 / D_MODEL)
            dff_ref[...] = ((1.0 + ada_ref[5:6, :]) * dr2).astype(BF16)
            dx1_ref[...] = ALPHA * dr2

    row = lambda i, f: (i, 0)
    full = lambda i, f: (0, 0)
    chunk = lambda i, f: (f, 0, 0)
    return pl.pallas_call(
        body, name="ffn_fwd", grid=(ns, nf),
        out_shape=(jax.ShapeDtypeStruct((s, D_MODEL), BF16), jax.ShapeDtypeStruct((s, D_MODEL), F32),
                   jax.ShapeDtypeStruct((8, D_MODEL), F32)),
        in_specs=[pl.BlockSpec((ts, D_MODEL), row), pl.BlockSpec((6, D_MODEL), full),
                  pl.BlockSpec((1, D_MODEL, FF_CHUNK), chunk), pl.BlockSpec((1, FF_CHUNK, D_MODEL), chunk),
                  pl.BlockSpec((2, D_MODEL), full), pl.BlockSpec((ts, D_MODEL), row)],
        out_specs=(pl.BlockSpec((ts, D_MODEL), row), pl.BlockSpec((ts, D_MODEL), row),
                   pl.BlockSpec((8, D_MODEL), full)),
        scratch_shapes=[pltpu.VMEM((ts, D_MODEL), F32), pltpu.VMEM((ts, D_MODEL), BF16)],
        compiler_params=_cparams(2),
    )(x1, ada, w1, w2, ln2, tgt)


def _ffn_bwd_chunk(f, x1, ada, w1, w2, dff, dx1, ts):
    s = x1.shape[0]
    ns = s // ts

    def body(x1_ref, ada_ref, w1_ref, w2_ref, dff_ref, dx1_in, dx1_out, dw1_ref, dw2_ref, pg_ref):
        i = pl.program_id(0)

        @pl.when(i == 0)
        def _():
            pg_ref[...] = jnp.zeros_like(pg_ref)
            dw1_ref[...] = jnp.zeros_like(dw1_ref)
            dw2_ref[...] = jnp.zeros_like(dw2_ref)

        x1v = x1_ref[...]
        u2 = (x1v * (1.0 + ada_ref[4:5, :]) + ada_ref[3:4, :]).astype(BF16)
        h = _dot(u2, w1_ref[0])
        r = jnp.maximum(h, 0.0)
        hid = (r * r).astype(BF16)
        dffv = dff_ref[...]
        dh = (_dot_nt(dffv, w2_ref[0]) * (2.0 * r)).astype(BF16)
        dw2_ref[0] += _dot_tn(hid, dffv)
        dw1_ref[0] += _dot_tn(u2, dh)
        du2 = _dot_nt(dh, w1_ref[0])
        dx1_out[...] = dx1_in[...] + du2 * (1.0 + ada_ref[4:5, :])
        pg_ref[0:1, :] += _rowsum(du2 * x1v)
        pg_ref[1:2, :] += _rowsum(du2)

    row = lambda i: (i, 0)
    full = lambda i: (0, 0)
    full3 = lambda i: (0, 0, 0)
    chunk = lambda i: (f, 0, 0)
    return pl.pallas_call(
        body, name=f"ffn_bwd_{f}", grid=(ns,),
        out_shape=(jax.ShapeDtypeStruct((s, D_MODEL), F32), jax.ShapeDtypeStruct((1, D_MODEL, FF_CHUNK), F32),
                   jax.ShapeDtypeStruct((1, FF_CHUNK, D_MODEL), F32), jax.ShapeDtypeStruct((8, D_MODEL), F32)),
        in_specs=[pl.BlockSpec((ts, D_MODEL), row), pl.BlockSpec((6, D_MODEL), full),
                  pl.BlockSpec((1, D_MODEL, FF_CHUNK), chunk), pl.BlockSpec((1, FF_CHUNK, D_MODEL), chunk),
                  pl.BlockSpec((ts, D_MODEL), row), pl.BlockSpec((ts, D_MODEL), row)],
        out_specs=(pl.BlockSpec((ts, D_MODEL), row), pl.BlockSpec((1, D_MODEL, FF_CHUNK), full3),
                   pl.BlockSpec((1, FF_CHUNK, D_MODEL), full3), pl.BlockSpec((8, D_MODEL), full)),
        compiler_params=_cparams(1),
    )(x1, ada, w1, w2, dff, dx1)


def _mid_bwd(a, g, o, x, dx1, ada, wdw, prm, ln1, w_out, gm, sel, ts):
    s = x.shape[0]
    ns = s // ts

    def body(a_ref, g_ref, ah_ref, gh_ref, o_ref, x_ref, dx1_ref, ada_ref, wdw_ref, prm_ref, ln_ref, wo_ref,
             gm_ref, sel_ref,
             do_ref, da_ref, dg_ref, dxa_ref, dwo_ref, pgm_ref, pgc_ref, dwdw_ref,
             ugx, dyx, ush, dsh):
        i = pl.program_id(0)
        tile = ns - 1 - i

        @pl.when(i == 0)
        def _():
            dwo_ref[...] = jnp.zeros_like(dwo_ref)
            pgm_ref[...] = jnp.zeros_like(pgm_ref)
            pgc_ref[...] = jnp.zeros_like(pgc_ref)
            dwdw_ref[...] = jnp.zeros_like(dwdw_ref)
            dyx[ts:ts + HALO, :] = jnp.zeros((HALO, 512), F32)

        gmv = gm_ref[...]
        av = a_ref[...]
        ov = o_ref[...]
        sg_g, rs, yhat, yn, sg, co = _conv_branch(av, g_ref[...], ah_ref[...], gh_ref[...], tile == 0, ugx, ush,
                                                  wdw_ref, prm_ref, gmv, ts)
        ra, oh, rc, ch, mi = _mix_inputs(ov, co, prm_ref)
        mixed = _dot(mi, wo_ref[...])
        gt1 = 1.0 + ada_ref[2:3, :]
        r1 = ALPHA * x_ref[...] + gt1 * mixed
        xh, rstd = _layernorm_stats(r1)
        dx1 = dx1_ref[...]
        pgm_ref[0:1, :] += _rowsum(dx1 * xh)
        pgm_ref[1:2, :] += _rowsum(dx1)
        dr1 = _layernorm_bwd(dx1, xh, rstd, ln_ref[0:1, :])
        dxa_ref[...] = ALPHA * dr1
        pgm_ref[2:3, :] += _rowsum(dr1 * mixed)
        dmixed = (gt1 * dr1).astype(BF16)
        dmi = _dot_nt(dmixed, wo_ref[...])
        dwo_ref[...] += _dot_tn(mi, dmixed)
        dna = dmi[:, 0:512]
        dnc = dmi[:, 512:1024]
        pgc_ref[3:4, :] += _rowsum(dna * oh)
        doh = dna * prm_ref[3:4, :]
        do = ra * (doh - oh * _mean_last(doh * oh))
        lane = lax.broadcasted_iota(jnp.int32, (ts, 128), 1)
        delta = _dot_f32(do * ov, sel_ref[...])
        for h in range(N_HEADS):
            own, at, _ = _aug_masks(lane, h)
            hi, mid, lo = _pieces(-delta[:, h:h + 1])
            dop = do[:, 128 * (h // 2):128 * (h // 2) + 128]
            do_ref[:, 128 * h:128 * h + 128] = jnp.where(
                own, dop, jnp.where(at(0), hi, jnp.where(at(1), mid, jnp.where(at(2), lo, 0.0)))).astype(BF16)
        pgc_ref[4:5, :] += _rowsum(dnc * ch)
        dch = dnc * prm_ref[4:5, :]
        dco = rc * (dch - ch * _mean_last(dch * ch))
        dyn = dco * (sg * (1.0 + yn * (1.0 - sg)))
        pgc_ref[1:2, :] += _rowsum(dyn * yhat)
        pgc_ref[2:3, :] += _rowsum(dyn)
        dyh = dyn * prm_ref[1:2, :]
        dy = rs * (dyh - _dot_f32(dyh, gmv) - yhat * _dot_f32(dyh * yhat, gmv))
        pgc_ref[0:1, :] += _rowsum(dy)
        dyr = dy.astype(BF16).astype(F32)
        dyx[0:ts, :] = dyr
        _fill_shifts(dyx, dsh, ts)
        dug = jnp.zeros((ts, CONV_W), F32)
        for kk in range(CONV_K):
            off = HALO - (CONV_K - 1) + kk
            dwdw_ref[kk:kk + 1, :] += _rowsum(dyr * _rows_at(ugx, ush, off, ts))
            back = CONV_K - 1 - kk
            dug = dug + wdw_ref[kk:kk + 1, :] * _rows_at(dyx, dsh, back, ts)
        dyx[ts:ts + HALO, :] = dyr[0:HALO, :]
        da_ref[...] = (dug * sg_g).astype(BF16)
        dg_ref[...] = (dug * av * sg_g * (1.0 - sg_g)).astype(BF16)

    row = lambda i: (ns - 1 - i, 0)
    full = lambda i: (0, 0)
    halo = lambda i: (_halo_index(ns - 1 - i, ts), 0)
    return pl.pallas_call(
        body, name="mid_bwd", grid=(ns,),
        out_shape=(jax.ShapeDtypeStruct((s, N_HEADS * 128), BF16), jax.ShapeDtypeStruct((s, 512), BF16),
                   jax.ShapeDtypeStruct((s, 512), BF16), jax.ShapeDtypeStruct((s, D_MODEL), F32),
                   jax.ShapeDtypeStruct((D_MODEL, D_MODEL), F32),
                   jax.ShapeDtypeStruct((8, D_MODEL), F32), jax.ShapeDtypeStruct((8, 512), F32),
                   jax.ShapeDtypeStruct((32, 512), F32)),
        in_specs=[pl.BlockSpec((ts, 512), row), pl.BlockSpec((ts, 512), row),
                  pl.BlockSpec((HALO, 512), halo), pl.BlockSpec((HALO, 512), halo),
                  pl.BlockSpec((ts, 512), row), pl.BlockSpec((ts, D_MODEL), row), pl.BlockSpec((ts, D_MODEL), row),
                  pl.BlockSpec((6, D_MODEL), full), pl.BlockSpec((32, 512), full), pl.BlockSpec((8, 512), full),
                  pl.BlockSpec((2, D_MODEL), full), pl.BlockSpec((D_MODEL, D_MODEL), full),
                  pl.BlockSpec((512, 512), full), pl.BlockSpec((512, 128), full)],
        out_specs=(pl.BlockSpec((ts, N_HEADS * 128), row), pl.BlockSpec((ts, 512), row), pl.BlockSpec((ts, 512), row),
                   pl.BlockSpec((ts, D_MODEL), row),
                   pl.BlockSpec((D_MODEL, D_MODEL), full), pl.BlockSpec((8, D_MODEL), full),
                   pl.BlockSpec((8, 512), full), pl.BlockSpec((32, 512), full)),
        scratch_shapes=[pltpu.VMEM((ts + HALO, 512), F32), pltpu.VMEM((ts + HALO, 512), F32), _shift_scratch(ts),
                        _shift_scratch(ts)],
        compiler_params=_cparams(1),
    )(a, g, a, g, o, x, dx1, ada, wdw, prm, ln1, w_out, gm, sel)


def _attn_bwd(qb, ka, va, doa, dwo, dw1, dw2, tk):
    s = qb.shape[0]
    nk = s // tk
    tq = tk

    def body(q_ref, do_ref, k_ref, v_ref, dwo_hbm, dw1_hbm, dw2_hbm,
             dq_ref, rs_ref, dk_ref, dv_ref, cs_ref, land_out, land_ff, *sems):
        pair = pl.program_id(0)
        j = pl.program_id(1)

        def exchanges():
            return (_scatter_copies((dwo_hbm,), land_out, _STACK_OUT, *sems[:3]),
                    _scatter_copies((dw1_hbm, dw2_hbm), land_ff, _STACK_FF, *sems[3:]))

        @pl.when((pair == 0) & (j == 0))
        def _():
            for ex in exchanges():
                ex.start()

        @pl.when(j == 0)
        def _():
            dq_ref[...] = jnp.zeros_like(dq_ref)

        @pl.when((pair == 0) & (j == 0))
        def _():
            rs_ref[...] = jnp.zeros_like(rs_ref)
            cs_ref[...] = jnp.zeros_like(cs_ref)

        lane = lax.broadcasted_iota(jnp.int32, (tk, 128), 1)
        low = lane < HEAD_DIM
        t_off = lax.broadcasted_iota(jnp.int32, (tq, tk), 0)
        s_off = lax.broadcasted_iota(jnp.int32, (tq, tk), 1)

        def block(i, carry, diagonal):
            rows_q = pl.ds(pl.multiple_of(i * tq, tq), tq)
            dq_h, out = [], []
            for hh in range(2):
                cols = slice(128 * hh, 128 * hh + 128)
                dk_acc, dv_acc = carry[2 * hh:2 * hh + 2]
                qh = q_ref[rows_q, cols]
                dh = do_ref[rows_q, cols]
                pr = jnp.exp(_dot_nt(qh, k_ref[:, cols]))
                if diagonal:
                    pr = jnp.where(s_off <= t_off, pr, 0.0)
                ds = (pr * _dot_nt(dh, v_ref[:, cols])).astype(BF16)
                dq_h.append(_dot(ds, k_ref[:, cols]))
                out += [dk_acc + _dot_tn(ds, qh), dv_acc + _dot_tn(pr.astype(BF16), dh)]
            dq_ref[rows_q, :] += jnp.where(low, dq_h[0], dq_h[1])
            rs_ref[rows_q, :] += (jnp.where(lane == 2 * pair, dq_h[0][:, HEAD_DIM:HEAD_DIM + 1], 0.0)
                                  + jnp.where(lane == 2 * pair + 1, dq_h[1][:, 0:1], 0.0))
            return tuple(out)

        first = block(j, (jnp.zeros((tk, 128), F32),) * 4, diagonal=True)
        dk0, dv0, dk1, dv1 = lax.fori_loop(j + 1, nk, functools.partial(block, diagonal=False), first)
        dk_ref[...] = jnp.where(low, dk0, dk1).astype(BF16)
        dv_ref[...] = jnp.where(low, dv0, dv1).astype(BF16)
        rows_k = pl.ds(pl.multiple_of(j * tk, tk), tk)
        cs_ref[rows_k, :] += (jnp.where(lane == 2 * pair, dk0[:, HEAD_DIM + 3:HEAD_DIM + 4], 0.0)
                              + jnp.where(lane == 2 * pair + 1, dk1[:, 3:4], 0.0))

        @pl.when((pair == N_PAIRS - 1) & (j == nk - 1))
        def _():
            for ex in exchanges():
                ex.wait()

    whole = lambda p, j: (0, 0)
    return pl.pallas_call(
        body, name="attn_bwd", grid=(N_PAIRS, nk),
        out_shape=(jax.ShapeDtypeStruct((s, 512), F32), jax.ShapeDtypeStruct((s, 128), F32),
                   jax.ShapeDtypeStruct((s, 512), BF16), jax.ShapeDtypeStruct((s, 512), BF16),
                   jax.ShapeDtypeStruct((s, 128), F32), jax.ShapeDtypeStruct(dwo.shape, F32),
                   jax.ShapeDtypeStruct((N_SHARD, 2 * FF_CHUNK, D_MODEL), F32)),
        in_specs=[pl.BlockSpec((s, 256), lambda p, j: (0, p)), pl.BlockSpec((s, 256), lambda p, j: (0, p)),
                  pl.BlockSpec((tk, 256), lambda p, j: (j, p)), pl.BlockSpec((tk, 256), lambda p, j: (j, p)),
                  ANY, ANY, ANY],
        out_specs=(pl.BlockSpec((s, 128), lambda p, j: (0, p)), pl.BlockSpec((s, 128), whole),
                   pl.BlockSpec((tk, 128), lambda p, j: (j, p)), pl.BlockSpec((tk, 128), lambda p, j: (j, p)),
                   pl.BlockSpec((s, 128), whole), ANY, ANY),
        scratch_shapes=_copy_sems(1) + _copy_sems(2),
        compiler_params=_cparams(2),
    )(qb, doa, ka, va, dwo, dw1, dw2)


def _inproj_bwd(x, ada, w_p, dq, dk, dv, da, dg, dfc, drs, logf, dxa, ts):
    s = x.shape[0]
    ns = s // ts

    def body(x_ref, ada_ref, w_ref, dq_ref, dk_ref, dv_ref, da_ref, dg_ref, dfc_ref, drs_ref, lf_ref, dxa_ref,
             gx_ref, dw_ref, pgi_ref, dbf_ref, carry, dw_vm):
        i = pl.program_id(0)

        @pl.when(i == 0)
        def _():
            carry[...] = jnp.zeros_like(carry)
            dw_vm[...] = jnp.zeros_like(dw_vm)
            pgi_ref[...] = jnp.zeros_like(pgi_ref)
            dbf_ref[...] = jnp.zeros_like(dbf_ref)

        r = lax.broadcasted_iota(jnp.int32, (ts, ts), 0)
        cc = lax.broadcasted_iota(jnp.int32, (ts, ts), 1)
        tri = (cc >= r).astype(BF16)
        dlogf = carry[...] + _tri_dot(tri, drs_ref[...] - dfc_ref[...])
        carry[...] = dlogf[0:1, :]
        lane = lax.broadcasted_iota(jnp.int32, (ts, 128), 1)
        dz = jnp.where(lane < N_HEADS, dlogf * (1.0 - jnp.exp(lf_ref[...])), 0.0)
        dbf_ref[0:1, :] += _rowsum(dz)
        dproj = jnp.concatenate(
            [(dq_ref[...] * (HEAD_DIM ** -0.5)).astype(BF16), dk_ref[...], dv_ref[...], da_ref[...], dg_ref[...],
             dz.astype(BF16)], axis=-1)
        xv = x_ref[...]
        sc1 = 1.0 + ada_ref[1:2, :]
        u = (xv * sc1 + ada_ref[0:1, :]).astype(BF16)
        du = _dot_nt(dproj, w_ref[...])
        dw_vm[...] += _dot_tn(u, dproj)
        gx_ref[...] = dxa_ref[...] + du * sc1
        pgi_ref[0:1, :] += _rowsum(du * xv)
        pgi_ref[1:2, :] += _rowsum(du)

        @pl.when(i == ns - 1)
        def _():
            dw_ref[...] = dw_vm[...].astype(BF16)

    row = lambda i: (ns - 1 - i, 0)
    full = lambda i: (0, 0)
    return pl.pallas_call(
        body, name="inproj_bwd", grid=(ns,),
        out_shape=(jax.ShapeDtypeStruct((s, D_MODEL), F32), jax.ShapeDtypeStruct((D_MODEL, N_IN_PAD), BF16),
                   jax.ShapeDtypeStruct((8, D_MODEL), F32), jax.ShapeDtypeStruct((8, 128), F32)),
        in_specs=[pl.BlockSpec((ts, D_MODEL), row), pl.BlockSpec((6, D_MODEL), full),
                  pl.BlockSpec((D_MODEL, N_IN_PAD), full)]
        + [pl.BlockSpec((ts, 512), row)] * 5 + [pl.BlockSpec((ts, 128), row)] * 3
        + [pl.BlockSpec((ts, D_MODEL), row)],
        out_specs=(pl.BlockSpec((ts, D_MODEL), row), pl.BlockSpec((D_MODEL, N_IN_PAD), full),
                   pl.BlockSpec((8, D_MODEL), full), pl.BlockSpec((8, 128), full)),
        scratch_shapes=[pltpu.VMEM((1, 128), F32), pltpu.VMEM((D_MODEL, N_IN_PAD), F32)],
        compiler_params=_cparams(1),
    )(x, ada, w_p, dq, dk, dv, da, dg, dfc, drs, logf, dxa)


def _small_reduce(packed):
    def body(p_ref, sum_ref, all_ref, ssem, rsem):
        x, y, c = _position()
        me = 4 * x + 2 * y + c
        all_ref[me] = p_ref[...]
        sends = []
        for k in range(1, 8):
            peer = (x ^ ((k >> 2) & 1), y ^ ((k >> 1) & 1), c ^ (k & 1))
            cp = pltpu.make_async_remote_copy(
                src_ref=p_ref, dst_ref=all_ref.at[me], send_sem=ssem.at[k], recv_sem=rsem.at[k],
                device_id=peer, device_id_type=MESH)
            cp.start()
            sends.append(cp)
        for k in range(1, 8):
            pltpu.make_async_remote_copy(
                src_ref=p_ref, dst_ref=all_ref.at[me ^ k], send_sem=ssem.at[k], recv_sem=rsem.at[k],
                device_id=(x, y, c), device_id_type=MESH).wait_recv()
        for cp in sends:
            cp.wait_send()
        total = all_ref[0]
        for dev in range(1, 8):
            total = total + all_ref[dev]
        sum_ref[...] = total
        loss = jnp.sum(total[SMALL_ROWS - 1:SMALL_ROWS, :], axis=-1, keepdims=True)
        sum_ref[SMALL_ROWS - 1:SMALL_ROWS, :] = jnp.broadcast_to(loss, (1, D_MODEL))

    vm = pl.BlockSpec(memory_space=pltpu.VMEM)
    return pl.pallas_call(
        body, name="small_reduce",
        out_shape=(jax.ShapeDtypeStruct((SMALL_ROWS, D_MODEL), F32), jax.ShapeDtypeStruct((8, SMALL_ROWS, D_MODEL), F32)),
        in_specs=[vm], out_specs=(vm, vm),
        scratch_shapes=[pltpu.SemaphoreType.DMA((8,)), pltpu.SemaphoreType.DMA((8,))],
        compiler_params=pltpu.CompilerParams(vmem_limit_bytes=VMEM_LIMIT),
    )(packed)


def _adam_math(gv, wv, mv, vv):
    m_new = B1 * mv + (1.0 - B1) * gv
    v_new = B2 * vv + (1.0 - B2) * (gv * gv)
    m_hat = m_new / (1.0 - B1 ** STEP)
    v_hat = v_new / (1.0 - B2 ** STEP)
    delta = -LR * (m_hat / (jnp.sqrt(v_hat) + ADAM_EPS) + WD * wv)
    return delta, m_new, v_new


def _adamw(gv, wv, mv, vv, name):
    rows, cols = gv.shape
    tr = rows
    for cand in (256, 128, 64, 32, 16, 8):
        if rows % cand == 0 and rows > cand:
            tr = cand
            break

    def body(g_ref, w_ref, m_ref, v_ref, d_ref, mo_ref, vo_ref):
        d_ref[...], mo_ref[...], vo_ref[...] = _adam_math(g_ref[...], w_ref[...], m_ref[...], v_ref[...])

    spec = pl.BlockSpec((tr, cols), lambda i: (i, 0))
    return pl.pallas_call(
        body, name=name, grid=(rows // tr,),
        out_shape=(jax.ShapeDtypeStruct((rows, cols), F32),) * 3,
        in_specs=[spec] * 4, out_specs=(spec,) * 3,
        compiler_params=_cparams(1),
    )(gv, wv, mv, vv)


def _w_ada_update(sct, dd, wv, mv, vv):
    rows, cols = wv.shape
    tr = 128

    def body(s_ref, d_ref, w_ref, m_ref, v_ref, g_ref, dl_ref, mo_ref, vo_ref):
        sv = s_ref[...]
        dv = d_ref[...]
        gv = sv[:, 0:1] * dv[0:1, :]
        for b in range(1, 8):
            gv = gv + sv[:, b:b + 1] * dv[b:b + 1, :]
        g_ref[...] = gv
        dl_ref[...], mo_ref[...], vo_ref[...] = _adam_math(gv, w_ref[...], m_ref[...], v_ref[...])

    spec = pl.BlockSpec((tr, cols), lambda i: (i, 0))
    return pl.pallas_call(
        body, name="w_ada_update", grid=(rows // tr,),
        out_shape=(jax.ShapeDtypeStruct((rows, cols), F32),) * 4,
        in_specs=[pl.BlockSpec((tr, 8), lambda i: (i, 0)), pl.BlockSpec((8, cols), lambda i: (0, 0))] + [spec] * 3,
        out_specs=(spec,) * 4,
        compiler_params=_cparams(1),
    )(sct, dd, wv, mv, vv)


def _grad_exchange(part):
    rows = part.shape[1]

    def body(p_hbm, land_hbm, lsem, ssem, rsem):
        cps = _scatter_copies((p_hbm,), land_hbm, ((0, rows),), lsem, ssem, rsem)
        cps.start()
        cps.wait()

    return pl.pallas_call(
        body, name="grad_exchange",
        out_shape=jax.ShapeDtypeStruct(part.shape, part.dtype),
        in_specs=[ANY], out_specs=ANY,
        scratch_shapes=_copy_sems(1),
    )(part)


def _sum_chips(land_in, land_out, land_ff):
    tr = 256
    n_in, n_out = 768 // tr, 256 // tr

    def body(in_ref, out_ref, ff_ref, s_ref):
        i = pl.program_id(0)

        def total(ref):
            s_ref[...] = ((ref[0].astype(F32) + ref[1].astype(F32)) + ref[2].astype(F32)) + ref[3].astype(F32)

        pl.when(i < n_in)(lambda: total(in_ref))
        pl.when((i >= n_in) & (i < n_in + n_out))(lambda: total(out_ref))
        pl.when(i >= n_in + n_out)(lambda: total(ff_ref))

    return pl.pallas_call(
        body, name="sum_chips", grid=(STACK_ROWS // tr,),
        out_shape=jax.ShapeDtypeStruct((STACK_ROWS, D_MODEL), F32),
        in_specs=[pl.BlockSpec((N_SHARD, tr, D_MODEL), lambda i: (0, jnp.minimum(i, n_in - 1), 0)),
                  pl.BlockSpec((N_SHARD, tr, D_MODEL), lambda i: (0, jnp.clip(i - n_in, 0, n_out - 1), 0)),
                  pl.BlockSpec((N_SHARD, tr, D_MODEL), lambda i: (0, jnp.maximum(i - n_in - n_out, 0), 0))],
        out_specs=pl.BlockSpec((tr, D_MODEL), lambda i: (i, 0)),
        compiler_params=_cparams(1),
    )(land_in, land_out, land_ff)


def _core_swap(part):
    def body(p_ref, o_ref, ssem, rsem):
        x, y, c = _position()
        cp = pltpu.make_async_remote_copy(src_ref=p_ref, dst_ref=o_ref, send_sem=ssem, recv_sem=rsem,
                                          device_id=(x, y, 1 - c), device_id_type=MESH)
        cp.start()
        cp.wait()

    return pl.pallas_call(
        body, name="core_swap",
        out_shape=jax.ShapeDtypeStruct(part.shape, part.dtype),
        in_specs=[ANY], out_specs=ANY,
        scratch_shapes=[pltpu.SemaphoreType.DMA, pltpu.SemaphoreType.DMA],
    )(part)


def _add_pair(mine, other):
    tr = 256

    def body(a_ref, b_ref, o_ref):
        o_ref[...] = a_ref[...] + b_ref[...]

    spec = pl.BlockSpec((tr, D_MODEL), lambda i: (i, 0))
    return pl.pallas_call(
        body, name="add_pair", grid=(STACK_ROWS // tr,),
        out_shape=jax.ShapeDtypeStruct((STACK_ROWS, D_MODEL), F32),
        in_specs=[spec, spec], out_specs=spec,
        compiler_params=_cparams(1),
    )(mine, other)


def _pad_lanes(v, width=D_MODEL):
    v = v.reshape(1, -1)
    return jnp.pad(v, ((0, 0), (0, width - v.shape[1])))


def _pack_small(b_ada, ln1_g, ln1_b, ln2_g, ln2_b, b_dw, gn_g, gn_b, g_attn, g_conv, b_forget, w_dw_full, last):
    rows = [b_ada.reshape(6, D_MODEL)] + [_pad_lanes(v) for v in
                                          (ln1_g, ln1_b, ln2_g, ln2_b, b_dw, gn_g, gn_b, g_attn, g_conv, b_forget)]
    rows.append(jnp.pad(w_dw_full.reshape(CONV_K, -1), ((0, 0), (0, D_MODEL - w_dw_full.reshape(CONV_K, -1).shape[1]))))
    rows.append(_pad_lanes(last))
    return jnp.concatenate(rows, axis=0)


def _unpack_small(p):
    return dict(b_ada=p[0:6].reshape(1, 6 * D_MODEL), ln1_g=p[6:7], ln1_b=p[7:8], ln2_g=p[8:9], ln2_b=p[9:10],
                b_dw=p[10:11, :512], gn_g=p[11:12, :512], gn_b=p[12:13, :512], g_attn_out=p[13:14, :512],
                g_conv_out=p[14:15, :512], b_forget=p[15:16, :N_HEADS])


def kernel(x, c, w_ada, b_ada, w_in, b_forget, w_dw, b_dw, gn_g, gn_b, g_attn_out, g_conv_out, w_out, ln1_g, ln1_b, w_ff1, w_ff2, ln2_g, ln2_b, loss_target, m_w_ada, m_b_ada, m_w_in, m_b_forget, m_w_dw, m_b_dw, m_gn_g, m_gn_b, m_g_attn_out, m_g_conv_out, m_w_out, m_ln1_g, m_ln1_b, m_w_ff1, m_w_ff2, m_ln2_g, m_ln2_b, v_w_ada, v_b_ada, v_w_in, v_b_forget, v_w_dw, v_b_dw, v_gn_g, v_gn_b, v_g_attn_out, v_g_conv_out, v_w_out, v_ln1_g, v_ln1_b, v_w_ff1, v_w_ff2, v_ln2_g, v_ln2_b):
    seq = x.shape[1]
    ts = min(512, seq // 2)
    tq = min(512, seq // 2)
    ts_mid = min(256, seq // 2)
    q_idx = 2 * lax.axis_index("x") + lax.axis_index("y")
    xs = x[0]
    tgt = loss_target[0]

    w_in_sh = jnp.pad(w_in[0], ((0, 0), (0, IN_SHARD_PAD - IN_SHARD))).astype(BF16)
    wdw_rows = jnp.pad(w_dw[0, :, 0, :], ((0, 1), (0, 0)))
    sc_all, ada, (win_all, wdw_all) = _ada_fwd(c, w_ada[0], b_ada, [w_in_sh, wdw_rows])
    w_in_full = jnp.transpose(win_all[:, :, :IN_SHARD], (1, 0, 2)).reshape(D_MODEL, N_IN)
    w_p = jnp.concatenate([w_in_full[:, 0:1536], w_in_full[:, 1544:2568], w_in_full[:, 1536:1544],
                           jnp.zeros((D_MODEL, 120), BF16)], axis=1)
    bf = _pad_lanes(b_forget, 128)
    wdw_full = lax.reduce_precision(jnp.transpose(wdw_all, (1, 0, 2)).reshape(32, 512), 8, 7)

    prm = jnp.concatenate([b_dw, gn_g, gn_b, g_attn_out, g_conv_out, jnp.zeros((3, 512), F32)], axis=0)
    ln1 = jnp.concatenate([ln1_g, ln1_b], axis=0)
    ln2 = jnp.concatenate([ln2_g, ln2_b], axis=0)
    ch = jnp.arange(512)
    gm = ((ch[:, None] // HEAD_DIM == ch[None, :] // HEAD_DIM).astype(F32) / HEAD_DIM).astype(BF16)
    sel = (ch[:, None] // HEAD_DIM == jnp.arange(128)[None, :]).astype(BF16)

    qa, ka, va, a, g, logf = _inproj_fwd(xs, ada, w_p, bf, ts)
    o, qb, (wout_all, w1_all, w2_all) = _attn_fwd(
        qa, ka, va, [w_out[0].astype(BF16), w_ff1[0].astype(BF16), w_ff2[0].astype(BF16)], tq)
    w_out_full = wout_all.reshape(D_MODEL, D_MODEL)
    x1 = _mid_fwd(a, g, o, xs, ada, wdw_full, prm, ln1, w_out_full, gm, ts_mid)
    dff, dx1, pg_f = _ffn_fwd(x1, ada, w1_all, w2_all, ln2, tgt, ts)

    dw1, dw2, pg_b = [], [], jnp.zeros((8, D_MODEL), F32)
    for f in range(N_SHARD):
        dx1, dw1_f, dw2_f, pg_bf = _ffn_bwd_chunk(f, x1, ada, w1_all, w2_all, dff, dx1, ts)
        dw1.append(dw1_f)
        dw2.append(dw2_f)
        pg_b = pg_b + pg_bf
    dw1 = jnp.concatenate(dw1, axis=0)
    dw2 = jnp.concatenate(dw2, axis=0)
    doa, da, dg, dxa, dwo, pgm, pgc, dwdw = _mid_bwd(
        a, g, o, xs, dx1, ada, wdw_full, prm, ln1, w_out_full, gm, sel, ts_mid)
    dq, drs, dk, dv, dfc, land_out, land_ff = _attn_bwd(qb, ka, va, doa, dwo.reshape(N_SHARD, 256, D_MODEL),
                                                         dw1, dw2, tq)
    gx, dwp, pgi, dbf = _inproj_bwd(xs, ada, w_p, dq, dk, dv, da, dg, dfc, drs, logf, dxa, ts)

    d_ada = jnp.concatenate([pgi[1:2], pgi[0:1], pgm[2:3], pg_b[1:2], pg_b[0:1], pg_f[2:3]], axis=0)
    packed = _pack_small(d_ada, pgm[0:1], pgm[1:2], pg_f[0:1], pg_f[1:2], pgc[0:1], pgc[1:2], pgc[2:3], pgc[3:4],
                         pgc[4:5], dbf[0:1, :N_HEADS], dwdw[0:CONV_K], pg_f[3:4])
    small_sum, small_all = _small_reduce(packed)
    loss = small_sum[SMALL_ROWS - 1, 0]
    gsm = _unpack_small(small_sum)
    g_wdw = lax.dynamic_slice(small_sum[16:16 + CONV_K, :512], (0, q_idx * 128), (CONV_K, 128))

    zrow = jnp.zeros((CONV_K + 1, D_MODEL), F32)
    w_small = _pack_small(b_ada, ln1_g, ln1_b, ln2_g, ln2_b, b_dw, gn_g, gn_b, g_attn_out,
                          g_conv_out, b_forget, zrow[:CONV_K, :512], zrow[0])
    m_small = _pack_small(m_b_ada, m_ln1_g, m_ln1_b, m_ln2_g, m_ln2_b, m_b_dw, m_gn_g, m_gn_b, m_g_attn_out,
                          m_g_conv_out, m_b_forget, zrow[:CONV_K, :512], zrow[0])
    v_small = _pack_small(v_b_ada, v_ln1_g, v_ln1_b, v_ln2_g, v_ln2_b, v_b_dw, v_gn_g, v_gn_b, v_g_attn_out,
                          v_g_conv_out, v_b_forget, zrow[:CONV_K, :512], zrow[0])
    d_small, mn_small, vn_small = (_unpack_small(t) for t in _adamw(small_sum, w_small, m_small, v_small, "adamw_small"))
    d_wdw, mn_wdw, vn_wdw = _adamw(g_wdw, w_dw[0, :, 0, :], m_w_dw[0, :, 0, :], v_w_dw[0, :, 0, :], "adamw_wdw")

    dd = lax.dynamic_slice(small_all[:, 0:6, :].reshape(8, 6 * D_MODEL), (0, q_idx * 1536), (8, 1536))
    g_wada, d_wada, mn_wada, vn_wada = _w_ada_update(sc_all.T, dd, w_ada[0], m_w_ada[0], v_w_ada[0])

    dw_in_cols = jnp.concatenate([dwp[:, 0:1536], dwp[:, 2560:2568], dwp[:, 1536:2560]], axis=1)
    dw_in_sh = jnp.pad(jnp.transpose(dw_in_cols.reshape(D_MODEL, N_SHARD, IN_SHARD), (1, 0, 2)),
                       ((0, 0), (0, 0), (0, IN_SHARD_PAD - IN_SHARD))).reshape(N_SHARD, 768, D_MODEL)
    part = _sum_chips(_grad_exchange(dw_in_sh), land_out, land_ff)
    total = _add_pair(part, _core_swap(part))
    g_win = total[0:768].reshape(D_MODEL, IN_SHARD_PAD)[:, :IN_SHARD]
    g_wout = total[768:1024]
    g_w1 = total[1024:2048]
    g_w2 = total[2048:3072]
    d_win, mn_win, vn_win = _adamw(g_win, w_in[0], m_w_in[0], v_w_in[0], "adamw_w_in")
    d_wout, mn_wout, vn_wout = _adamw(g_wout, w_out[0], m_w_out[0], v_w_out[0], "adamw_w_out")
    d_w1, mn_w1, vn_w1 = _adamw(g_w1, w_ff1[0], m_w_ff1[0], v_w_ff1[0], "adamw_w_ff1")
    d_w2, mn_w2, vn_w2 = _adamw(g_w2, w_ff2[0], m_w_ff2[0], v_w_ff2[0], "adamw_w_ff2")

    def group(wada, sm, win, wdw, wout, w1, w2):
        return (wada[None], sm["b_ada"], win[None], sm["b_forget"], wdw[None, :, None, :], sm["b_dw"], sm["gn_g"],
                sm["gn_b"], sm["g_attn_out"], sm["g_conv_out"], wout[None], sm["ln1_g"], sm["ln1_b"], w1[None],
                w2[None], sm["ln2_g"], sm["ln2_b"])

    return ((loss, gx[None])
            + group(g_wada, gsm, g_win, g_wdw, g_wout, g_w1, g_w2)
            + group(d_wada, d_small, d_win, d_wdw, d_wout, d_w1, d_w2)
            + group(mn_wada, mn_small, mn_win, mn_wdw, mn_wout, mn_w1, mn_w2)
            + group(vn_wada, vn_small, vn_win, vn_wdw, vn_wout, vn_w1, vn_w2))
```

```python
import functools

import jax
import jax.numpy as jnp
from jax import lax
from jax.experimental import pallas as pl
from jax.experimental.pallas import tpu as pltpu

F32 = jnp.float32
BF16 = jnp.bfloat16
MESH = pl.DeviceIdType.MESH
ANY = pl.BlockSpec(memory_space=pl.ANY)

D_MODEL = 1024
HEAD_DIM = 64
ATTN_W = 512
CONV_W = 512
N_HEADS = 8
N_PAIRS = 4
CONV_K = 31
HALO = 32
D_FF = 4096
N_SHARD = 4
FF_CHUNK = D_FF // N_SHARD
N_IN = 2568
IN_SHARD = N_IN // N_SHARD
IN_SHARD_PAD = 768
N_IN_PAD = 5 * 512 + 128
LN_EPS = 1e-5
ALPHA = 2.0 ** 0.25
LR, B1, B2, ADAM_EPS, WD, STEP = 0.001, 0.9, 0.999, 1e-08, 0.01, 10
VMEM_LIMIT = 56 * 1024 * 1024
SMALL_ROWS = 48
STACK_ROWS = 768 + 256 + 1024 + 1024
_STACK_OUT = ((0, 256),)
_STACK_FF = ((0, 1024), (1024, 1024))


def _cparams(n_axes):
    return pltpu.CompilerParams(dimension_semantics=("arbitrary",) * n_axes, vmem_limit_bytes=VMEM_LIMIT)


def _dot(a, b):
    return jnp.dot(a, b, preferred_element_type=F32)


def _dot_nt(a, b):
    return lax.dot_general(a, b, (((1,), (1,)), ((), ())), preferred_element_type=F32)


def _dot_tn(a, b):
    return lax.dot_general(a, b, (((0,), (0,)), ((), ())), preferred_element_type=F32)


def _dot_f32(a, b):
    hi, mid, lo = _split3(a)
    return _dot(hi, b) + _dot(mid, b) + _dot(lo, b)


def _split3(x):
    hi = x.astype(BF16)
    r = x - hi.astype(F32)
    mid = r.astype(BF16)
    lo = (r - mid.astype(F32)).astype(BF16)
    return hi, mid, lo


def _tri_dot(tri, x):
    hi, mid, lo = _split3(x)
    return _dot(tri, hi) + _dot(tri, mid) + _dot(tri, lo)


def _rowsum(x):
    return jnp.sum(x, axis=0, keepdims=True)


def _mean_last(x):
    return jnp.mean(x, axis=-1, keepdims=True)


def _position():
    x, y, c = lax.axis_index("x"), lax.axis_index("y"), lax.axis_index("c")
    return x, y, c


def _ada_fwd(c_row, w_ada, b_ada, shards):
    n_col = w_ada.shape[1]
    n = len(shards)

    def body(c_ref, w_ref, b_ref, *rest):
        sh_in, (sc_ref, ada_ref), sh_out = rest[:n], rest[n:n + 2], rest[n + 2:2 * n + 2]
        call_ref, part_ref, pall_ref, s1, r1, s2, r2, lsem, ssem, rsem = rest[2 * n + 2:]
        x, y, c = _position()
        me = 4 * x + 2 * y + c
        q = 2 * x + y
        gather = _gather_copies(sh_in, sh_out, lsem, ssem, rsem)
        gather.start()
        call_ref[me] = jnp.broadcast_to(c_ref[...], (8, D_MODEL))

        def c_copy(k):
            peer = (x ^ ((k >> 2) & 1), y ^ ((k >> 1) & 1), c ^ (k & 1))
            return pltpu.make_async_remote_copy(
                src_ref=call_ref.at[me], dst_ref=call_ref.at[me], send_sem=s1.at[k], recv_sem=r1.at[k],
                device_id=peer, device_id_type=MESH)

        def c_recv(k):
            src = me ^ k
            return pltpu.make_async_remote_copy(
                src_ref=call_ref.at[src], dst_ref=call_ref.at[src], send_sem=s1.at[k], recv_sem=r1.at[k],
                device_id=(x, y, c), device_id_type=MESH)

        sends = [c_copy(k) for k in range(1, 8)]
        for cp in sends:
            cp.start()
        for k in range(1, 8):
            c_recv(k).wait_recv()
        for cp in sends:
            cp.wait_send()

        row = lax.broadcasted_iota(jnp.int32, (8, D_MODEL), 0)
        c_all = jnp.zeros((8, D_MODEL), F32)
        for j in range(8):
            c_all = jnp.where(row == j, call_ref[j], c_all)
        sc_all = c_all * jax.nn.sigmoid(c_all)
        sc_ref[...] = sc_all
        b_slice = b_ref[:, pl.ds(pl.multiple_of(q * n_col, 128), n_col)]
        part = _dot(sc_all.astype(BF16), w_ref[...].astype(BF16)) + b_slice
        part_ref[...] = part
        pall_ref[q] = part

        def p_copy(j):
            peer = (x ^ ((j >> 1) & 1), y ^ (j & 1), c)
            return pltpu.make_async_remote_copy(
                src_ref=part_ref, dst_ref=pall_ref.at[q], send_sem=s2.at[j], recv_sem=r2.at[j],
                device_id=peer, device_id_type=MESH)

        def p_recv(j):
            src_q = q ^ j
            return pltpu.make_async_remote_copy(
                src_ref=part_ref, dst_ref=pall_ref.at[src_q], send_sem=s2.at[j], recv_sem=r2.at[j],
                device_id=(x, y, c), device_id_type=MESH)

        sends2 = [p_copy(j) for j in range(1, 4)]
        for cp in sends2:
            cp.start()
        for j in range(1, 4):
            p_recv(j).wait_recv()
        for cp in sends2:
            cp.wait_send()
        for qq in range(N_SHARD):
            ada_ref[qq] = pall_ref[qq, pl.ds(me, 1), :]
        gather.wait()

    vm = pl.BlockSpec(memory_space=pltpu.VMEM)
    res = pl.pallas_call(
        body, name="ada_fwd",
        out_shape=(jax.ShapeDtypeStruct((8, D_MODEL), F32), jax.ShapeDtypeStruct((N_SHARD, 1, n_col), F32))
        + tuple(jax.ShapeDtypeStruct((N_SHARD,) + w.shape, w.dtype) for w in shards),
        in_specs=[vm, vm, vm] + [ANY] * n, out_specs=(vm, vm) + (ANY,) * n,
        scratch_shapes=[pltpu.VMEM((8, 8, D_MODEL), F32), pltpu.VMEM((8, n_col), F32),
                        pltpu.VMEM((N_SHARD, 8, n_col), F32),
                        pltpu.SemaphoreType.DMA((8,)), pltpu.SemaphoreType.DMA((8,)),
                        pltpu.SemaphoreType.DMA((4,)), pltpu.SemaphoreType.DMA((4,))] + _copy_sems(n),
        compiler_params=pltpu.CompilerParams(vmem_limit_bytes=VMEM_LIMIT),
    )(c_row, w_ada, b_ada, *shards)
    return res[0], res[1].reshape(6, D_MODEL), res[2:]


class _ChipCopies:
    def __init__(self, lsem, ssem, rsem):
        self.x, self.y, self.c = _position()
        self.q = 2 * self.x + self.y
        self.lsem, self.ssem, self.rsem = lsem, ssem, rsem
        self.local, self.send, self.recv = [], [], []

    def _remote(self, a, j, src, dst, peer):
        return pltpu.make_async_remote_copy(src_ref=src, dst_ref=dst, send_sem=self.ssem.at[a, j],
                                            recv_sem=self.rsem.at[a, j], device_id=peer, device_id_type=MESH)

    def add(self, a, own_src, own_dst, src_for, dst_mine, dst_from):
        x, y, c, q = self.x, self.y, self.c, self.q
        self.local.append(pltpu.make_async_copy(own_src, own_dst, self.lsem.at[a]))
        for j in range(1, 4):
            peer = (x ^ ((j >> 1) & 1), y ^ (j & 1), c)
            self.send.append(self._remote(a, j, src_for(q ^ j), dst_mine, peer))
            self.recv.append(self._remote(a, j, own_src, dst_from(q ^ j), (x, y, c)))

    def start(self):
        for cp in self.local + self.send:
            cp.start()

    def wait(self):
        for cp in self.recv:
            cp.wait_recv()
        for cp in self.send:
            cp.wait_send()
        for cp in self.local:
            cp.wait()


def _gather_copies(ins, outs, lsem, ssem, rsem):
    cps = _ChipCopies(lsem, ssem, rsem)
    for a in range(len(ins)):
        cps.add(a, ins[a], outs[a].at[cps.q], lambda chip, a=a: ins[a], outs[a].at[cps.q],
                lambda chip, a=a: outs[a].at[chip])
    return cps


def _scatter_copies(ins, land, offs, lsem, ssem, rsem):
    cps = _ChipCopies(lsem, ssem, rsem)
    for a, (off, rows) in enumerate(offs):
        cps.add(a, ins[a].at[cps.q], land.at[cps.q, pl.ds(off, rows)], lambda chip, a=a: ins[a].at[chip],
                land.at[cps.q, pl.ds(off, rows)], lambda chip, off=off, rows=rows: land.at[chip, pl.ds(off, rows)])
    return cps


def _copy_sems(n):
    return [pltpu.SemaphoreType.DMA((n,)), pltpu.SemaphoreType.DMA((n, 4)), pltpu.SemaphoreType.DMA((n, 4))]


def _aug_masks(lane, h):
    a0 = HEAD_DIM if h % 2 == 0 else 0
    own = (lane < HEAD_DIM) if h % 2 == 0 else (lane >= HEAD_DIM)
    return own, (lambda k: lane == a0 + k), (lambda k0, k1: (lane >= a0 + k0) & (lane < a0 + k1))


def _pieces(x):
    hi, mid, lo = _split3(x)
    return hi.astype(F32), mid.astype(F32), lo.astype(F32)


def _inproj_fwd(x, ada, w_p, bf, ts):
    s = x.shape[0]
    ns = s // ts

    def body(x_ref, ada_ref, w_ref, bf_ref, q_ref, k_ref, v_ref, a_ref, g_ref, lf_ref, carry):
        i = pl.program_id(0)

        @pl.when(i == 0)
        def _():
            carry[...] = jnp.zeros_like(carry)

        u = (x_ref[...] * (1.0 + ada_ref[1:2, :]) + ada_ref[0:1, :]).astype(BF16)
        proj = _dot(u, w_ref[...])
        a_ref[...] = proj[:, 1536:2048]
        g_ref[...] = proj[:, 2048:2560]
        z = proj[:, 2560:2688] + bf_ref[...]
        lane = lax.broadcasted_iota(jnp.int32, (ts, 128), 1)
        logf = jnp.minimum(z, 0.0) - jnp.log(1.0 + jnp.exp(-jnp.abs(z)))
        logf = jnp.where(lane < N_HEADS, logf, 0.0)
        lf_ref[...] = logf
        r = lax.broadcasted_iota(jnp.int32, (ts, ts), 0)
        cc = lax.broadcasted_iota(jnp.int32, (ts, ts), 1)
        tri = (cc <= r).astype(BF16)
        fc = _tri_dot(tri, logf) + carry[...]
        carry[...] = fc[ts - 1:ts, :]
        for h in range(N_HEADS):
            pc = slice(128 * (h // 2), 128 * (h // 2) + 128)
            hc = slice(128 * h, 128 * h + 128)
            own, at, span = _aug_masks(lane, h)
            hi, mid, lo = _pieces(fc[:, h:h + 1])
            qp = proj[:, pc] * (HEAD_DIM ** -0.5)
            kp = proj[:, 512:1024][:, pc]
            vp = proj[:, 1024:1536][:, pc]
            q_aug = jnp.where(own, qp, jnp.where(at(0), hi, jnp.where(at(1), mid, jnp.where(at(2), lo,
                              jnp.where(span(3, 6), 1.0, 0.0)))))
            k_aug = jnp.where(own, kp, jnp.where(span(0, 3), 1.0, jnp.where(at(3), -hi, jnp.where(at(4), -mid,
                              jnp.where(at(5), -lo, jnp.where(span(6, 9), -1.0, 0.0))))))
            v_aug = jnp.where(own, vp, jnp.where(span(0, 3), 1.0, 0.0))
            q_ref[:, hc] = q_aug.astype(BF16)
            k_ref[:, hc] = k_aug.astype(BF16)
            v_ref[:, hc] = v_aug.astype(BF16)

    row = lambda i: (i, 0)
    full = lambda i: (0, 0)
    return pl.pallas_call(
        body, name="inproj_fwd", grid=(ns,),
        out_shape=(jax.ShapeDtypeStruct((s, N_HEADS * 128), BF16),) * 3 + (jax.ShapeDtypeStruct((s, 512), F32),) * 2
        + (jax.ShapeDtypeStruct((s, 128), F32),),
        in_specs=[pl.BlockSpec((ts, D_MODEL), row), pl.BlockSpec((6, D_MODEL), full),
                  pl.BlockSpec((D_MODEL, N_IN_PAD), full), pl.BlockSpec((1, 128), full)],
        out_specs=(pl.BlockSpec((ts, N_HEADS * 128), row),) * 3 + (pl.BlockSpec((ts, 512), row),) * 2
        + (pl.BlockSpec((ts, 128), row),),
        scratch_shapes=[pltpu.VMEM((1, 128), F32)],
        compiler_params=_cparams(1),
    )(x, ada, w_p, bf)


def _attn_fwd(qa, ka, va, shards, tq):
    s = qa.shape[0]
    nq = s // tq
    tk = tq
    n = len(shards)

    def body(q_ref, k_ref, v_ref, *rest):
        sh_in, (o_ref, qb_ref), sh_out = rest[:n], rest[n:n + 2], rest[n + 2:2 * n + 2]
        lsem, ssem, rsem = rest[2 * n + 2:]
        pair = pl.program_id(0)
        i = pl.program_id(1)

        @pl.when((pair == 0) & (i == 0))
        def _():
            _gather_copies(sh_in, sh_out, lsem, ssem, rsem).start()

        lane = lax.broadcasted_iota(jnp.int32, (tq, 128), 1)
        t_off = lax.broadcasted_iota(jnp.int32, (tq, tk), 0)
        s_off = lax.broadcasted_iota(jnp.int32, (tq, tk), 1)

        def block(j, carry, diagonal):
            start = pl.multiple_of(j * tk, tk)
            out = []
            for hh in range(2):
                cols = slice(128 * hh, 128 * hh + 128)
                m, acc = carry[2 * hh:2 * hh + 2]
                sc = _dot_nt(q_ref[:, cols], k_ref[pl.ds(start, tk), cols])
                if diagonal:
                    sc = jnp.where(s_off <= t_off, sc, -jnp.inf)
                m_new = jnp.maximum(m, jnp.max(sc, axis=-1, keepdims=True))
                pr = jnp.exp(sc - m_new).astype(BF16)
                acc = acc * jnp.exp(m - m_new) + _dot(pr, v_ref[pl.ds(start, tk), cols])
                out += [m_new, acc]
            return tuple(out)

        init = (jnp.full((tq, 1), -jnp.inf, F32), jnp.zeros((tq, 128), F32)) * 2
        res = lax.fori_loop(0, i, functools.partial(block, diagonal=False), init)
        res = block(i, res, diagonal=True)
        outs = []
        for hh in range(2):
            cols = slice(128 * hh, 128 * hh + 128)
            _, at, _ = _aug_masks(lane, hh)
            a0 = HEAD_DIM if hh == 0 else 0
            m, acc = res[2 * hh:2 * hh + 2]
            denom = acc[:, a0:a0 + 1]
            outs.append(acc / denom)
            hi, mid, lo = _split3(m + jnp.log(denom))
            qb_ref[:, cols] = jnp.where(at(6), hi, jnp.where(at(7), mid, jnp.where(at(8), lo, q_ref[:, cols])))
        o_ref[...] = jnp.where(lane < HEAD_DIM, outs[0], outs[1])

        @pl.when((pair == N_PAIRS - 1) & (i == nq - 1))
        def _():
            _gather_copies(sh_in, sh_out, lsem, ssem, rsem).wait()

    res = pl.pallas_call(
        body, name="attn_fwd", grid=(N_PAIRS, nq),
        out_shape=(jax.ShapeDtypeStruct((s, 512), F32), jax.ShapeDtypeStruct((s, N_HEADS * 128), BF16))
        + tuple(jax.ShapeDtypeStruct((N_SHARD,) + w.shape, w.dtype) for w in shards),
        in_specs=[pl.BlockSpec((tq, 256), lambda p, i: (i, p)), pl.BlockSpec((s, 256), lambda p, i: (0, p)),
                  pl.BlockSpec((s, 256), lambda p, i: (0, p))] + [ANY] * n,
        out_specs=(pl.BlockSpec((tq, 128), lambda p, i: (i, p)), pl.BlockSpec((tq, 256), lambda p, i: (i, p)))
        + (ANY,) * n,
        scratch_shapes=_copy_sems(n),
        compiler_params=_cparams(2),
    )(qa, ka, va, *shards)
    return res[0], res[1], res[2:]


def _fill_shifts(buf, shifted, ts):
    for s in range(1, 8):
        shifted[s - 1] = buf[s:s + ts + HALO - 8, :]


def _rows_at(buf, shifted, off, ts):
    s = off % 8
    if s == 0:
        return buf[off:off + ts, :]
    return shifted[s - 1, off - s:off - s + ts, :]


def _shift_scratch(ts):
    return pltpu.VMEM((7, ts + HALO - 8, 512), F32)


def _conv_branch(a, g, ah, gh, first, ugx, ush, wdw_ref, prm_ref, gm, ts):
    sg_g = jax.nn.sigmoid(g)
    ug = (a * sg_g).astype(BF16).astype(F32)
    ugh = jnp.where(first, 0.0, (ah * jax.nn.sigmoid(gh)).astype(BF16).astype(F32))
    ugx[0:HALO, :] = ugh
    ugx[HALO:HALO + ts, :] = ug
    _fill_shifts(ugx, ush, ts)
    y = jnp.zeros((ts, CONV_W), F32) + prm_ref[0:1, :]
    for kk in range(CONV_K):
        off = HALO - (CONV_K - 1) + kk
        y = y + wdw_ref[kk:kk + 1, :] * _rows_at(ugx, ush, off, ts)
    mu = _dot_f32(y, gm)
    d = y - mu
    var = _dot_f32(d * d, gm)
    rs = lax.rsqrt(var + LN_EPS)
    yhat = d * rs
    yn = yhat * prm_ref[1:2, :] + prm_ref[2:3, :]
    sg = jax.nn.sigmoid(yn)
    co = yn * sg
    return sg_g, rs, yhat, yn, sg, co


def _mix_inputs(o, co, prm_ref):
    ra = lax.rsqrt(_mean_last(o * o) + LN_EPS)
    oh = o * ra
    rc = lax.rsqrt(_mean_last(co * co) + LN_EPS)
    ch = co * rc
    mi = jnp.concatenate([oh * prm_ref[3:4, :], ch * prm_ref[4:5, :]], axis=-1).astype(BF16)
    return ra, oh, rc, ch, mi


def _layernorm_stats(r):
    mu = _mean_last(r)
    d = r - mu
    rstd = lax.rsqrt(_mean_last(d * d) + LN_EPS)
    return d * rstd, rstd


def _layernorm_bwd(dout, xh, rstd, gain):
    dxh = dout * gain
    return rstd * (dxh - _mean_last(dxh) - xh * _mean_last(dxh * xh))


def _halo_index(tile, ts):
    return jnp.maximum(tile * (ts // HALO) - 1, 0)


def _mid_fwd(a, g, o, x, ada, wdw, prm, ln1, w_out, gm, ts):
    s = x.shape[0]
    ns = s // ts

    def body(a_ref, g_ref, ah_ref, gh_ref, o_ref, x_ref, ada_ref, wdw_ref, prm_ref, ln_ref, wo_ref, gm_ref,
             x1_ref, ugx, ush):
        i = pl.program_id(0)
        co = _conv_branch(a_ref[...], g_ref[...], ah_ref[...], gh_ref[...], i == 0, ugx, ush, wdw_ref, prm_ref,
                          gm_ref[...], ts)[-1]
        mi = _mix_inputs(o_ref[...], co, prm_ref)[-1]
        mixed = _dot(mi, wo_ref[...])
        r1 = ALPHA * x_ref[...] + (1.0 + ada_ref[2:3, :]) * mixed
        xh, _ = _layernorm_stats(r1)
        x1_ref[...] = xh * ln_ref[0:1, :] + ln_ref[1:2, :]

    row = lambda i: (i, 0)
    full = lambda i: (0, 0)
    halo = lambda i: (_halo_index(i, ts), 0)
    return pl.pallas_call(
        body, name="mid_fwd", grid=(ns,),
        out_shape=jax.ShapeDtypeStruct((s, D_MODEL), F32),
        in_specs=[pl.BlockSpec((ts, 512), row), pl.BlockSpec((ts, 512), row),
                  pl.BlockSpec((HALO, 512), halo), pl.BlockSpec((HALO, 512), halo),
                  pl.BlockSpec((ts, 512), row), pl.BlockSpec((ts, D_MODEL), row),
                  pl.BlockSpec((6, D_MODEL), full), pl.BlockSpec((32, 512), full), pl.BlockSpec((8, 512), full),
                  pl.BlockSpec((2, D_MODEL), full), pl.BlockSpec((D_MODEL, D_MODEL), full),
                  pl.BlockSpec((512, 512), full)],
        out_specs=pl.BlockSpec((ts, D_MODEL), row),
        scratch_shapes=[pltpu.VMEM((ts + HALO, 512), F32), _shift_scratch(ts)],
        compiler_params=_cparams(1),
    )(a, g, a, g, o, x, ada, wdw, prm, ln1, w_out, gm)


def _ffn_fwd(x1, ada, w1, w2, ln2, tgt, ts):
    s = x1.shape[0]
    ns = s // ts
    nf = N_SHARD

    def body(x1_ref, ada_ref, w1_ref, w2_ref, ln_ref, t_ref, dff_ref, dx1_ref, pg_ref, ffacc, u2):
        i = pl.program_id(0)
        f = pl.program_id(1)

        @pl.when((i == 0) & (f == 0))
        def _():
            pg_ref[...] = jnp.zeros_like(pg_ref)

        @pl.when(f == 0)
        def _():
            u2[...] = (x1_ref[...] * (1.0 + ada_ref[4:5, :]) + ada_ref[3:4, :]).astype(BF16)
            ffacc[...] = jnp.zeros_like(ffacc)

        h = _dot(u2[...], w1_ref[0])
        r = jnp.maximum(h, 0.0)
        ffacc[...] += _dot((r * r).astype(BF16), w2_ref[0])

        @pl.when(f == nf - 1)
        def _():
            ff = ffacc[...]
            r2 = ALPHA * x1_ref[...] + (1.0 + ada_ref[5:6, :]) * ff
            xh, rstd = _layernorm_stats(r2)
            yv = xh * ln_ref[0:1, :] + ln_ref[1:2, :]
            err = yv - t_ref[...]
            dy = err * (1.0 / D_MODEL)
            dr2 = _layernorm_bwd(dy, xh, rstd, ln_ref[0:1, :])
            pg_ref[0:1, :] += _rowsum(dy * xh)
            pg_ref[1:2, :] += _rowsum(dy)
            pg_ref[2:3, :] += _rowsum(dr2 * ff)
            pg_ref[3:4, :] += _rowsum(err * err) * (0.5 / D_MODEL)
            dff_ref[...] = ((1.0 + ada_ref[5:6, :]) * dr2).astype(BF16)
            dx1_ref[...] = ALPHA * dr2

    row = lambda i, f: (i, 0)
    full = lambda i, f: (0, 0)
    chunk = lambda i, f: (f, 0, 0)
    return pl.pallas_call(
        body, name="ffn_fwd", grid=(ns, nf),
        out_shape=(jax.ShapeDtypeStruct((s, D_MODEL), BF16), jax.ShapeDtypeStruct((s, D_MODEL), F32),
                   jax.ShapeDtypeStruct((8, D_MODEL), F32)),
        in_specs=[pl.BlockSpec((ts, D_MODEL), row), pl.BlockSpec((6, D_MODEL), full),
                  pl.BlockSpec((1, D_MODEL, FF_CHUNK), chunk), pl.BlockSpec((1, FF_CHUNK, D_MODEL), chunk),
                  pl.BlockSpec((2, D_MODEL), full), pl.BlockSpec((ts, D_MODEL), row)],
        out_specs=(pl.BlockSpec((ts, D_MODEL), row), pl.BlockSpec((ts, D_MODEL), row),
                   pl.BlockSpec((8, D_MODEL), full)),
        scratch_shapes=[pltpu.VMEM((ts, D_MODEL), F32), pltpu.VMEM((ts, D_MODEL), BF16)],
        compiler_params=_cparams(2),
    )(x1, ada, w1, w2, ln2, tgt)


def _ffn_bwd_chunk(f, x1, ada, w1, w2, dff, dx1, ts):
    s = x1.shape[0]
    ns = s // ts

    def body(x1_ref, ada_ref, w1_ref, w2_ref, dff_ref, dx1_in, dx1_out, dw1_ref, dw2_ref, pg_ref):
        i = pl.program_id(0)

        @pl.when(i == 0)
        def _():
            pg_ref[...] = jnp.zeros_like(pg_ref)
            dw1_ref[...] = jnp.zeros_like(dw1_ref)
            dw2_ref[...] = jnp.zeros_like(dw2_ref)

        x1v = x1_ref[...]
        u2 = (x1v * (1.0 + ada_ref[4:5, :]) + ada_ref[3:4, :]).astype(BF16)
        h = _dot(u2, w1_ref[0])
        r = jnp.maximum(h, 0.0)
        hid = (r * r).astype(BF16)
        dffv = dff_ref[...]
        dh = (_dot_nt(dffv, w2_ref[0]) * (2.0 * r)).astype(BF16)
        dw2_ref[0] += _dot_tn(hid, dffv)
        dw1_ref[0] += _dot_tn(u2, dh)
        du2 = _dot_nt(dh, w1_ref[0])
        dx1_out[...] = dx1_in[...] + du2 * (1.0 + ada_ref[4:5, :])
        pg_ref[0:1, :] += _rowsum(du2 * x1v)
        pg_ref[1:2, :] += _rowsum(du2)

    row = lambda i: (i, 0)
    full = lambda i: (0, 0)
    full3 = lambda i: (0, 0, 0)
    chunk = lambda i: (f, 0, 0)
    return pl.pallas_call(
        body, name=f"ffn_bwd_{f}", grid=(ns,),
        out_shape=(jax.ShapeDtypeStruct((s, D_MODEL), F32), jax.ShapeDtypeStruct((1, D_MODEL, FF_CHUNK), F32),
                   jax.ShapeDtypeStruct((1, FF_CHUNK, D_MODEL), F32), jax.ShapeDtypeStruct((8, D_MODEL), F32)),
        in_specs=[pl.BlockSpec((ts, D_MODEL), row), pl.BlockSpec((6, D_MODEL), full),
                  pl.BlockSpec((1, D_MODEL, FF_CHUNK), chunk), pl.BlockSpec((1, FF_CHUNK, D_MODEL), chunk),
                  pl.BlockSpec((ts, D_MODEL), row), pl.BlockSpec((ts, D_MODEL), row)],
        out_specs=(pl.BlockSpec((ts, D_MODEL), row), pl.BlockSpec((1, D_MODEL, FF_CHUNK), full3),
                   pl.BlockSpec((1, FF_CHUNK, D_MODEL), full3), pl.BlockSpec((8, D_MODEL), full)),
        compiler_params=_cparams(1),
    )(x1, ada, w1, w2, dff, dx1)


def _mid_bwd(a, g, o, x, dx1, ada, wdw, prm, ln1, w_out, gm, sel, ts):
    s = x.shape[0]
    ns = s // ts

    def body(a_ref, g_ref, ah_ref, gh_ref, o_ref, x_ref, dx1_ref, ada_ref, wdw_ref, prm_ref, ln_ref, wo_ref,
             gm_ref, sel_ref,
             do_ref, da_ref, dg_ref, dxa_ref, dwo_ref, pgm_ref, pgc_ref, dwdw_ref,
             ugx, dyx, ush, dsh):
        i = pl.program_id(0)
        tile = ns - 1 - i

        @pl.when(i == 0)
        def _():
            dwo_ref[...] = jnp.zeros_like(dwo_ref)
            pgm_ref[...] = jnp.zeros_like(pgm_ref)
            pgc_ref[...] = jnp.zeros_like(pgc_ref)
            dwdw_ref[...] = jnp.zeros_like(dwdw_ref)
            dyx[ts:ts + HALO, :] = jnp.zeros((HALO, 512), F32)

        gmv = gm_ref[...]
        av = a_ref[...]
        ov = o_ref[...]
        sg_g, rs, yhat, yn, sg, co = _conv_branch(av, g_ref[...], ah_ref[...], gh_ref[...], tile == 0, ugx, ush,
                                                  wdw_ref, prm_ref, gmv, ts)
        ra, oh, rc, ch, mi = _mix_inputs(ov, co, prm_ref)
        mixed = _dot(mi, wo_ref[...])
        gt1 = 1.0 + ada_ref[2:3, :]
        r1 = ALPHA * x_ref[...] + gt1 * mixed
        xh, rstd = _layernorm_stats(r1)
        dx1 = dx1_ref[...]
        pgm_ref[0:1, :] += _rowsum(dx1 * xh)
        pgm_ref[1:2, :] += _rowsum(dx1)
        dr1 = _layernorm_bwd(dx1, xh, rstd, ln_ref[0:1, :])
        dxa_ref[...] = ALPHA * dr1
        pgm_ref[2:3, :] += _rowsum(dr1 * mixed)
        dmixed = (gt1 * dr1).astype(BF16)
        dmi = _dot_nt(dmixed, wo_ref[...])
        dwo_ref[...] += _dot_tn(mi, dmixed)
        dna = dmi[:, 0:512]
        dnc = dmi[:, 512:1024]
        pgc_ref[3:4, :] += _rowsum(dna * oh)
        doh = dna * prm_ref[3:4, :]
        do = ra * (doh - oh * _mean_last(doh * oh))
        lane = lax.broadcasted_iota(jnp.int32, (ts, 128), 1)
        delta = _dot_f32(do * ov, sel_ref[...])
        for h in range(N_HEADS):
            own, at, _ = _aug_masks(lane, h)
            hi, mid, lo = _pieces(-delta[:, h:h + 1])
            dop = do[:, 128 * (h // 2):128 * (h // 2) + 128]
            do_ref[:, 128 * h:128 * h + 128] = jnp.where(
                own, dop, jnp.where(at(0), hi, jnp.where(at(1), mid, jnp.where(at(2), lo, 0.0)))).astype(BF16)
        pgc_ref[4:5, :] += _rowsum(dnc * ch)
        dch = dnc * prm_ref[4:5, :]
        dco = rc * (dch - ch * _mean_last(dch * ch))
        dyn = dco * (sg * (1.0 + yn * (1.0 - sg)))
        pgc_ref[1:2, :] += _rowsum(dyn * yhat)
        pgc_ref[2:3, :] += _rowsum(dyn)
        dyh = dyn * prm_ref[1:2, :]
        dy = rs * (dyh - _dot_f32(dyh, gmv) - yhat * _dot_f32(dyh * yhat, gmv))
        pgc_ref[0:1, :] += _rowsum(dy)
        dyr = dy.astype(BF16).astype(F32)
        dyx[0:ts, :] = dyr
        _fill_shifts(dyx, dsh, ts)
        dug = jnp.zeros((ts, CONV_W), F32)
        for kk in range(CONV_K):
            off = HALO - (CONV_K - 1) + kk
            dwdw_ref[kk:kk + 1, :] += _rowsum(dyr * _rows_at(ugx, ush, off, ts))
            back = CONV_K - 1 - kk
            dug = dug + wdw_ref[kk:kk + 1, :] * _rows_at(dyx, dsh, back, ts)
        dyx[ts:ts + HALO, :] = dyr[0:HALO, :]
        da_ref[...] = (dug * sg_g).astype(BF16)
        dg_ref[...] = (dug * av * sg_g * (1.0 - sg_g)).astype(BF16)

    row = lambda i: (ns - 1 - i, 0)
    full = lambda i: (0, 0)
    halo = lambda i: (_halo_index(ns - 1 - i, ts), 0)
    return pl.pallas_call(
        body, name="mid_bwd", grid=(ns,),
        out_shape=(jax.ShapeDtypeStruct((s, N_HEADS * 128), BF16), jax.ShapeDtypeStruct((s, 512), BF16),
                   jax.ShapeDtypeStruct((s, 512), BF16), jax.ShapeDtypeStruct((s, D_MODEL), F32),
                   jax.ShapeDtypeStruct((D_MODEL, D_MODEL), F32),
                   jax.ShapeDtypeStruct((8, D_MODEL), F32), jax.ShapeDtypeStruct((8, 512), F32),
                   jax.ShapeDtypeStruct((32, 512), F32)),
        in_specs=[pl.BlockSpec((ts, 512), row), pl.BlockSpec((ts, 512), row),
                  pl.BlockSpec((HALO, 512), halo), pl.BlockSpec((HALO, 512), halo),
                  pl.BlockSpec((ts, 512), row), pl.BlockSpec((ts, D_MODEL), row), pl.BlockSpec((ts, D_MODEL), row),
                  pl.BlockSpec((6, D_MODEL), full), pl.BlockSpec((32, 512), full), pl.BlockSpec((8, 512), full),
                  pl.BlockSpec((2, D_MODEL), full), pl.BlockSpec((D_MODEL, D_MODEL), full),
                  pl.BlockSpec((512, 512), full), pl.BlockSpec((512, 128), full)],
        out_specs=(pl.BlockSpec((ts, N_HEADS * 128), row), pl.BlockSpec((ts, 512), row), pl.BlockSpec((ts, 512), row),
                   pl.BlockSpec((ts, D_MODEL), row),
                   pl.BlockSpec((D_MODEL, D_MODEL), full), pl.BlockSpec((8, D_MODEL), full),
                   pl.BlockSpec((8, 512), full), pl.BlockSpec((32, 512), full)),
        scratch_shapes=[pltpu.VMEM((ts + HALO, 512), F32), pltpu.VMEM((ts + HALO, 512), F32), _shift_scratch(ts),
                        _shift_scratch(ts)],
        compiler_params=_cparams(1),
    )(a, g, a, g, o, x, dx1, ada, wdw, prm, ln1, w_out, gm, sel)


def _attn_bwd(qb, ka, va, doa, dwo, dw1, dw2, tk):
    s = qb.shape[0]
    nk = s // tk
    tq = tk

    def body(q_ref, do_ref, k_ref, v_ref, dwo_hbm, dw1_hbm, dw2_hbm,
             dq_ref, rs_ref, dk_ref, dv_ref, cs_ref, land_out, land_ff, *sems):
        pair = pl.program_id(0)
        j = pl.program_id(1)

        def exchanges():
            return (_scatter_copies((dwo_hbm,), land_out, _STACK_OUT, *sems[:3]),
                    _scatter_copies((dw1_hbm, dw2_hbm), land_ff, _STACK_FF, *sems[3:]))

        @pl.when((pair == 0) & (j == 0))
        def _():
            for ex in exchanges():
                ex.start()

        @pl.when(j == 0)
        def _():
            dq_ref[...] = jnp.zeros_like(dq_ref)

        @pl.when((pair == 0) & (j == 0))
        def _():
            rs_ref[...] = jnp.zeros_like(rs_ref)
            cs_ref[...] = jnp.zeros_like(cs_ref)

        lane = lax.broadcasted_iota(jnp.int32, (tk, 128), 1)
        low = lane < HEAD_DIM
        t_off = lax.broadcasted_iota(jnp.int32, (tq, tk), 0)
        s_off = lax.broadcasted_iota(jnp.int32, (tq, tk), 1)

        def block(i, carry, diagonal):
            rows_q = pl.ds(pl.multiple_of(i * tq, tq), tq)
            dq_h, out = [], []
            for hh in range(2):
                cols = slice(128 * hh, 128 * hh + 128)
                dk_acc, dv_acc = carry[2 * hh:2 * hh + 2]
                qh = q_ref[rows_q, cols]
                dh = do_ref[rows_q, cols]
                pr = jnp.exp(_dot_nt(qh, k_ref[:, cols]))
                if diagonal:
                    pr = jnp.where(s_off <= t_off, pr, 0.0)
                ds = (pr * _dot_nt(dh, v_ref[:, cols])).astype(BF16)
                dq_h.append(_dot(ds, k_ref[:, cols]))
                out += [dk_acc + _dot_tn(ds, qh), dv_acc + _dot_tn(pr.astype(BF16), dh)]
            dq_ref[rows_q, :] += jnp.where(low, dq_h[0], dq_h[1])
            rs_ref[rows_q, :] += (jnp.where(lane == 2 * pair, dq_h[0][:, HEAD_DIM:HEAD_DIM + 1], 0.0)
                                  + jnp.where(lane == 2 * pair + 1, dq_h[1][:, 0:1], 0.0))
            return tuple(out)

        first = block(j, (jnp.zeros((tk, 128), F32),) * 4, diagonal=True)
        dk0, dv0, dk1, dv1 = lax.fori_loop(j + 1, nk, functools.partial(block, diagonal=False), first)
        dk_ref[...] = jnp.where(low, dk0, dk1).astype(BF16)
        dv_ref[...] = jnp.where(low, dv0, dv1).astype(BF16)
        rows_k = pl.ds(pl.multiple_of(j * tk, tk), tk)
        cs_ref[rows_k, :] += (jnp.where(lane == 2 * pair, dk0[:, HEAD_DIM + 3:HEAD_DIM + 4], 0.0)
                              + jnp.where(lane == 2 * pair + 1, dk1[:, 3:4], 0.0))

        @pl.when((pair == N_PAIRS - 1) & (j == nk - 1))
        def _():
            for ex in exchanges():
                ex.wait()

    whole = lambda p, j: (0, 0)
    return pl.pallas_call(
        body, name="attn_bwd", grid=(N_PAIRS, nk),
        out_shape=(jax.ShapeDtypeStruct((s, 512), F32), jax.ShapeDtypeStruct((s, 128), F32),
                   jax.ShapeDtypeStruct((s, 512), BF16), jax.ShapeDtypeStruct((s, 512), BF16),
                   jax.ShapeDtypeStruct((s, 128), F32), jax.ShapeDtypeStruct(dwo.shape, F32),
                   jax.ShapeDtypeStruct((N_SHARD, 2 * FF_CHUNK, D_MODEL), F32)),
        in_specs=[pl.BlockSpec((s, 256), lambda p, j: (0, p)), pl.BlockSpec((s, 256), lambda p, j: (0, p)),
                  pl.BlockSpec((tk, 256), lambda p, j: (j, p)), pl.BlockSpec((tk, 256), lambda p, j: (j, p)),
                  ANY, ANY, ANY],
        out_specs=(pl.BlockSpec((s, 128), lambda p, j: (0, p)), pl.BlockSpec((s, 128), whole),
                   pl.BlockSpec((tk, 128), lambda p, j: (j, p)), pl.BlockSpec((tk, 128), lambda p, j: (j, p)),
                   pl.BlockSpec((s, 128), whole), ANY, ANY),
        scratch_shapes=_copy_sems(1) + _copy_sems(2),
        compiler_params=_cparams(2),
    )(qb, doa, ka, va, dwo, dw1, dw2)


def _inproj_bwd(x, ada, w_p, dq, dk, dv, da, dg, dfc, drs, logf, dxa, ts):
    s = x.shape[0]
    ns = s // ts

    def body(x_ref, ada_ref, w_ref, dq_ref, dk_ref, dv_ref, da_ref, dg_ref, dfc_ref, drs_ref, lf_ref, dxa_ref,
             gx_ref, dw_ref, pgi_ref, dbf_ref, carry, dw_vm):
        i = pl.program_id(0)

        @pl.when(i == 0)
        def _():
            carry[...] = jnp.zeros_like(carry)
            dw_vm[...] = jnp.zeros_like(dw_vm)
            pgi_ref[...] = jnp.zeros_like(pgi_ref)
            dbf_ref[...] = jnp.zeros_like(dbf_ref)

        r = lax.broadcasted_iota(jnp.int32, (ts, ts), 0)
        cc = lax.broadcasted_iota(jnp.int32, (ts, ts), 1)
        tri = (cc >= r).astype(BF16)
        dlogf = carry[...] + _tri_dot(tri, drs_ref[...] - dfc_ref[...])
        carry[...] = dlogf[0:1, :]
        lane = lax.broadcasted_iota(jnp.int32, (ts, 128), 1)
        dz = jnp.where(lane < N_HEADS, dlogf * (1.0 - jnp.exp(lf_ref[...])), 0.0)
        dbf_ref[0:1, :] += _rowsum(dz)
        dproj = jnp.concatenate(
            [(dq_ref[...] * (HEAD_DIM ** -0.5)).astype(BF16), dk_ref[...], dv_ref[...], da_ref[...], dg_ref[...],
             dz.astype(BF16)], axis=-1)
        xv = x_ref[...]
        sc1 = 1.0 + ada_ref[1:2, :]
        u = (xv * sc1 + ada_ref[0:1, :]).astype(BF16)
        du = _dot_nt(dproj, w_ref[...])
        dw_vm[...] += _dot_tn(u, dproj)
        gx_ref[...] = dxa_ref[...] + du * sc1
        pgi_ref[0:1, :] += _rowsum(du * xv)
        pgi_ref[1:2, :] += _rowsum(du)

        @pl.when(i == ns - 1)
        def _():
            dw_ref[...] = dw_vm[...].astype(BF16)

    row = lambda i: (ns - 1 - i, 0)
    full = lambda i: (0, 0)
    return pl.pallas_call(
        body, name="inproj_bwd", grid=(ns,),
        out_shape=(jax.ShapeDtypeStruct((s, D_MODEL), F32), jax.ShapeDtypeStruct((D_MODEL, N_IN_PAD), BF16),
                   jax.ShapeDtypeStruct((8, D_MODEL), F32), jax.ShapeDtypeStruct((8, 128), F32)),
        in_specs=[pl.BlockSpec((ts, D_MODEL), row), pl.BlockSpec((6, D_MODEL), full),
                  pl.BlockSpec((D_MODEL, N_IN_PAD), full)]
        + [pl.BlockSpec((ts, 512), row)] * 5 + [pl.BlockSpec((ts, 128), row)] * 3
        + [pl.BlockSpec((ts, D_MODEL), row)],
        out_specs=(pl.BlockSpec((ts, D_MODEL), row), pl.BlockSpec((D_MODEL, N_IN_PAD), full),
                   pl.BlockSpec((8, D_MODEL), full), pl.BlockSpec((8, 128), full)),
        scratch_shapes=[pltpu.VMEM((1, 128), F32), pltpu.VMEM((D_MODEL, N_IN_PAD), F32)],
        compiler_params=_cparams(1),
    )(x, ada, w_p, dq, dk, dv, da, dg, dfc, drs, logf, dxa)


def _small_reduce(packed, part):
    rows = part.shape[1]

    def body(p_ref, part_hbm, sum_ref, all_ref, land_hbm, ssem, rsem, lsem2, ssem2, rsem2):
        x, y, c = _position()
        me = 4 * x + 2 * y + c
        exchange = _scatter_copies((part_hbm,), land_hbm, ((0, rows),), lsem2, ssem2, rsem2)
        exchange.start()
        all_ref[me] = p_ref[...]
        sends = []
        for k in range(1, 8):
            peer = (x ^ ((k >> 2) & 1), y ^ ((k >> 1) & 1), c ^ (k & 1))
            cp = pltpu.make_async_remote_copy(
                src_ref=p_ref, dst_ref=all_ref.at[me], send_sem=ssem.at[k], recv_sem=rsem.at[k],
                device_id=peer, device_id_type=MESH)
            cp.start()
            sends.append(cp)
        for k in range(1, 8):
            pltpu.make_async_remote_copy(
                src_ref=p_ref, dst_ref=all_ref.at[me ^ k], send_sem=ssem.at[k], recv_sem=rsem.at[k],
                device_id=(x, y, c), device_id_type=MESH).wait_recv()
        for cp in sends:
            cp.wait_send()
        total = all_ref[0]
        for dev in range(1, 8):
            total = total + all_ref[dev]
        sum_ref[...] = total
        loss = jnp.sum(total[SMALL_ROWS - 1:SMALL_ROWS, :], axis=-1, keepdims=True)
        sum_ref[SMALL_ROWS - 1:SMALL_ROWS, :] = jnp.broadcast_to(loss, (1, D_MODEL))
        exchange.wait()

    vm = pl.BlockSpec(memory_space=pltpu.VMEM)
    return pl.pallas_call(
        body, name="small_reduce",
        out_shape=(jax.ShapeDtypeStruct((SMALL_ROWS, D_MODEL), F32), jax.ShapeDtypeStruct((8, SMALL_ROWS, D_MODEL), F32),
                   jax.ShapeDtypeStruct(part.shape, part.dtype)),
        in_specs=[vm, ANY], out_specs=(vm, vm, ANY),
        scratch_shapes=[pltpu.SemaphoreType.DMA((8,)), pltpu.SemaphoreType.DMA((8,))] + _copy_sems(1),
        compiler_params=pltpu.CompilerParams(vmem_limit_bytes=VMEM_LIMIT),
    )(packed, part)


def _adam_math(gv, wv, mv, vv):
    m_new = B1 * mv + (1.0 - B1) * gv
    v_new = B2 * vv + (1.0 - B2) * (gv * gv)
    m_hat = m_new / (1.0 - B1 ** STEP)
    v_hat = v_new / (1.0 - B2 ** STEP)
    delta = -LR * (m_hat / (jnp.sqrt(v_hat) + ADAM_EPS) + WD * wv)
    return delta, m_new, v_new


def _adamw(gv, wv, mv, vv, name):
    rows, cols = gv.shape
    tr = rows
    for cand in (256, 128, 64, 32, 16, 8):
        if rows % cand == 0 and rows > cand:
            tr = cand
            break

    def body(g_ref, w_ref, m_ref, v_ref, d_ref, mo_ref, vo_ref):
        d_ref[...], mo_ref[...], vo_ref[...] = _adam_math(g_ref[...], w_ref[...], m_ref[...], v_ref[...])

    spec = pl.BlockSpec((tr, cols), lambda i: (i, 0))
    return pl.pallas_call(
        body, name=name, grid=(rows // tr,),
        out_shape=(jax.ShapeDtypeStruct((rows, cols), F32),) * 3,
        in_specs=[spec] * 4, out_specs=(spec,) * 3,
        compiler_params=_cparams(1),
    )(gv, wv, mv, vv)


def _w_ada_update(sct, dd, wv, mv, vv):
    rows, cols = wv.shape
    tr = 128

    def body(s_ref, d_ref, w_ref, m_ref, v_ref, g_ref, dl_ref, mo_ref, vo_ref):
        sv = s_ref[...]
        dv = d_ref[...]
        gv = sv[:, 0:1] * dv[0:1, :]
        for b in range(1, 8):
            gv = gv + sv[:, b:b + 1] * dv[b:b + 1, :]
        g_ref[...] = gv
        dl_ref[...], mo_ref[...], vo_ref[...] = _adam_math(gv, w_ref[...], m_ref[...], v_ref[...])

    spec = pl.BlockSpec((tr, cols), lambda i: (i, 0))
    return pl.pallas_call(
        body, name="w_ada_update", grid=(rows // tr,),
        out_shape=(jax.ShapeDtypeStruct((rows, cols), F32),) * 4,
        in_specs=[pl.BlockSpec((tr, 8), lambda i: (i, 0)), pl.BlockSpec((8, cols), lambda i: (0, 0))] + [spec] * 3,
        out_specs=(spec,) * 4,
        compiler_params=_cparams(1),
    )(sct, dd, wv, mv, vv)


def _sum_chips(land_in, land_out, land_ff):
    tr = 256
    n_in, n_out = 768 // tr, 256 // tr

    def body(in_ref, out_ref, ff_ref, s_ref):
        i = pl.program_id(0)

        def total(ref):
            s_ref[...] = ((ref[0].astype(F32) + ref[1].astype(F32)) + ref[2].astype(F32)) + ref[3].astype(F32)

        pl.when(i < n_in)(lambda: total(in_ref))
        pl.when((i >= n_in) & (i < n_in + n_out))(lambda: total(out_ref))
        pl.when(i >= n_in + n_out)(lambda: total(ff_ref))

    return pl.pallas_call(
        body, name="sum_chips", grid=(STACK_ROWS // tr,),
        out_shape=jax.ShapeDtypeStruct((STACK_ROWS, D_MODEL), F32),
        in_specs=[pl.BlockSpec((N_SHARD, tr, D_MODEL), lambda i: (0, jnp.minimum(i, n_in - 1), 0)),
                  pl.BlockSpec((N_SHARD, tr, D_MODEL), lambda i: (0, jnp.clip(i - n_in, 0, n_out - 1), 0)),
                  pl.BlockSpec((N_SHARD, tr, D_MODEL), lambda i: (0, jnp.maximum(i - n_in - n_out, 0), 0))],
        out_specs=pl.BlockSpec((tr, D_MODEL), lambda i: (i, 0)),
        compiler_params=_cparams(1),
    )(land_in, land_out, land_ff)


def _core_swap(part):
    def body(p_ref, o_ref, ssem, rsem):
        x, y, c = _position()
        cp = pltpu.make_async_remote_copy(src_ref=p_ref, dst_ref=o_ref, send_sem=ssem, recv_sem=rsem,
                                          device_id=(x, y, 1 - c), device_id_type=MESH)
        cp.start()
        cp.wait()

    return pl.pallas_call(
        body, name="core_swap",
        out_shape=jax.ShapeDtypeStruct(part.shape, part.dtype),
        in_specs=[ANY], out_specs=ANY,
        scratch_shapes=[pltpu.SemaphoreType.DMA, pltpu.SemaphoreType.DMA],
    )(part)


def _add_pair(mine, other):
    tr = 256

    def body(a_ref, b_ref, o_ref):
        o_ref[...] = a_ref[...] + b_ref[...]

    spec = pl.BlockSpec((tr, D_MODEL), lambda i: (i, 0))
    return pl.pallas_call(
        body, name="add_pair", grid=(STACK_ROWS // tr,),
        out_shape=jax.ShapeDtypeStruct((STACK_ROWS, D_MODEL), F32),
        in_specs=[spec, spec], out_specs=spec,
        compiler_params=_cparams(1),
    )(mine, other)


def _pad_lanes(v, width=D_MODEL):
    v = v.reshape(1, -1)
    return jnp.pad(v, ((0, 0), (0, width - v.shape[1])))


def _pack_small(b_ada, ln1_g, ln1_b, ln2_g, ln2_b, b_dw, gn_g, gn_b, g_attn, g_conv, b_forget, w_dw_full, last):
    rows = [b_ada.reshape(6, D_MODEL)] + [_pad_lanes(v) for v in
                                          (ln1_g, ln1_b, ln2_g, ln2_b, b_dw, gn_g, gn_b, g_attn, g_conv, b_forget)]
    rows.append(jnp.pad(w_dw_full.reshape(CONV_K, -1), ((0, 0), (0, D_MODEL - w_dw_full.reshape(CONV_K, -1).shape[1]))))
    rows.append(_pad_lanes(last))
    return jnp.concatenate(rows, axis=0)


def _unpack_small(p):
    return dict(b_ada=p[0:6].reshape(1, 6 * D_MODEL), ln1_g=p[6:7], ln1_b=p[7:8], ln2_g=p[8:9], ln2_b=p[9:10],
                b_dw=p[10:11, :512], gn_g=p[11:12, :512], gn_b=p[12:13, :512], g_attn_out=p[13:14, :512],
                g_conv_out=p[14:15, :512], b_forget=p[15:16, :N_HEADS])


def kernel(x, c, w_ada, b_ada, w_in, b_forget, w_dw, b_dw, gn_g, gn_b, g_attn_out, g_conv_out, w_out, ln1_g, ln1_b, w_ff1, w_ff2, ln2_g, ln2_b, loss_target, m_w_ada, m_b_ada, m_w_in, m_b_forget, m_w_dw, m_b_dw, m_gn_g, m_gn_b, m_g_attn_out, m_g_conv_out, m_w_out, m_ln1_g, m_ln1_b, m_w_ff1, m_w_ff2, m_ln2_g, m_ln2_b, v_w_ada, v_b_ada, v_w_in, v_b_forget, v_w_dw, v_b_dw, v_gn_g, v_gn_b, v_g_attn_out, v_g_conv_out, v_w_out, v_ln1_g, v_ln1_b, v_w_ff1, v_w_ff2, v_ln2_g, v_ln2_b):
    seq = x.shape[1]
    ts = min(512, seq // 2)
    tq = min(512, seq // 2)
    ts_mid = min(256, seq // 2)
    q_idx = 2 * lax.axis_index("x") + lax.axis_index("y")
    xs = x[0]
    tgt = loss_target[0]

    w_in_sh = jnp.pad(w_in[0], ((0, 0), (0, IN_SHARD_PAD - IN_SHARD))).astype(BF16)
    wdw_rows = jnp.pad(w_dw[0, :, 0, :], ((0, 1), (0, 0)))
    sc_all, ada, (win_all, wdw_all) = _ada_fwd(c, w_ada[0], b_ada, [w_in_sh, wdw_rows])
    w_in_full = jnp.transpose(win_all[:, :, :IN_SHARD], (1, 0, 2)).reshape(D_MODEL, N_IN)
    w_p = jnp.concatenate([w_in_full[:, 0:1536], w_in_full[:, 1544:2568], w_in_full[:, 1536:1544],
                           jnp.zeros((D_MODEL, 120), BF16)], axis=1)
    bf = _pad_lanes(b_forget, 128)
    wdw_full = lax.reduce_precision(jnp.transpose(wdw_all, (1, 0, 2)).reshape(32, 512), 8, 7)

    prm = jnp.concatenate([b_dw, gn_g, gn_b, g_attn_out, g_conv_out, jnp.zeros((3, 512), F32)], axis=0)
    ln1 = jnp.concatenate([ln1_g, ln1_b], axis=0)
    ln2 = jnp.concatenate([ln2_g, ln2_b], axis=0)
    ch = jnp.arange(512)
    gm = ((ch[:, None] // HEAD_DIM == ch[None, :] // HEAD_DIM).astype(F32) / HEAD_DIM).astype(BF16)
    sel = (ch[:, None] // HEAD_DIM == jnp.arange(128)[None, :]).astype(BF16)

    qa, ka, va, a, g, logf = _inproj_fwd(xs, ada, w_p, bf, ts)
    o, qb, (wout_all, w1_all, w2_all) = _attn_fwd(
        qa, ka, va, [w_out[0].astype(BF16), w_ff1[0].astype(BF16), w_ff2[0].astype(BF16)], tq)
    w_out_full = wout_all.reshape(D_MODEL, D_MODEL)
    x1 = _mid_fwd(a, g, o, xs, ada, wdw_full, prm, ln1, w_out_full, gm, ts_mid)
    dff, dx1, pg_f = _ffn_fwd(x1, ada, w1_all, w2_all, ln2, tgt, ts)

    dw1, dw2, pg_b = [], [], jnp.zeros((8, D_MODEL), F32)
    for f in range(N_SHARD):
        dx1, dw1_f, dw2_f, pg_bf = _ffn_bwd_chunk(f, x1, ada, w1_all, w2_all, dff, dx1, ts)
        dw1.append(dw1_f)
        dw2.append(dw2_f)
        pg_b = pg_b + pg_bf
    dw1 = jnp.concatenate(dw1, axis=0)
    dw2 = jnp.concatenate(dw2, axis=0)
    doa, da, dg, dxa, dwo, pgm, pgc, dwdw = _mid_bwd(
        a, g, o, xs, dx1, ada, wdw_full, prm, ln1, w_out_full, gm, sel, ts_mid)
    dq, drs, dk, dv, dfc, land_out, land_ff = _attn_bwd(qb, ka, va, doa, dwo.reshape(N_SHARD, 256, D_MODEL),
                                                         dw1, dw2, tq)
    gx, dwp, pgi, dbf = _inproj_bwd(xs, ada, w_p, dq, dk, dv, da, dg, dfc, drs, logf, dxa, ts)

    d_ada = jnp.concatenate([pgi[1:2], pgi[0:1], pgm[2:3], pg_b[1:2], pg_b[0:1], pg_f[2:3]], axis=0)
    packed = _pack_small(d_ada, pgm[0:1], pgm[1:2], pg_f[0:1], pg_f[1:2], pgc[0:1], pgc[1:2], pgc[2:3], pgc[3:4],
                         pgc[4:5], dbf[0:1, :N_HEADS], dwdw[0:CONV_K], pg_f[3:4])
    dw_in_cols = jnp.concatenate([dwp[:, 0:1536], dwp[:, 2560:2568], dwp[:, 1536:2560]], axis=1)
    dw_in_sh = jnp.pad(jnp.transpose(dw_in_cols.reshape(D_MODEL, N_SHARD, IN_SHARD), (1, 0, 2)),
                       ((0, 0), (0, 0), (0, IN_SHARD_PAD - IN_SHARD))).reshape(N_SHARD, 768, D_MODEL)
    small_sum, small_all, land_in = _small_reduce(packed, dw_in_sh)
    loss = small_sum[SMALL_ROWS - 1, 0]
    gsm = _unpack_small(small_sum)
    g_wdw = lax.dynamic_slice(small_sum[16:16 + CONV_K, :512], (0, q_idx * 128), (CONV_K, 128))

    zrow = jnp.zeros((CONV_K + 1, D_MODEL), F32)
    w_small = _pack_small(b_ada, ln1_g, ln1_b, ln2_g, ln2_b, b_dw, gn_g, gn_b, g_attn_out,
                          g_conv_out, b_forget, zrow[:CONV_K, :512], zrow[0])
    m_small = _pack_small(m_b_ada, m_ln1_g, m_ln1_b, m_ln2_g, m_ln2_b, m_b_dw, m_gn_g, m_gn_b, m_g_attn_out,
                          m_g_conv_out, m_b_forget, zrow[:CONV_K, :512], zrow[0])
    v_small = _pack_small(v_b_ada, v_ln1_g, v_ln1_b, v_ln2_g, v_ln2_b, v_b_dw, v_gn_g, v_gn_b, v_g_attn_out,
                          v_g_conv_out, v_b_forget, zrow[:CONV_K, :512], zrow[0])
    d_small, mn_small, vn_small = (_unpack_small(t) for t in _adamw(small_sum, w_small, m_small, v_small, "adamw_small"))
    d_wdw, mn_wdw, vn_wdw = _adamw(g_wdw, w_dw[0, :, 0, :], m_w_dw[0, :, 0, :], v_w_dw[0, :, 0, :], "adamw_wdw")

    dd = lax.dynamic_slice(small_all[:, 0:6, :].reshape(8, 6 * D_MODEL), (0, q_idx * 1536), (8, 1536))
    g_wada, d_wada, mn_wada, vn_wada = _w_ada_update(sc_all.T, dd, w_ada[0], m_w_ada[0], v_w_ada[0])

    part = _sum_chips(land_in, land_out, land_ff)
    total = _add_pair(part, _core_swap(part))
    g_win = total[0:768].reshape(D_MODEL, IN_SHARD_PAD)[:, :IN_SHARD]
    g_wout = total[768:1024]
    g_w1 = total[1024:2048]
    g_w2 = total[2048:3072]
    d_win, mn_win, vn_win = _adamw(g_win, w_in[0], m_w_in[0], v_w_in[0], "adamw_w_in")
    d_wout, mn_wout, vn_wout = _adamw(g_wout, w_out[0], m_w_out[0], v_w_out[0], "adamw_w_out")
    d_w1, mn_w1, vn_w1 = _adamw(g_w1, w_ff1[0], m_w_ff1[0], v_w_ff1[0], "adamw_w_ff1")
    d_w2, mn_w2, vn_w2 = _adamw(g_w2, w_ff2[0], m_w_ff2[0], v_w_ff2[0], "adamw_w_ff2")

    def group(wada, sm, win, wdw, wout, w1, w2):
        return (wada[None], sm["b_ada"], win[None], sm["b_forget"], wdw[None, :, None, :], sm["b_dw"], sm["gn_g"],
                sm["gn_b"], sm["g_attn_out"], sm["g_conv_out"], wout[None], sm["ln1_g"], sm["ln1_b"], w1[None],
                w2[None], sm["ln2_g"], sm["ln2_b"])

    return ((loss, gx[None])
            + group(g_wada, gsm, g_win, g_wdw, g_wout, g_w1, g_w2)
            + group(d_wada, d_small, d_win, d_wdw, d_wout, d_w1, d_w2)
            + group(mn_wada, mn_small, mn_win, mn_wdw, mn_wout, mn_w1, mn_w2)
            + group(vn_wada, vn_small, vn_win, vn_wdw, vn_wout, vn_w1, vn_w2))
```

```python
import functools

import jax
import jax.numpy as jnp
from jax import lax
from jax.experimental import pallas as pl
from jax.experimental.pallas import tpu as pltpu

F32 = jnp.float32
BF16 = jnp.bfloat16
MESH = pl.DeviceIdType.MESH
ANY = pl.BlockSpec(memory_space=pl.ANY)

D_MODEL = 1024
HEAD_DIM = 64
ATTN_W = 512
CONV_W = 512
N_HEADS = 8
N_PAIRS = 4
CONV_K = 31
HALO = 32
D_FF = 4096
N_SHARD = 4
FF_CHUNK = D_FF // N_SHARD
N_IN = 2568
IN_SHARD = N_IN // N_SHARD
IN_SHARD_PAD = 768
N_IN_PAD = 5 * 512 + 128
LN_EPS = 1e-5
ALPHA = 2.0 ** 0.25
LR, B1, B2, ADAM_EPS, WD, STEP = 0.001, 0.9, 0.999, 1e-08, 0.01, 10
VMEM_LIMIT = 56 * 1024 * 1024
SMALL_ROWS = 48
STACK_ROWS = 768 + 256 + 1024 + 1024
_STACK_OUT = ((0, 256),)
_STACK_FF = ((0, 1024), (1024, 1024))


def _cparams(n_axes):
    return pltpu.CompilerParams(dimension_semantics=("arbitrary",) * n_axes, vmem_limit_bytes=VMEM_LIMIT)


def _dot(a, b):
    return jnp.dot(a, b, preferred_element_type=F32)


def _dot_nt(a, b):
    return lax.dot_general(a, b, (((1,), (1,)), ((), ())), preferred_element_type=F32)


def _dot_tn(a, b):
    return lax.dot_general(a, b, (((0,), (0,)), ((), ())), preferred_element_type=F32)


def _dot_f32(a, b):
    hi, mid, lo = _split3(a)
    return _dot(hi, b) + _dot(mid, b) + _dot(lo, b)


def _split3(x):
    hi = x.astype(BF16)
    r = x - hi.astype(F32)
    mid = r.astype(BF16)
    lo = (r - mid.astype(F32)).astype(BF16)
    return hi, mid, lo


def _tri_dot(tri, x):
    hi, mid, lo = _split3(x)
    return _dot(tri, hi) + _dot(tri, mid) + _dot(tri, lo)


def _rowsum(x):
    return jnp.sum(x, axis=0, keepdims=True)


def _mean_last(x):
    return jnp.mean(x, axis=-1, keepdims=True)


def _position():
    x, y, c = lax.axis_index("x"), lax.axis_index("y"), lax.axis_index("c")
    return x, y, c


def _ada_fwd(c_row, w_ada, b_ada, shards):
    n_col = w_ada.shape[1]
    n = len(shards)

    def body(c_ref, w_ref, b_ref, *rest):
        sh_in, (sc_ref, ada_ref), sh_out = rest[:n], rest[n:n + 2], rest[n + 2:2 * n + 2]
        call_ref, part_ref, pall_ref, s1, r1, s2, r2, lsem, ssem, rsem = rest[2 * n + 2:]
        x, y, c = _position()
        me = 4 * x + 2 * y + c
        q = 2 * x + y
        gather = _gather_copies(sh_in, sh_out, lsem, ssem, rsem)
        gather.start()
        call_ref[me] = jnp.broadcast_to(c_ref[...], (8, D_MODEL))

        def c_copy(k):
            peer = (x ^ ((k >> 2) & 1), y ^ ((k >> 1) & 1), c ^ (k & 1))
            return pltpu.make_async_remote_copy(
                src_ref=call_ref.at[me], dst_ref=call_ref.at[me], send_sem=s1.at[k], recv_sem=r1.at[k],
                device_id=peer, device_id_type=MESH)

        def c_recv(k):
            src = me ^ k
            return pltpu.make_async_remote_copy(
                src_ref=call_ref.at[src], dst_ref=call_ref.at[src], send_sem=s1.at[k], recv_sem=r1.at[k],
                device_id=(x, y, c), device_id_type=MESH)

        sends = [c_copy(k) for k in range(1, 8)]
        for cp in sends:
            cp.start()
        for k in range(1, 8):
            c_recv(k).wait_recv()
        for cp in sends:
            cp.wait_send()

        row = lax.broadcasted_iota(jnp.int32, (8, D_MODEL), 0)
        c_all = jnp.zeros((8, D_MODEL), F32)
        for j in range(8):
            c_all = jnp.where(row == j, call_ref[j], c_all)
        sc_all = c_all * jax.nn.sigmoid(c_all)
        sc_ref[...] = sc_all
        b_slice = b_ref[:, pl.ds(pl.multiple_of(q * n_col, 128), n_col)]
        part = _dot(sc_all.astype(BF16), w_ref[...].astype(BF16)) + b_slice
        part_ref[...] = part
        pall_ref[q] = part

        def p_copy(j):
            peer = (x ^ ((j >> 1) & 1), y ^ (j & 1), c)
            return pltpu.make_async_remote_copy(
                src_ref=part_ref, dst_ref=pall_ref.at[q], send_sem=s2.at[j], recv_sem=r2.at[j],
                device_id=peer, device_id_type=MESH)

        def p_recv(j):
            src_q = q ^ j
            return pltpu.make_async_remote_copy(
                src_ref=part_ref, dst_ref=pall_ref.at[src_q], send_sem=s2.at[j], recv_sem=r2.at[j],
                device_id=(x, y, c), device_id_type=MESH)

        sends2 = [p_copy(j) for j in range(1, 4)]
        for cp in sends2:
            cp.start()
        for j in range(1, 4):
            p_recv(j).wait_recv()
        for cp in sends2:
            cp.wait_send()
        for qq in range(N_SHARD):
            ada_ref[qq] = pall_ref[qq, pl.ds(me, 1), :]
        gather.wait()

    vm = pl.BlockSpec(memory_space=pltpu.VMEM)
    res = pl.pallas_call(
        body, name="ada_fwd",
        out_shape=(jax.ShapeDtypeStruct((8, D_MODEL), F32), jax.ShapeDtypeStruct((N_SHARD, 1, n_col), F32))
        + tuple(jax.ShapeDtypeStruct((N_SHARD,) + w.shape, w.dtype) for w in shards),
        in_specs=[vm, vm, vm] + [ANY] * n, out_specs=(vm, vm) + (ANY,) * n,
        scratch_shapes=[pltpu.VMEM((8, 8, D_MODEL), F32), pltpu.VMEM((8, n_col), F32),
                        pltpu.VMEM((N_SHARD, 8, n_col), F32),
                        pltpu.SemaphoreType.DMA((8,)), pltpu.SemaphoreType.DMA((8,)),
                        pltpu.SemaphoreType.DMA((4,)), pltpu.SemaphoreType.DMA((4,))] + _copy_sems(n),
        compiler_params=pltpu.CompilerParams(vmem_limit_bytes=VMEM_LIMIT),
    )(c_row, w_ada, b_ada, *shards)
    return res[0], res[1].reshape(6, D_MODEL), res[2:]


class _ChipCopies:
    def __init__(self, lsem, ssem, rsem):
        self.x, self.y, self.c = _position()
        self.q = 2 * self.x + self.y
        self.lsem, self.ssem, self.rsem = lsem, ssem, rsem
        self.local, self.send, self.recv = [], [], []

    def _remote(self, a, j, src, dst, peer):
        return pltpu.make_async_remote_copy(src_ref=src, dst_ref=dst, send_sem=self.ssem.at[a, j],
                                            recv_sem=self.rsem.at[a, j], device_id=peer, device_id_type=MESH)

    def add(self, a, own_src, own_dst, src_for, dst_mine, dst_from):
        x, y, c, q = self.x, self.y, self.c, self.q
        self.local.append(pltpu.make_async_copy(own_src, own_dst, self.lsem.at[a]))
        for j in range(1, 4):
            peer = (x ^ ((j >> 1) & 1), y ^ (j & 1), c)
            self.send.append(self._remote(a, j, src_for(q ^ j), dst_mine, peer))
            self.recv.append(self._remote(a, j, own_src, dst_from(q ^ j), (x, y, c)))

    def start(self):
        for cp in self.local + self.send:
            cp.start()

    def wait(self):
        for cp in self.recv:
            cp.wait_recv()
        for cp in self.send:
            cp.wait_send()
        for cp in self.local:
            cp.wait()


def _gather_copies(ins, outs, lsem, ssem, rsem):
    cps = _ChipCopies(lsem, ssem, rsem)
    for a in range(len(ins)):
        cps.add(a, ins[a], outs[a].at[cps.q], lambda chip, a=a: ins[a], outs[a].at[cps.q],
                lambda chip, a=a: outs[a].at[chip])
    return cps


def _scatter_copies(ins, land, offs, lsem, ssem, rsem):
    cps = _ChipCopies(lsem, ssem, rsem)
    for a, (off, rows) in enumerate(offs):
        cps.add(a, ins[a].at[cps.q], land.at[cps.q, pl.ds(off, rows)], lambda chip, a=a: ins[a].at[chip],
                land.at[cps.q, pl.ds(off, rows)], lambda chip, off=off, rows=rows: land.at[chip, pl.ds(off, rows)])
    return cps


def _copy_sems(n):
    return [pltpu.SemaphoreType.DMA((n,)), pltpu.SemaphoreType.DMA((n, 4)), pltpu.SemaphoreType.DMA((n, 4))]


def _aug_masks(lane, h):
    a0 = HEAD_DIM if h % 2 == 0 else 0
    own = (lane < HEAD_DIM) if h % 2 == 0 else (lane >= HEAD_DIM)
    return own, (lambda k: lane == a0 + k), (lambda k0, k1: (lane >= a0 + k0) & (lane < a0 + k1))


def _pieces(x):
    hi, mid, lo = _split3(x)
    return hi.astype(F32), mid.astype(F32), lo.astype(F32)


def _inproj_fwd(x, ada, w_p, bf, ts):
    s = x.shape[0]
    ns = s // ts

    def body(x_ref, ada_ref, w_ref, bf_ref, q_ref, k_ref, v_ref, a_ref, g_ref, lf_ref, carry):
        i = pl.program_id(0)

        @pl.when(i == 0)
        def _():
            carry[...] = jnp.zeros_like(carry)

        u = (x_ref[...] * (1.0 + ada_ref[1:2, :]) + ada_ref[0:1, :]).astype(BF16)
        proj = _dot(u, w_ref[...])
        a_ref[...] = proj[:, 1536:2048]
        g_ref[...] = proj[:, 2048:2560]
        z = proj[:, 2560:2688] + bf_ref[...]
        lane = lax.broadcasted_iota(jnp.int32, (ts, 128), 1)
        logf = jnp.minimum(z, 0.0) - jnp.log(1.0 + jnp.exp(-jnp.abs(z)))
        logf = jnp.where(lane < N_HEADS, logf, 0.0)
        lf_ref[...] = logf
        r = lax.broadcasted_iota(jnp.int32, (ts, ts), 0)
        cc = lax.broadcasted_iota(jnp.int32, (ts, ts), 1)
        tri = (cc <= r).astype(BF16)
        fc = _tri_dot(tri, logf) + carry[...]
        carry[...] = fc[ts - 1:ts, :]
        for h in range(N_HEADS):
            pc = slice(128 * (h // 2), 128 * (h // 2) + 128)
            hc = slice(128 * h, 128 * h + 128)
            own, at, span = _aug_masks(lane, h)
            hi, mid, lo = _pieces(fc[:, h:h + 1])
            qp = proj[:, pc] * (HEAD_DIM ** -0.5)
            kp = proj[:, 512:1024][:, pc]
            vp = proj[:, 1024:1536][:, pc]
            q_aug = jnp.where(own, qp, jnp.where(at(0), hi, jnp.where(at(1), mid, jnp.where(at(2), lo,
                              jnp.where(span(3, 6), 1.0, 0.0)))))
            k_aug = jnp.where(own, kp, jnp.where(span(0, 3), 1.0, jnp.where(at(3), -hi, jnp.where(at(4), -mid,
                              jnp.where(at(5), -lo, jnp.where(span(6, 9), -1.0, 0.0))))))
            v_aug = jnp.where(own, vp, jnp.where(span(0, 3), 1.0, 0.0))
            q_ref[:, hc] = q_aug.astype(BF16)
            k_ref[:, hc] = k_aug.astype(BF16)
            v_ref[:, hc] = v_aug.astype(BF16)

    row = lambda i: (i, 0)
    full = lambda i: (0, 0)
    return pl.pallas_call(
        body, name="inproj_fwd", grid=(ns,),
        out_shape=(jax.ShapeDtypeStruct((s, N_HEADS * 128), BF16),) * 3 + (jax.ShapeDtypeStruct((s, 512), F32),) * 2
        + (jax.ShapeDtypeStruct((s, 128), F32),),
        in_specs=[pl.BlockSpec((ts, D_MODEL), row), pl.BlockSpec((6, D_MODEL), full),
                  pl.BlockSpec((D_MODEL, N_IN_PAD), full), pl.BlockSpec((1, 128), full)],
        out_specs=(pl.BlockSpec((ts, N_HEADS * 128), row),) * 3 + (pl.BlockSpec((ts, 512), row),) * 2
        + (pl.BlockSpec((ts, 128), row),),
        scratch_shapes=[pltpu.VMEM((1, 128), F32)],
        compiler_params=_cparams(1),
    )(x, ada, w_p, bf)


def _attn_fwd(qa, ka, va, shards, tq):
    s = qa.shape[0]
    nq = s // tq
    tk = tq
    n = len(shards)

    def body(q_ref, k_ref, v_ref, *rest):
        sh_in, (o_ref, qb_ref), sh_out = rest[:n], rest[n:n + 2], rest[n + 2:2 * n + 2]
        lsem, ssem, rsem = rest[2 * n + 2:]
        pair = pl.program_id(0)
        i = pl.program_id(1)

        @pl.when((pair == 0) & (i == 0))
        def _():
            _gather_copies(sh_in, sh_out, lsem, ssem, rsem).start()

        lane = lax.broadcasted_iota(jnp.int32, (tq, 128), 1)
        t_off = lax.broadcasted_iota(jnp.int32, (tq, tk), 0)
        s_off = lax.broadcasted_iota(jnp.int32, (tq, tk), 1)

        def block(j, carry, diagonal):
            start = pl.multiple_of(j * tk, tk)
            out = []
            for hh in range(2):
                cols = slice(128 * hh, 128 * hh + 128)
                m, acc = carry[2 * hh:2 * hh + 2]
                sc = _dot_nt(q_ref[:, cols], k_ref[pl.ds(start, tk), cols])
                if diagonal:
                    sc = jnp.where(s_off <= t_off, sc, -jnp.inf)
                m_new = jnp.maximum(m, jnp.max(sc, axis=-1, keepdims=True))
                pr = jnp.exp(sc - m_new).astype(BF16)
                acc = acc * jnp.exp(m - m_new) + _dot(pr, v_ref[pl.ds(start, tk), cols])
                out += [m_new, acc]
            return tuple(out)

        init = (jnp.full((tq, 1), -jnp.inf, F32), jnp.zeros((tq, 128), F32)) * 2
        res = lax.fori_loop(0, i, functools.partial(block, diagonal=False), init)
        res = block(i, res, diagonal=True)
        outs = []
        for hh in range(2):
            cols = slice(128 * hh, 128 * hh + 128)
            _, at, _ = _aug_masks(lane, hh)
            a0 = HEAD_DIM if hh == 0 else 0
            m, acc = res[2 * hh:2 * hh + 2]
            denom = acc[:, a0:a0 + 1]
            outs.append(acc / denom)
            hi, mid, lo = _split3(m + jnp.log(denom))
            qb_ref[:, cols] = jnp.where(at(6), hi, jnp.where(at(7), mid, jnp.where(at(8), lo, q_ref[:, cols])))
        o_ref[...] = jnp.where(lane < HEAD_DIM, outs[0], outs[1])

        @pl.when((pair == N_PAIRS - 1) & (i == nq - 1))
        def _():
            _gather_copies(sh_in, sh_out, lsem, ssem, rsem).wait()

    res = pl.pallas_call(
        body, name="attn_fwd", grid=(N_PAIRS, nq),
        out_shape=(jax.ShapeDtypeStruct((s, 512), F32), jax.ShapeDtypeStruct((s, N_HEADS * 128), BF16))
        + tuple(jax.ShapeDtypeStruct((N_SHARD,) + w.shape, w.dtype) for w in shards),
        in_specs=[pl.BlockSpec((tq, 256), lambda p, i: (i, p)), pl.BlockSpec((s, 256), lambda p, i: (0, p)),
                  pl.BlockSpec((s, 256), lambda p, i: (0, p))] + [ANY] * n,
        out_specs=(pl.BlockSpec((tq, 128), lambda p, i: (i, p)), pl.BlockSpec((tq, 256), lambda p, i: (i, p)))
        + (ANY,) * n,
        scratch_shapes=_copy_sems(n),
        compiler_params=_cparams(2),
    )(qa, ka, va, *shards)
    return res[0], res[1], res[2:]


def _fill_shifts(buf, shifted, ts):
    for s in range(1, 8):
        shifted[s - 1] = buf[s:s + ts + HALO - 8, :]


def _rows_at(buf, shifted, off, ts):
    s = off % 8
    if s == 0:
        return buf[off:off + ts, :]
    return shifted[s - 1, off - s:off - s + ts, :]


def _shift_scratch(ts):
    return pltpu.VMEM((7, ts + HALO - 8, 512), F32)


def _conv_branch(a, g, ah, gh, first, ugx, ush, wdw_ref, prm_ref, gm, ts):
    sg_g = jax.nn.sigmoid(g)
    ug = (a * sg_g).astype(BF16).astype(F32)
    ugh = jnp.where(first, 0.0, (ah * jax.nn.sigmoid(gh)).astype(BF16).astype(F32))
    ugx[0:HALO, :] = ugh
    ugx[HALO:HALO + ts, :] = ug
    _fill_shifts(ugx, ush, ts)
    y = jnp.zeros((ts, CONV_W), F32) + prm_ref[0:1, :]
    for kk in range(CONV_K):
        off = HALO - (CONV_K - 1) + kk
        y = y + wdw_ref[kk:kk + 1, :] * _rows_at(ugx, ush, off, ts)
    mu = _dot_f32(y, gm)
    d = y - mu
    var = _dot_f32(d * d, gm)
    rs = lax.rsqrt(var + LN_EPS)
    yhat = d * rs
    yn = yhat * prm_ref[1:2, :] + prm_ref[2:3, :]
    sg = jax.nn.sigmoid(yn)
    co = yn * sg
    return sg_g, rs, yhat, yn, sg, co


def _mix_inputs(o, co, prm_ref):
    ra = lax.rsqrt(_mean_last(o * o) + LN_EPS)
    oh = o * ra
    rc = lax.rsqrt(_mean_last(co * co) + LN_EPS)
    ch = co * rc
    mi = jnp.concatenate([oh * prm_ref[3:4, :], ch * prm_ref[4:5, :]], axis=-1).astype(BF16)
    return ra, oh, rc, ch, mi


def _layernorm_stats(r):
    mu = _mean_last(r)
    d = r - mu
    rstd = lax.rsqrt(_mean_last(d * d) + LN_EPS)
    return d * rstd, rstd


def _layernorm_bwd(dout, xh, rstd, gain):
    dxh = dout * gain
    return rstd * (dxh - _mean_last(dxh) - xh * _mean_last(dxh * xh))


def _halo_index(tile, ts):
    return jnp.maximum(tile * (ts // HALO) - 1, 0)


def _mid_fwd(a, g, o, x, ada, wdw, prm, ln1, w_out, gm, ts):
    s = x.shape[0]
    ns = s // ts

    def body(a_ref, g_ref, ah_ref, gh_ref, o_ref, x_ref, ada_ref, wdw_ref, prm_ref, ln_ref, wo_ref, gm_ref,
             x1_ref, ugx, ush):
        i = pl.program_id(0)
        co = _conv_branch(a_ref[...], g_ref[...], ah_ref[...], gh_ref[...], i == 0, ugx, ush, wdw_ref, prm_ref,
                          gm_ref[...], ts)[-1]
        mi = _mix_inputs(o_ref[...], co, prm_ref)[-1]
        mixed = _dot(mi, wo_ref[...])
        r1 = ALPHA * x_ref[...] + (1.0 + ada_ref[2:3, :]) * mixed
        xh, _ = _layernorm_stats(r1)
        x1_ref[...] = xh * ln_ref[0:1, :] + ln_ref[1:2, :]

    row = lambda i: (i, 0)
    full = lambda i: (0, 0)
    halo = lambda i: (_halo_index(i, ts), 0)
    return pl.pallas_call(
        body, name="mid_fwd", grid=(ns,),
        out_shape=jax.ShapeDtypeStruct((s, D_MODEL), F32),
        in_specs=[pl.BlockSpec((ts, 512), row), pl.BlockSpec((ts, 512), row),
                  pl.BlockSpec((HALO, 512), halo), pl.BlockSpec((HALO, 512), halo),
                  pl.BlockSpec((ts, 512), row), pl.BlockSpec((ts, D_MODEL), row),
                  pl.BlockSpec((6, D_MODEL), full), pl.BlockSpec((32, 512), full), pl.BlockSpec((8, 512), full),
                  pl.BlockSpec((2, D_MODEL), full), pl.BlockSpec((D_MODEL, D_MODEL), full),
                  pl.BlockSpec((512, 512), full)],
        out_specs=pl.BlockSpec((ts, D_MODEL), row),
        scratch_shapes=[pltpu.VMEM((ts + HALO, 512), F32), _shift_scratch(ts)],
        compiler_params=_cparams(1),
    )(a, g, a, g, o, x, ada, wdw, prm, ln1, w_out, gm)


def _ffn_fwd(x1, ada, w1, w2, ln2, tgt, ts):
    s = x1.shape[0]
    ns = s // ts
    nf = N_SHARD

    def body(x1_ref, ada_ref, w1_ref, w2_ref, ln_ref, t_ref, dff_ref, dx1_ref, pg_ref, ffacc, u2):
        i = pl.program_id(0)
        f = pl.program_id(1)

        @pl.when((i == 0) & (f == 0))
        def _():
            pg_ref[...] = jnp.zeros_like(pg_ref)

        @pl.when(f == 0)
        def _():
            u2[...] = (x1_ref[...] * (1.0 + ada_ref[4:5, :]) + ada_ref[3:4, :]).astype(BF16)
            ffacc[...] = jnp.zeros_like(ffacc)

        h = _dot(u2[...], w1_ref[0])
        r = jnp.maximum(h, 0.0)
        ffacc[...] += _dot((r * r).astype(BF16), w2_ref[0])

        @pl.when(f == nf - 1)
        def _():
            ff = ffacc[...]
            r2 = ALPHA * x1_ref[...] + (1.0 + ada_ref[5:6, :]) * ff
            xh, rstd = _layernorm_stats(r2)
            yv = xh * ln_ref[0:1, :] + ln_ref[1:2, :]
            err = yv - t_ref[...]
            dy = err * (1.0 / D_MODEL)
            dr2 = _layernorm_bwd(dy, xh, rstd, ln_ref[0:1, :])
            pg_ref[0:1, :] += _rowsum(dy * xh)
            pg_ref[1:2, :] += _rowsum(dy)
            pg_ref[2:3, :] += _rowsum(dr2 * ff)
            pg_ref[3:4, :] += _rowsum(err * err) * (0.5 / D_MODEL)
            dff_ref[...] = ((1.0 + ada_ref[5:6, :]) * dr2).astype(BF16)
            dx1_ref[...] = ALPHA * dr2

    row = lambda i, f: (i, 0)
    full = lambda i, f: (0, 0)
    chunk = lambda i, f: (f, 0, 0)
    return pl.pallas_call(
        body, name="ffn_fwd", grid=(ns, nf),
        out_shape=(jax.ShapeDtypeStruct((s, D_MODEL), BF16), jax.ShapeDtypeStruct((s, D_MODEL), F32),
                   jax.ShapeDtypeStruct((8, D_MODEL), F32)),
        in_specs=[pl.BlockSpec((ts, D_MODEL), row), pl.BlockSpec((6, D_MODEL), full),
                  pl.BlockSpec((1, D_MODEL, FF_CHUNK), chunk), pl.BlockSpec((1, FF_CHUNK, D_MODEL), chunk),
                  pl.BlockSpec((2, D_MODEL), full), pl.BlockSpec((ts, D_MODEL), row)],
        out_specs=(pl.BlockSpec((ts, D_MODEL), row), pl.BlockSpec((ts, D_MODEL), row),
                   pl.BlockSpec((8, D_MODEL), full)),
        scratch_shapes=[pltpu.VMEM((ts, D_MODEL), F32), pltpu.VMEM((ts, D_MODEL), BF16)],
        compiler_params=_cparams(2),
    )(x1, ada, w1, w2, ln2, tgt)


def _ffn_bwd_chunk(f, x1, ada, w1, w2, dff, dx1, dw1_all, dw2_all, ts):
    s = x1.shape[0]
    ns = s // ts

    def body(x1_ref, ada_ref, w1_ref, w2_ref, dff_ref, dx1_in, dw1_any, dw2_any, dx1_out, dw1_ref, dw2_ref, pg_ref):
        i = pl.program_id(0)

        @pl.when(i == 0)
        def _():
            pg_ref[...] = jnp.zeros_like(pg_ref)
            dw1_ref[...] = jnp.zeros_like(dw1_ref)
            dw2_ref[...] = jnp.zeros_like(dw2_ref)

        x1v = x1_ref[...]
        u2 = (x1v * (1.0 + ada_ref[4:5, :]) + ada_ref[3:4, :]).astype(BF16)
        h = _dot(u2, w1_ref[0])
        r = jnp.maximum(h, 0.0)
        hid = (r * r).astype(BF16)
        dffv = dff_ref[...]
        dh = (_dot_nt(dffv, w2_ref[0]) * (2.0 * r)).astype(BF16)
        dw2_ref[0] += _dot_tn(hid, dffv)
        dw1_ref[0] += _dot_tn(u2, dh)
        du2 = _dot_nt(dh, w1_ref[0])
        dx1_out[...] = dx1_in[...] + du2 * (1.0 + ada_ref[4:5, :])
        pg_ref[0:1, :] += _rowsum(du2 * x1v)
        pg_ref[1:2, :] += _rowsum(du2)

    row = lambda i: (i, 0)
    full = lambda i: (0, 0)
    full3 = lambda i: (0, 0, 0)
    chunk = lambda i: (f, 0, 0)
    return pl.pallas_call(
        body, name=f"ffn_bwd_{f}", grid=(ns,),
        out_shape=(jax.ShapeDtypeStruct((s, D_MODEL), F32), jax.ShapeDtypeStruct((N_SHARD, D_MODEL, FF_CHUNK), F32),
                   jax.ShapeDtypeStruct((N_SHARD, FF_CHUNK, D_MODEL), F32), jax.ShapeDtypeStruct((8, D_MODEL), F32)),
        in_specs=[pl.BlockSpec((ts, D_MODEL), row), pl.BlockSpec((6, D_MODEL), full),
                  pl.BlockSpec((1, D_MODEL, FF_CHUNK), chunk), pl.BlockSpec((1, FF_CHUNK, D_MODEL), chunk),
                  pl.BlockSpec((ts, D_MODEL), row), pl.BlockSpec((ts, D_MODEL), row), ANY, ANY],
        out_specs=(pl.BlockSpec((ts, D_MODEL), row), pl.BlockSpec((1, D_MODEL, FF_CHUNK), chunk),
                   pl.BlockSpec((1, FF_CHUNK, D_MODEL), chunk), pl.BlockSpec((8, D_MODEL), full)),
        input_output_aliases={6: 1, 7: 2},
        compiler_params=_cparams(1),
    )(x1, ada, w1, w2, dff, dx1, dw1_all, dw2_all)


def _mid_bwd(a, g, o, x, dx1, ada, wdw, prm, ln1, w_out, gm, sel, ts):
    s = x.shape[0]
    ns = s // ts

    def body(a_ref, g_ref, ah_ref, gh_ref, o_ref, x_ref, dx1_ref, ada_ref, wdw_ref, prm_ref, ln_ref, wo_ref,
             gm_ref, sel_ref,
             do_ref, da_ref, dg_ref, dxa_ref, dwo_ref, pgm_ref, pgc_ref, dwdw_ref,
             ugx, dyx, ush, dsh):
        i = pl.program_id(0)
        tile = ns - 1 - i

        @pl.when(i == 0)
        def _():
            dwo_ref[...] = jnp.zeros_like(dwo_ref)
            pgm_ref[...] = jnp.zeros_like(pgm_ref)
            pgc_ref[...] = jnp.zeros_like(pgc_ref)
            dwdw_ref[...] = jnp.zeros_like(dwdw_ref)
            dyx[ts:ts + HALO, :] = jnp.zeros((HALO, 512), F32)

        gmv = gm_ref[...]
        av = a_ref[...]
        ov = o_ref[...]
        sg_g, rs, yhat, yn, sg, co = _conv_branch(av, g_ref[...], ah_ref[...], gh_ref[...], tile == 0, ugx, ush,
                                                  wdw_ref, prm_ref, gmv, ts)
        ra, oh, rc, ch, mi = _mix_inputs(ov, co, prm_ref)
        mixed = _dot(mi, wo_ref[...])
        gt1 = 1.0 + ada_ref[2:3, :]
        r1 = ALPHA * x_ref[...] + gt1 * mixed
        xh, rstd = _layernorm_stats(r1)
        dx1 = dx1_ref[...]
        pgm_ref[0:1, :] += _rowsum(dx1 * xh)
        pgm_ref[1:2, :] += _rowsum(dx1)
        dr1 = _layernorm_bwd(dx1, xh, rstd, ln_ref[0:1, :])
        dxa_ref[...] = ALPHA * dr1
        pgm_ref[2:3, :] += _rowsum(dr1 * mixed)
        dmixed = (gt1 * dr1).astype(BF16)
        dmi = _dot_nt(dmixed, wo_ref[...])
        dwo_ref[...] += _dot_tn(mi, dmixed)
        dna = dmi[:, 0:512]
        dnc = dmi[:, 512:1024]
        pgc_ref[3:4, :] += _rowsum(dna * oh)
        doh = dna * prm_ref[3:4, :]
        do = ra * (doh - oh * _mean_last(doh * oh))
        lane = lax.broadcasted_iota(jnp.int32, (ts, 128), 1)
        delta = _dot_f32(do * ov, sel_ref[...])
        for h in range(N_HEADS):
            own, at, _ = _aug_masks(lane, h)
            hi, mid, lo = _pieces(-delta[:, h:h + 1])
            dop = do[:, 128 * (h // 2):128 * (h // 2) + 128]
            do_ref[:, 128 * h:128 * h + 128] = jnp.where(
                own, dop, jnp.where(at(0), hi, jnp.where(at(1), mid, jnp.where(at(2), lo, 0.0)))).astype(BF16)
        pgc_ref[4:5, :] += _rowsum(dnc * ch)
        dch = dnc * prm_ref[4:5, :]
        dco = rc * (dch - ch * _mean_last(dch * ch))
        dyn = dco * (sg * (1.0 + yn * (1.0 - sg)))
        pgc_ref[1:2, :] += _rowsum(dyn * yhat)
        pgc_ref[2:3, :] += _rowsum(dyn)
        dyh = dyn * prm_ref[1:2, :]
        dy = rs * (dyh - _dot_f32(dyh, gmv) - yhat * _dot_f32(dyh * yhat, gmv))
        pgc_ref[0:1, :] += _rowsum(dy)
        dyr = dy.astype(BF16).astype(F32)
        dyx[0:ts, :] = dyr
        _fill_shifts(dyx, dsh, ts)
        dug = jnp.zeros((ts, CONV_W), F32)
        for kk in range(CONV_K):
            off = HALO - (CONV_K - 1) + kk
            dwdw_ref[kk:kk + 1, :] += _rowsum(dyr * _rows_at(ugx, ush, off, ts))
            back = CONV_K - 1 - kk
            dug = dug + wdw_ref[kk:kk + 1, :] * _rows_at(dyx, dsh, back, ts)
        dyx[ts:ts + HALO, :] = dyr[0:HALO, :]
        da_ref[...] = (dug * sg_g).astype(BF16)
        dg_ref[...] = (dug * av * sg_g * (1.0 - sg_g)).astype(BF16)

    row = lambda i: (ns - 1 - i, 0)
    full = lambda i: (0, 0)
    halo = lambda i: (_halo_index(ns - 1 - i, ts), 0)
    return pl.pallas_call(
        body, name="mid_bwd", grid=(ns,),
        out_shape=(jax.ShapeDtypeStruct((s, N_HEADS * 128), BF16), jax.ShapeDtypeStruct((s, 512), BF16),
                   jax.ShapeDtypeStruct((s, 512), BF16), jax.ShapeDtypeStruct((s, D_MODEL), F32),
                   jax.ShapeDtypeStruct((D_MODEL, D_MODEL), F32),
                   jax.ShapeDtypeStruct((8, D_MODEL), F32), jax.ShapeDtypeStruct((8, 512), F32),
                   jax.ShapeDtypeStruct((32, 512), F32)),
        in_specs=[pl.BlockSpec((ts, 512), row), pl.BlockSpec((ts, 512), row),
                  pl.BlockSpec((HALO, 512), halo), pl.BlockSpec((HALO, 512), halo),
                  pl.BlockSpec((ts, 512), row), pl.BlockSpec((ts, D_MODEL), row), pl.BlockSpec((ts, D_MODEL), row),
                  pl.BlockSpec((6, D_MODEL), full), pl.BlockSpec((32, 512), full), pl.BlockSpec((8, 512), full),
                  pl.BlockSpec((2, D_MODEL), full), pl.BlockSpec((D_MODEL, D_MODEL), full),
                  pl.BlockSpec((512, 512), full), pl.BlockSpec((512, 128), full)],
        out_specs=(pl.BlockSpec((ts, N_HEADS * 128), row), pl.BlockSpec((ts, 512), row), pl.BlockSpec((ts, 512), row),
                   pl.BlockSpec((ts, D_MODEL), row),
                   pl.BlockSpec((D_MODEL, D_MODEL), full), pl.BlockSpec((8, D_MODEL), full),
                   pl.BlockSpec((8, 512), full), pl.BlockSpec((32, 512), full)),
        scratch_shapes=[pltpu.VMEM((ts + HALO, 512), F32), pltpu.VMEM((ts + HALO, 512), F32), _shift_scratch(ts),
                        _shift_scratch(ts)],
        compiler_params=_cparams(1),
    )(a, g, a, g, o, x, dx1, ada, wdw, prm, ln1, w_out, gm, sel)


def _attn_bwd(qb, ka, va, doa, dwo, dw1, dw2, tk):
    s = qb.shape[0]
    nk = s // tk
    tq = tk

    def body(q_ref, do_ref, k_ref, v_ref, dwo_hbm, dw1_hbm, dw2_hbm,
             dq_ref, rs_ref, dk_ref, dv_ref, cs_ref, land_out, land_ff, *sems):
        pair = pl.program_id(0)
        j = pl.program_id(1)

        def exchanges():
            return (_scatter_copies((dwo_hbm,), land_out, _STACK_OUT, *sems[:3]),
                    _scatter_copies((dw1_hbm, dw2_hbm), land_ff, _STACK_FF, *sems[3:]))

        @pl.when((pair == 0) & (j == 0))
        def _():
            for ex in exchanges():
                ex.start()

        @pl.when(j == 0)
        def _():
            dq_ref[...] = jnp.zeros_like(dq_ref)

        @pl.when((pair == 0) & (j == 0))
        def _():
            rs_ref[...] = jnp.zeros_like(rs_ref)
            cs_ref[...] = jnp.zeros_like(cs_ref)

        lane = lax.broadcasted_iota(jnp.int32, (tk, 128), 1)
        low = lane < HEAD_DIM
        t_off = lax.broadcasted_iota(jnp.int32, (tq, tk), 0)
        s_off = lax.broadcasted_iota(jnp.int32, (tq, tk), 1)

        def block(i, carry, diagonal):
            rows_q = pl.ds(pl.multiple_of(i * tq, tq), tq)
            dq_h, out = [], []
            for hh in range(2):
                cols = slice(128 * hh, 128 * hh + 128)
                dk_acc, dv_acc = carry[2 * hh:2 * hh + 2]
                qh = q_ref[rows_q, cols]
                dh = do_ref[rows_q, cols]
                pr = jnp.exp(_dot_nt(qh, k_ref[:, cols]))
                if diagonal:
                    pr = jnp.where(s_off <= t_off, pr, 0.0)
                ds = (pr * _dot_nt(dh, v_ref[:, cols])).astype(BF16)
                dq_h.append(_dot(ds, k_ref[:, cols]))
                out += [dk_acc + _dot_tn(ds, qh), dv_acc + _dot_tn(pr.astype(BF16), dh)]
            dq_ref[rows_q, :] += jnp.where(low, dq_h[0], dq_h[1])
            rs_ref[rows_q, :] += (jnp.where(lane == 2 * pair, dq_h[0][:, HEAD_DIM:HEAD_DIM + 1], 0.0)
                                  + jnp.where(lane == 2 * pair + 1, dq_h[1][:, 0:1], 0.0))
            return tuple(out)

        first = block(j, (jnp.zeros((tk, 128), F32),) * 4, diagonal=True)
        dk0, dv0, dk1, dv1 = lax.fori_loop(j + 1, nk, functools.partial(block, diagonal=False), first)
        dk_ref[...] = jnp.where(low, dk0, dk1).astype(BF16)
        dv_ref[...] = jnp.where(low, dv0, dv1).astype(BF16)
        rows_k = pl.ds(pl.multiple_of(j * tk, tk), tk)
        cs_ref[rows_k, :] += (jnp.where(lane == 2 * pair, dk0[:, HEAD_DIM + 3:HEAD_DIM + 4], 0.0)
                              + jnp.where(lane == 2 * pair + 1, dk1[:, 3:4], 0.0))

        @pl.when((pair == N_PAIRS - 1) & (j == nk - 1))
        def _():
            for ex in exchanges():
                ex.wait()

    whole = lambda p, j: (0, 0)
    return pl.pallas_call(
        body, name="attn_bwd", grid=(N_PAIRS, nk),
        out_shape=(jax.ShapeDtypeStruct((s, 512), F32), jax.ShapeDtypeStruct((s, 128), F32),
                   jax.ShapeDtypeStruct((s, 512), BF16), jax.ShapeDtypeStruct((s, 512), BF16),
                   jax.ShapeDtypeStruct((s, 128), F32), jax.ShapeDtypeStruct(dwo.shape, F32),
                   jax.ShapeDtypeStruct((N_SHARD, 2 * FF_CHUNK, D_MODEL), F32)),
        in_specs=[pl.BlockSpec((s, 256), lambda p, j: (0, p)), pl.BlockSpec((s, 256), lambda p, j: (0, p)),
                  pl.BlockSpec((tk, 256), lambda p, j: (j, p)), pl.BlockSpec((tk, 256), lambda p, j: (j, p)),
                  ANY, ANY, ANY],
        out_specs=(pl.BlockSpec((s, 128), lambda p, j: (0, p)), pl.BlockSpec((s, 128), whole),
                   pl.BlockSpec((tk, 128), lambda p, j: (j, p)), pl.BlockSpec((tk, 128), lambda p, j: (j, p)),
                   pl.BlockSpec((s, 128), whole), ANY, ANY),
        scratch_shapes=_copy_sems(1) + _copy_sems(2),
        compiler_params=_cparams(2),
    )(qb, doa, ka, va, dwo, dw1, dw2)


def _inproj_bwd(x, ada, w_p, dq, dk, dv, da, dg, dfc, drs, logf, dxa, ts):
    s = x.shape[0]
    ns = s // ts

    def body(x_ref, ada_ref, w_ref, dq_ref, dk_ref, dv_ref, da_ref, dg_ref, dfc_ref, drs_ref, lf_ref, dxa_ref,
             gx_ref, dw_ref, pgi_ref, dbf_ref, carry, dw_vm):
        i = pl.program_id(0)

        @pl.when(i == 0)
        def _():
            carry[...] = jnp.zeros_like(carry)
            dw_vm[...] = jnp.zeros_like(dw_vm)
            pgi_ref[...] = jnp.zeros_like(pgi_ref)
            dbf_ref[...] = jnp.zeros_like(dbf_ref)

        r = lax.broadcasted_iota(jnp.int32, (ts, ts), 0)
        cc = lax.broadcasted_iota(jnp.int32, (ts, ts), 1)
        tri = (cc >= r).astype(BF16)
        dlogf = carry[...] + _tri_dot(tri, drs_ref[...] - dfc_ref[...])
        carry[...] = dlogf[0:1, :]
        lane = lax.broadcasted_iota(jnp.int32, (ts, 128), 1)
        dz = jnp.where(lane < N_HEADS, dlogf * (1.0 - jnp.exp(lf_ref[...])), 0.0)
        dbf_ref[0:1, :] += _rowsum(dz)
        dproj = jnp.concatenate(
            [(dq_ref[...] * (HEAD_DIM ** -0.5)).astype(BF16), dk_ref[...], dv_ref[...], da_ref[...], dg_ref[...],
             dz.astype(BF16)], axis=-1)
        xv = x_ref[...]
        sc1 = 1.0 + ada_ref[1:2, :]
        u = (xv * sc1 + ada_ref[0:1, :]).astype(BF16)
        du = _dot_nt(dproj, w_ref[...])
        dw_vm[...] += _dot_tn(u, dproj)
        gx_ref[...] = dxa_ref[...] + du * sc1
        pgi_ref[0:1, :] += _rowsum(du * xv)
        pgi_ref[1:2, :] += _rowsum(du)

        @pl.when(i == ns - 1)
        def _():
            dw_ref[...] = dw_vm[...].astype(BF16)

    row = lambda i: (ns - 1 - i, 0)
    full = lambda i: (0, 0)
    return pl.pallas_call(
        body, name="inproj_bwd", grid=(ns,),
        out_shape=(jax.ShapeDtypeStruct((s, D_MODEL), F32), jax.ShapeDtypeStruct((D_MODEL, N_IN_PAD), BF16),
                   jax.ShapeDtypeStruct((8, D_MODEL), F32), jax.ShapeDtypeStruct((8, 128), F32)),
        in_specs=[pl.BlockSpec((ts, D_MODEL), row), pl.BlockSpec((6, D_MODEL), full),
                  pl.BlockSpec((D_MODEL, N_IN_PAD), full)]
        + [pl.BlockSpec((ts, 512), row)] * 5 + [pl.BlockSpec((ts, 128), row)] * 3
        + [pl.BlockSpec((ts, D_MODEL), row)],
        out_specs=(pl.BlockSpec((ts, D_MODEL), row), pl.BlockSpec((D_MODEL, N_IN_PAD), full),
                   pl.BlockSpec((8, D_MODEL), full), pl.BlockSpec((8, 128), full)),
        scratch_shapes=[pltpu.VMEM((1, 128), F32), pltpu.VMEM((D_MODEL, N_IN_PAD), F32)],
        compiler_params=_cparams(1),
    )(x, ada, w_p, dq, dk, dv, da, dg, dfc, drs, logf, dxa)


def _small_reduce(packed, part):
    rows = part.shape[1]

    def body(p_ref, part_hbm, sum_ref, all_ref, land_hbm, ssem, rsem, lsem2, ssem2, rsem2):
        x, y, c = _position()
        me = 4 * x + 2 * y + c
        exchange = _scatter_copies((part_hbm,), land_hbm, ((0, rows),), lsem2, ssem2, rsem2)
        exchange.start()
        all_ref[me] = p_ref[...]
        sends = []
        for k in range(1, 8):
            peer = (x ^ ((k >> 2) & 1), y ^ ((k >> 1) & 1), c ^ (k & 1))
            cp = pltpu.make_async_remote_copy(
                src_ref=p_ref, dst_ref=all_ref.at[me], send_sem=ssem.at[k], recv_sem=rsem.at[k],
                device_id=peer, device_id_type=MESH)
            cp.start()
            sends.append(cp)
        for k in range(1, 8):
            pltpu.make_async_remote_copy(
                src_ref=p_ref, dst_ref=all_ref.at[me ^ k], send_sem=ssem.at[k], recv_sem=rsem.at[k],
                device_id=(x, y, c), device_id_type=MESH).wait_recv()
        for cp in sends:
            cp.wait_send()
        total = all_ref[0]
        for dev in range(1, 8):
            total = total + all_ref[dev]
        sum_ref[...] = total
        loss = jnp.sum(total[SMALL_ROWS - 1:SMALL_ROWS, :], axis=-1, keepdims=True)
        sum_ref[SMALL_ROWS - 1:SMALL_ROWS, :] = jnp.broadcast_to(loss, (1, D_MODEL))
        exchange.wait()

    vm = pl.BlockSpec(memory_space=pltpu.VMEM)
    return pl.pallas_call(
        body, name="small_reduce",
        out_shape=(jax.ShapeDtypeStruct((SMALL_ROWS, D_MODEL), F32), jax.ShapeDtypeStruct((8, SMALL_ROWS, D_MODEL), F32),
                   jax.ShapeDtypeStruct(part.shape, part.dtype)),
        in_specs=[vm, ANY], out_specs=(vm, vm, ANY),
        scratch_shapes=[pltpu.SemaphoreType.DMA((8,)), pltpu.SemaphoreType.DMA((8,))] + _copy_sems(1),
        compiler_params=pltpu.CompilerParams(vmem_limit_bytes=VMEM_LIMIT),
    )(packed, part)


def _adam_math(gv, wv, mv, vv):
    m_new = B1 * mv + (1.0 - B1) * gv
    v_new = B2 * vv + (1.0 - B2) * (gv * gv)
    m_hat = m_new / (1.0 - B1 ** STEP)
    v_hat = v_new / (1.0 - B2 ** STEP)
    delta = -LR * (m_hat / (jnp.sqrt(v_hat) + ADAM_EPS) + WD * wv)
    return delta, m_new, v_new


def _adamw(gv, wv, mv, vv, name):
    rows, cols = gv.shape
    tr = rows
    for cand in (256, 128, 64, 32, 16, 8):
        if rows % cand == 0 and rows > cand:
            tr = cand
            break

    def body(g_ref, w_ref, m_ref, v_ref, d_ref, mo_ref, vo_ref):
        d_ref[...], mo_ref[...], vo_ref[...] = _adam_math(g_ref[...], w_ref[...], m_ref[...], v_ref[...])

    spec = pl.BlockSpec((tr, cols), lambda i: (i, 0))
    return pl.pallas_call(
        body, name=name, grid=(rows // tr,),
        out_shape=(jax.ShapeDtypeStruct((rows, cols), F32),) * 3,
        in_specs=[spec] * 4, out_specs=(spec,) * 3,
        compiler_params=_cparams(1),
    )(gv, wv, mv, vv)


def _w_ada_update(sct, dd, wv, mv, vv):
    rows, cols = wv.shape
    tr = 128

    def body(s_ref, d_ref, w_ref, m_ref, v_ref, g_ref, dl_ref, mo_ref, vo_ref):
        sv = s_ref[...]
        dv = d_ref[...]
        gv = sv[:, 0:1] * dv[0:1, :]
        for b in range(1, 8):
            gv = gv + sv[:, b:b + 1] * dv[b:b + 1, :]
        g_ref[...] = gv
        dl_ref[...], mo_ref[...], vo_ref[...] = _adam_math(gv, w_ref[...], m_ref[...], v_ref[...])

    spec = pl.BlockSpec((tr, cols), lambda i: (i, 0))
    return pl.pallas_call(
        body, name="w_ada_update", grid=(rows // tr,),
        out_shape=(jax.ShapeDtypeStruct((rows, cols), F32),) * 4,
        in_specs=[pl.BlockSpec((tr, 8), lambda i: (i, 0)), pl.BlockSpec((8, cols), lambda i: (0, 0))] + [spec] * 3,
        out_specs=(spec,) * 4,
        compiler_params=_cparams(1),
    )(sct, dd, wv, mv, vv)


def _sum_chips(land_in, land_out, land_ff):
    tr = 256
    n_in, n_out = 768 // tr, 256 // tr

    def body(in_ref, out_ref, ff_ref, s_ref):
        i = pl.program_id(0)

        def total(ref):
            s_ref[...] = ((ref[0].astype(F32) + ref[1].astype(F32)) + ref[2].astype(F32)) + ref[3].astype(F32)

        pl.when(i < n_in)(lambda: total(in_ref))
        pl.when((i >= n_in) & (i < n_in + n_out))(lambda: total(out_ref))
        pl.when(i >= n_in + n_out)(lambda: total(ff_ref))

    return pl.pallas_call(
        body, name="sum_chips", grid=(STACK_ROWS // tr,),
        out_shape=jax.ShapeDtypeStruct((STACK_ROWS, D_MODEL), F32),
        in_specs=[pl.BlockSpec((N_SHARD, tr, D_MODEL), lambda i: (0, jnp.minimum(i, n_in - 1), 0)),
                  pl.BlockSpec((N_SHARD, tr, D_MODEL), lambda i: (0, jnp.clip(i - n_in, 0, n_out - 1), 0)),
                  pl.BlockSpec((N_SHARD, tr, D_MODEL), lambda i: (0, jnp.maximum(i - n_in - n_out, 0), 0))],
        out_specs=pl.BlockSpec((tr, D_MODEL), lambda i: (i, 0)),
        compiler_params=_cparams(1),
    )(land_in, land_out, land_ff)


def _core_swap(part):
    def body(p_ref, o_ref, ssem, rsem):
        x, y, c = _position()
        cp = pltpu.make_async_remote_copy(src_ref=p_ref, dst_ref=o_ref, send_sem=ssem, recv_sem=rsem,
                                          device_id=(x, y, 1 - c), device_id_type=MESH)
        cp.start()
        cp.wait()

    return pl.pallas_call(
        body, name="core_swap",
        out_shape=jax.ShapeDtypeStruct(part.shape, part.dtype),
        in_specs=[ANY], out_specs=ANY,
        scratch_shapes=[pltpu.SemaphoreType.DMA, pltpu.SemaphoreType.DMA],
    )(part)


def _add_pair(mine, other):
    tr = 256

    def body(a_ref, b_ref, o_ref):
        o_ref[...] = a_ref[...] + b_ref[...]

    spec = pl.BlockSpec((tr, D_MODEL), lambda i: (i, 0))
    return pl.pallas_call(
        body, name="add_pair", grid=(STACK_ROWS // tr,),
        out_shape=jax.ShapeDtypeStruct((STACK_ROWS, D_MODEL), F32),
        in_specs=[spec, spec], out_specs=spec,
        compiler_params=_cparams(1),
    )(mine, other)


def _pad_lanes(v, width=D_MODEL):
    v = v.reshape(1, -1)
    return jnp.pad(v, ((0, 0), (0, width - v.shape[1])))


def _pack_small(b_ada, ln1_g, ln1_b, ln2_g, ln2_b, b_dw, gn_g, gn_b, g_attn, g_conv, b_forget, w_dw_full, last):
    rows = [b_ada.reshape(6, D_MODEL)] + [_pad_lanes(v) for v in
                                          (ln1_g, ln1_b, ln2_g, ln2_b, b_dw, gn_g, gn_b, g_attn, g_conv, b_forget)]
    rows.append(jnp.pad(w_dw_full.reshape(CONV_K, -1), ((0, 0), (0, D_MODEL - w_dw_full.reshape(CONV_K, -1).shape[1]))))
    rows.append(_pad_lanes(last))
    return jnp.concatenate(rows, axis=0)


def _unpack_small(p):
    return dict(b_ada=p[0:6].reshape(1, 6 * D_MODEL), ln1_g=p[6:7], ln1_b=p[7:8], ln2_g=p[8:9], ln2_b=p[9:10],
                b_dw=p[10:11, :512], gn_g=p[11:12, :512], gn_b=p[12:13, :512], g_attn_out=p[13:14, :512],
                g_conv_out=p[14:15, :512], b_forget=p[15:16, :N_HEADS])


def kernel(x, c, w_ada, b_ada, w_in, b_forget, w_dw, b_dw, gn_g, gn_b, g_attn_out, g_conv_out, w_out, ln1_g, ln1_b, w_ff1, w_ff2, ln2_g, ln2_b, loss_target, m_w_ada, m_b_ada, m_w_in, m_b_forget, m_w_dw, m_b_dw, m_gn_g, m_gn_b, m_g_attn_out, m_g_conv_out, m_w_out, m_ln1_g, m_ln1_b, m_w_ff1, m_w_ff2, m_ln2_g, m_ln2_b, v_w_ada, v_b_ada, v_w_in, v_b_forget, v_w_dw, v_b_dw, v_gn_g, v_gn_b, v_g_attn_out, v_g_conv_out, v_w_out, v_ln1_g, v_ln1_b, v_w_ff1, v_w_ff2, v_ln2_g, v_ln2_b):
    seq = x.shape[1]
    ts = min(512, seq // 2)
    tq = min(512, seq // 2)
    ts_mid = min(256, seq // 2)
    q_idx = 2 * lax.axis_index("x") + lax.axis_index("y")
    xs = x[0]
    tgt = loss_target[0]

    w_in_sh = jnp.pad(w_in[0], ((0, 0), (0, IN_SHARD_PAD - IN_SHARD))).astype(BF16)
    wdw_rows = jnp.pad(w_dw[0, :, 0, :], ((0, 1), (0, 0)))
    sc_all, ada, (win_all, wdw_all) = _ada_fwd(c, w_ada[0], b_ada, [w_in_sh, wdw_rows])
    w_in_full = jnp.transpose(win_all[:, :, :IN_SHARD], (1, 0, 2)).reshape(D_MODEL, N_IN)
    w_p = jnp.concatenate([w_in_full[:, 0:1536], w_in_full[:, 1544:2568], w_in_full[:, 1536:1544],
                           jnp.zeros((D_MODEL, 120), BF16)], axis=1)
    bf = _pad_lanes(b_forget, 128)
    wdw_full = lax.reduce_precision(jnp.transpose(wdw_all, (1, 0, 2)).reshape(32, 512), 8, 7)

    prm = jnp.concatenate([b_dw, gn_g, gn_b, g_attn_out, g_conv_out, jnp.zeros((3, 512), F32)], axis=0)
    ln1 = jnp.concatenate([ln1_g, ln1_b], axis=0)
    ln2 = jnp.concatenate([ln2_g, ln2_b], axis=0)
    ch = jnp.arange(512)
    gm = ((ch[:, None] // HEAD_DIM == ch[None, :] // HEAD_DIM).astype(F32) / HEAD_DIM).astype(BF16)
    sel = (ch[:, None] // HEAD_DIM == jnp.arange(128)[None, :]).astype(BF16)

    qa, ka, va, a, g, logf = _inproj_fwd(xs, ada, w_p, bf, ts)
    o, qb, (wout_all, w1_all, w2_all) = _attn_fwd(
        qa, ka, va, [w_out[0].astype(BF16), w_ff1[0].astype(BF16), w_ff2[0].astype(BF16)], tq)
    w_out_full = wout_all.reshape(D_MODEL, D_MODEL)
    x1 = _mid_fwd(a, g, o, xs, ada, wdw_full, prm, ln1, w_out_full, gm, ts_mid)
    dff, dx1, pg_f = _ffn_fwd(x1, ada, w1_all, w2_all, ln2, tgt, ts)

    dw1 = lax.empty((N_SHARD, D_MODEL, FF_CHUNK), F32)
    dw2 = lax.empty((N_SHARD, FF_CHUNK, D_MODEL), F32)
    pg_b = jnp.zeros((8, D_MODEL), F32)
    for f in range(N_SHARD):
        dx1, dw1, dw2, pg_bf = _ffn_bwd_chunk(f, x1, ada, w1_all, w2_all, dff, dx1, dw1, dw2, ts)
        pg_b = pg_b + pg_bf
    doa, da, dg, dxa, dwo, pgm, pgc, dwdw = _mid_bwd(
        a, g, o, xs, dx1, ada, wdw_full, prm, ln1, w_out_full, gm, sel, ts_mid)
    dq, drs, dk, dv, dfc, land_out, land_ff = _attn_bwd(qb, ka, va, doa, dwo.reshape(N_SHARD, 256, D_MODEL),
                                                         dw1, dw2, tq)
    gx, dwp, pgi, dbf = _inproj_bwd(xs, ada, w_p, dq, dk, dv, da, dg, dfc, drs, logf, dxa, ts)

    d_ada = jnp.concatenate([pgi[1:2], pgi[0:1], pgm[2:3], pg_b[1:2], pg_b[0:1], pg_f[2:3]], axis=0)
    packed = _pack_small(d_ada, pgm[0:1], pgm[1:2], pg_f[0:1], pg_f[1:2], pgc[0:1], pgc[1:2], pgc[2:3], pgc[3:4],
                         pgc[4:5], dbf[0:1, :N_HEADS], dwdw[0:CONV_K], pg_f[3:4])
    dw_in_cols = jnp.concatenate([dwp[:, 0:1536], dwp[:, 2560:2568], dwp[:, 1536:2560]], axis=1)
    dw_in_sh = jnp.pad(jnp.transpose(dw_in_cols.reshape(D_MODEL, N_SHARD, IN_SHARD), (1, 0, 2)),
                       ((0, 0), (0, 0), (0, IN_SHARD_PAD - IN_SHARD))).reshape(N_SHARD, 768, D_MODEL)
    small_sum, small_all, land_in = _small_reduce(packed, dw_in_sh)
    loss = small_sum[SMALL_ROWS - 1, 0]
    gsm = _unpack_small(small_sum)
    g_wdw = lax.dynamic_slice(small_sum[16:16 + CONV_K, :512], (0, q_idx * 128), (CONV_K, 128))

    zrow = jnp.zeros((CONV_K + 1, D_MODEL), F32)
    w_small = _pack_small(b_ada, ln1_g, ln1_b, ln2_g, ln2_b, b_dw, gn_g, gn_b, g_attn_out,
                          g_conv_out, b_forget, zrow[:CONV_K, :512], zrow[0])
    m_small = _pack_small(m_b_ada, m_ln1_g, m_ln1_b, m_ln2_g, m_ln2_b, m_b_dw, m_gn_g, m_gn_b, m_g_attn_out,
                          m_g_conv_out, m_b_forget, zrow[:CONV_K, :512], zrow[0])
    v_small = _pack_small(v_b_ada, v_ln1_g, v_ln1_b, v_ln2_g, v_ln2_b, v_b_dw, v_gn_g, v_gn_b, v_g_attn_out,
                          v_g_conv_out, v_b_forget, zrow[:CONV_K, :512], zrow[0])
    d_small, mn_small, vn_small = (_unpack_small(t) for t in _adamw(small_sum, w_small, m_small, v_small, "adamw_small"))
    d_wdw, mn_wdw, vn_wdw = _adamw(g_wdw, w_dw[0, :, 0, :], m_w_dw[0, :, 0, :], v_w_dw[0, :, 0, :], "adamw_wdw")

    dd = lax.dynamic_slice(small_all[:, 0:6, :].reshape(8, 6 * D_MODEL), (0, q_idx * 1536), (8, 1536))
    g_wada, d_wada, mn_wada, vn_wada = _w_ada_update(sc_all.T, dd, w_ada[0], m_w_ada[0], v_w_ada[0])

    part = _sum_chips(land_in, land_out, land_ff)
    total = _add_pair(part, _core_swap(part))
    g_win = total[0:768].reshape(D_MODEL, IN_SHARD_PAD)[:, :IN_SHARD]
    g_wout = total[768:1024]
    g_w1 = total[1024:2048]
    g_w2 = total[2048:3072]
    d_win, mn_win, vn_win = _adamw(g_win, w_in[0], m_w_in[0], v_w_in[0], "adamw_w_in")
    d_wout, mn_wout, vn_wout = _adamw(g_wout, w_out[0], m_w_out[0], v_w_out[0], "adamw_w_out")
    d_w1, mn_w1, vn_w1 = _adamw(g_w1, w_ff1[0], m_w_ff1[0], v_w_ff1[0], "adamw_w_ff1")
    d_w2, mn_w2, vn_w2 = _adamw(g_w2, w_ff2[0], m_w_ff2[0], v_w_ff2[0], "adamw_w_ff2")

    def group(wada, sm, win, wdw, wout, w1, w2):
        return (wada[None], sm["b_ada"], win[None], sm["b_forget"], wdw[None, :, None, :], sm["b_dw"], sm["gn_g"],
                sm["gn_b"], sm["g_attn_out"], sm["g_conv_out"], wout[None], sm["ln1_g"], sm["ln1_b"], w1[None],
                w2[None], sm["ln2_g"], sm["ln2_b"])

    return ((loss, gx[None])
            + group(g_wada, gsm, g_win, g_wdw, g_wout, g_w1, g_w2)
            + group(d_wada, d_small, d_win, d_wdw, d_wout, d_w1, d_w2)
            + group(mn_wada, mn_small, mn_win, mn_wdw, mn_wout, mn_w1, mn_w2)
            + group(vn_wada, vn_small, vn_win, vn_wdw, vn_wout, vn_w1, vn_w2))
```
